```python
import math
import jax, jax.numpy as jnp
from jax import lax
import numpy as np

D_MODEL = 2048
BATCH = 2
SEQ = 8192
DEPTH = 1

D_RNN = D_MODEL
RG_HEADS = 16
RG_HEAD_DIM = D_RNN // RG_HEADS
CONV_WIDTH = 4
RG_C = 8.0
D_SSM = D_MODEL // 2
SSM_GROUP = 16
SSM_GROUPS = D_SSM // SSM_GROUP
SSM_STATE = 64
D_FF = 4 * D_MODEL
D_IN = 2 * D_RNN + D_SSM + 2 * D_MODEL
LN_EPS = 1e-5

kernel_name = "hybrid_rglru_s5_gated_deepnorm_block"


def _layernorm(x, g, b):
    xf = x.astype(jnp.float32)
    mu = jnp.mean(xf, axis=-1, keepdims=True)
    var = jnp.mean(jnp.square(xf - mu), axis=-1, keepdims=True)
    y = (xf - mu) * lax.rsqrt(var + LN_EPS)
    return (y * g.astype(jnp.float32) + b.astype(jnp.float32)).astype(x.dtype)


def _real_linear_scan(a, b):
    def combine(c1, c2):
        a1, b1 = c1
        a2, b2 = c2
        return a1 * a2, a2 * b1 + b2
    _, h = lax.associative_scan(combine, (a, b), axis=1)
    return h


def _complex_linear_scan(a_re, a_im, b_re, b_im):
    def combine(c1, c2):
        a1r, a1i, b1r, b1i = c1
        a2r, a2i, b2r, b2i = c2
        ar = a2r * a1r - a2i * a1i
        ai = a2r * a1i + a2i * a1r
        br = a2r * b1r - a2i * b1i + b2r
        bi = a2r * b1i + a2i * b1r + b2i
        return ar, ai, br, bi
    _, _, h_re, h_im = lax.associative_scan(combine, (a_re, a_im, b_re, b_im), axis=1)
    return h_re, h_im


def _causal_depthwise_conv(x, w, bias):
    c = x.shape[-1]
    y = lax.conv_general_dilated(
        x, w[:, None, :].astype(x.dtype), window_strides=(1,),
        padding=[(CONV_WIDTH - 1, 0)], dimension_numbers=("NWC", "WIO", "NWC"),
        feature_group_count=c)
    return y + bias


def _rglru_branch(xr, gate, conv_w, conv_b, wa, ba, wx, bx, lam, w_a_out):
    bsz, s, _ = xr.shape
    xc = _causal_depthwise_conv(xr, conv_w, conv_b)
    xh = xc.reshape(bsz, s, RG_HEADS, RG_HEAD_DIM)
    r = jax.nn.sigmoid(jnp.einsum("bshi,hij->bshj", xh, wa) + ba).reshape(bsz, s, D_RNN)
    i = jax.nn.sigmoid(jnp.einsum("bshi,hij->bshj", xh, wx) + bx).reshape(bsz, s, D_RNN)
    log_a = (-RG_C * r.astype(jnp.float32)) * jax.nn.softplus(-lam.astype(jnp.float32))
    a = jnp.exp(log_a)
    mult = jnp.sqrt(-jnp.expm1(2.0 * log_a))
    b = mult * (i.astype(jnp.float32) * xc.astype(jnp.float32))
    h = _real_linear_scan(a, b).astype(xr.dtype)
    return (h * jax.nn.gelu(gate)) @ w_a_out


def _s5_branch(u, a_re, a_im, log_dt, b_re, b_im, c_re, c_im, d, glu_w, glu_v):
    bsz, s, _ = u.shape
    uf = u.astype(jnp.float32).reshape(bsz, s, SSM_GROUPS, SSM_GROUP)
    dt = jnp.exp(log_dt.astype(jnp.float32))[:, None]
    lr = jnp.minimum(a_re.astype(jnp.float32), -1e-4)
    li = a_im.astype(jnp.float32)
    mag = jnp.exp(lr * dt)
    lbr = mag * jnp.cos(li * dt)
    lbi = mag * jnp.sin(li * dt)
    zr, zi = lbr - 1.0, lbi
    den = lr * lr + li * li
    fr = (zr * lr + zi * li) / den
    fi = (zi * lr - zr * li) / den
    br32, bi32 = b_re.astype(jnp.float32), b_im.astype(jnp.float32)
    bbr = fr[..., None] * br32 - fi[..., None] * bi32
    bbi = fr[..., None] * bi32 + fi[..., None] * br32
    bu_re = jnp.einsum("bsgh,gph->bsgp", uf, bbr)
    bu_im = jnp.einsum("bsgh,gph->bsgp", uf, bbi)
    shp = (1, s, SSM_GROUPS, SSM_STATE)
    h_re, h_im = _complex_linear_scan(jnp.broadcast_to(lbr, shp), jnp.broadcast_to(lbi, shp),
                                      bu_re, bu_im)
    y = (jnp.einsum("bsgp,ghp->bsgh", h_re, c_re.astype(jnp.float32))
         - jnp.einsum("bsgp,ghp->bsgh", h_im, c_im.astype(jnp.float32))
         + d.astype(jnp.float32) * uf)
    y = jax.nn.gelu(y.reshape(bsz, s, D_SSM)).astype(u.dtype)
    return (y @ glu_w) * jax.nn.sigmoid(y @ glu_v)


def setup_inputs(seed: int = 0) -> dict:
    key = jax.random.key(seed)
    ks = jax.random.split(key, 32)
    L = DEPTH
    beta = (8.0 * DEPTH) ** -0.25

    def nrm(k, shape, scale):
        return jax.random.normal(k, shape, jnp.float32) * scale

    x = nrm(ks[0], (BATCH, SEQ, D_MODEL), 1.0)
    w_in = nrm(ks[1], (L, D_MODEL, D_IN), D_MODEL ** -0.5)
    conv_w = nrm(ks[2], (L, CONV_WIDTH, D_RNN), CONV_WIDTH ** -0.5)
    conv_b = nrm(ks[3], (L, D_RNN), 0.01)
    rg_wa = nrm(ks[4], (L, RG_HEADS, RG_HEAD_DIM, RG_HEAD_DIM), RG_HEAD_DIM ** -0.5)
    rg_ba = nrm(ks[5], (L, RG_HEADS, RG_HEAD_DIM), 0.01)
    rg_wx = nrm(ks[6], (L, RG_HEADS, RG_HEAD_DIM, RG_HEAD_DIM), RG_HEAD_DIM ** -0.5)
    rg_bx = nrm(ks[7], (L, RG_HEADS, RG_HEAD_DIM), 0.01)
    a_c = jax.random.uniform(ks[8], (L, D_RNN), jnp.float32, 0.9, 0.999)
    a0 = a_c ** (1.0 / RG_C)
    rg_lambda = jnp.log(a0) - jnp.log1p(-a0)
    w_a_out = nrm(ks[9], (L, D_RNN, D_MODEL), D_RNN ** -0.5)
    n = jnp.arange(SSM_STATE, dtype=jnp.float32)
    ssm_a_re = -0.5 + nrm(ks[10], (L, SSM_GROUPS, SSM_STATE), 0.01)
    ssm_a_im = math.pi * n + nrm(ks[11], (L, SSM_GROUPS, SSM_STATE), 0.01)
    ssm_log_dt = jax.random.uniform(ks[12], (L, SSM_GROUPS), jnp.float32,
                                    math.log(1e-3), math.log(1e-1))
    ssm_b_re = nrm(ks[13], (L, SSM_GROUPS, SSM_STATE, SSM_GROUP), (2.0 * SSM_GROUP) ** -0.5)
    ssm_b_im = nrm(ks[14], (L, SSM_GROUPS, SSM_STATE, SSM_GROUP), (2.0 * SSM_GROUP) ** -0.5)
    ssm_c_re = nrm(ks[15], (L, SSM_GROUPS, SSM_GROUP, SSM_STATE), (0.5 * SSM_STATE) ** -0.5)
    ssm_c_im = nrm(ks[16], (L, SSM_GROUPS, SSM_GROUP, SSM_STATE), (0.5 * SSM_STATE) ** -0.5)
    ssm_d = nrm(ks[17], (L, SSM_GROUPS, SSM_GROUP), 1.0)
    glu_w = nrm(ks[18], (L, D_SSM, D_MODEL), D_SSM ** -0.5)
    glu_v = nrm(ks[19], (L, D_SSM, D_MODEL), D_SSM ** -0.5)
    w_out = nrm(ks[20], (L, D_MODEL, D_MODEL), beta * D_MODEL ** -0.5)
    ln1_g = 1.0 + nrm(ks[21], (L, D_MODEL), 0.02)
    ln1_b = nrm(ks[22], (L, D_MODEL), 0.02)
    mlp_w_up = nrm(ks[23], (L, D_MODEL, D_FF), beta * D_MODEL ** -0.5)
    mlp_b_up = nrm(ks[24], (L, D_FF), 0.01)
    mlp_w_down = nrm(ks[25], (L, D_FF, D_MODEL), beta * D_FF ** -0.5)
    mlp_b_down = nrm(ks[26], (L, D_MODEL), 0.01)
    ln2_g = 1.0 + nrm(ks[27], (L, D_MODEL), 0.02)
    ln2_b = nrm(ks[28], (L, D_MODEL), 0.02)
    return {"x": x, "w_in": w_in, "conv_w": conv_w, "conv_b": conv_b,
            "rg_wa": rg_wa, "rg_ba": rg_ba, "rg_wx": rg_wx, "rg_bx": rg_bx,
            "rg_lambda": rg_lambda, "w_a_out": w_a_out,
            "ssm_a_re": ssm_a_re, "ssm_a_im": ssm_a_im, "ssm_log_dt": ssm_log_dt,
            "ssm_b_re": ssm_b_re, "ssm_b_im": ssm_b_im, "ssm_c_re": ssm_c_re,
            "ssm_c_im": ssm_c_im, "ssm_d": ssm_d, "glu_w": glu_w, "glu_v": glu_v,
            "w_out": w_out, "ln1_g": ln1_g, "ln1_b": ln1_b,
            "mlp_w_up": mlp_w_up, "mlp_b_up": mlp_b_up, "mlp_w_down": mlp_w_down,
            "mlp_b_down": mlp_b_down, "ln2_g": ln2_g, "ln2_b": ln2_b}


def reference(x, w_in, conv_w, conv_b, rg_wa, rg_ba, rg_wx, rg_bx, rg_lambda, w_a_out,
              ssm_a_re, ssm_a_im, ssm_log_dt, ssm_b_re, ssm_b_im, ssm_c_re, ssm_c_im,
              ssm_d, glu_w, glu_v, w_out, ln1_g, ln1_b, mlp_w_up, mlp_b_up,
              mlp_w_down, mlp_b_down, ln2_g, ln2_b):
    alpha = (2.0 * DEPTH) ** 0.25
    splits = [D_RNN, 2 * D_RNN, 2 * D_RNN + D_SSM, 2 * D_RNN + D_SSM + D_MODEL]
    for l in range(DEPTH):
        z = x @ w_in[l]
        xr, gate_r, u_s, g_a, g_b = jnp.split(z, splits, axis=-1)
        y_a = _rglru_branch(xr, gate_r, conv_w[l], conv_b[l], rg_wa[l], rg_ba[l],
                            rg_wx[l], rg_bx[l], rg_lambda[l], w_a_out[l])
        y_b = _s5_branch(u_s, ssm_a_re[l], ssm_a_im[l], ssm_log_dt[l], ssm_b_re[l],
                         ssm_b_im[l], ssm_c_re[l], ssm_c_im[l], ssm_d[l], glu_w[l], glu_v[l])
        mix = jax.nn.sigmoid(g_a) * y_a + jax.nn.sigmoid(g_b) * y_b
        x = _layernorm(alpha * x + mix @ w_out[l], ln1_g[l], ln1_b[l])
        h = jnp.square(jax.nn.relu(x @ mlp_w_up[l] + mlp_b_up[l])) @ mlp_w_down[l] + mlp_b_down[l]
        x = _layernorm(alpha * x + h, ln2_g[l], ln2_b[l])
    return x
```

```python
import functools
import math

import jax
import jax.numpy as jnp
from jax.experimental import pallas as pl
from jax.experimental.pallas import tpu as pltpu

F32 = jnp.float32
BF16 = jnp.bfloat16

D_MODEL = 2048
D_RNN = D_MODEL
RG_HEADS = 16
RG_HEAD_DIM = D_RNN // RG_HEADS
CONV_WIDTH = 4
RG_C = 8.0
D_SSM = D_MODEL // 2
SSM_GROUP = 16
SSM_GROUPS = D_SSM // SSM_GROUP
SSM_STATE = 64
D_FF = 4 * D_MODEL
D_IN = 2 * D_RNN + D_SSM + 2 * D_MODEL
LN_EPS = 1e-5

SUBLANES = 8
LANES = 128
VMEM_LIMIT = 56 * 1024 * 1024

RG_TILE = 256
S5_TILE = 256
S5_TILE_GROUPS = S5_TILE // SSM_GROUP
S5_TILE_STATES = S5_TILE_GROUPS * SSM_STATE
COL_GATE = D_RNN // RG_TILE
COL_U = 2 * D_RNN // S5_TILE
COL_GA = (2 * D_RNN + D_SSM) // 256
COL_GB = COL_GA + D_MODEL // 256


def _params(*sem):
    return pltpu.CompilerParams(dimension_semantics=sem, vmem_limit_bytes=VMEM_LIMIT)


def _matmul_kernel(x_ref, w_ref, o_ref):
    o_ref[...] = jnp.dot(x_ref[...], w_ref[...],
                         preferred_element_type=F32).astype(o_ref.dtype)


def _matmul(x, w, out_dtype, tm, tn):
    m, k = x.shape
    n = w.shape[1]
    return pl.pallas_call(
        _matmul_kernel,
        grid=(m // tm, n // tn),
        in_specs=[pl.BlockSpec((tm, k), lambda i, j: (i, 0)),
                  pl.BlockSpec((k, tn), lambda i, j: (0, j))],
        out_specs=pl.BlockSpec((tm, tn), lambda i, j: (i, j)),
        out_shape=jax.ShapeDtypeStruct((m, n), out_dtype),
        compiler_params=_params("parallel", "arbitrary"),
        name="in_proj",
    )(x, w)


def _shift_rows(v, d, fill):
    row = jax.lax.broadcasted_iota(jnp.int32, v.shape, 0)
    return jnp.where(row >= d, pltpu.roll(v, d, 0), fill)


def _rglru_kernel(x_ref, gate_ref, cw_ref, cb_ref, w_ref, ba_ref, bx_ref, lam_ref,
                  o_ref, ext_ref, a_ref, b_ref, carry_ref, *, tt):
    t = pl.program_id(2)

    @pl.when(t == 0)
    def _():
        ext_ref[0:SUBLANES, :] = jnp.zeros((SUBLANES, RG_TILE), F32)
        carry_ref[...] = jnp.zeros_like(carry_ref)

    x = x_ref[...]
    ext_ref[SUBLANES:SUBLANES + tt, :] = x
    xc = cb_ref[...] + cw_ref[3:4, :] * x
    for k in range(CONV_WIDTH - 1):
        shift = CONV_WIDTH - 1 - k
        xc = xc + cw_ref[k:k + 1, :] * ext_ref[SUBLANES - shift:SUBLANES - shift + tt, :]
    ext_ref[0:SUBLANES, :] = x[tt - SUBLANES:, :]

    lam = lam_ref[...]
    neg = -lam
    softplus = jnp.maximum(neg, 0.0) + jnp.log1p(jnp.exp(-jnp.abs(neg)))
    cvec = -RG_C * softplus

    xcb = xc.astype(BF16)
    for hh in range(RG_TILE // RG_HEAD_DIM):
        sl = slice(hh * RG_HEAD_DIM, (hh + 1) * RG_HEAD_DIM)
        pre = jnp.dot(xcb[:, sl], w_ref[hh], preferred_element_type=F32)
        r = jax.nn.sigmoid(pre[:, :RG_HEAD_DIM] + ba_ref[:, sl])
        i = jax.nn.sigmoid(pre[:, RG_HEAD_DIM:] + bx_ref[:, sl])
        log_a = cvec[:, sl] * r
        a = jnp.exp(log_a)
        a_ref[:, sl] = a
        one_minus_a2 = -jnp.tanh(log_a) * (1.0 + a * a)
        b_ref[:, sl] = jnp.sqrt(one_minus_a2) * (i * xc[:, sl])

    def group(g, carry):
        rows = pl.ds(pl.multiple_of(g * SUBLANES, SUBLANES), SUBLANES)
        a = a_ref[rows, :]
        b = b_ref[rows, :]
        for d in (1, 2, 4):
            b = a * _shift_rows(b, d, 0.0) + b
            a = a * _shift_rows(a, d, 1.0)
        h = b + a * carry
        b_ref[rows, :] = h
        return h[SUBLANES - 1:SUBLANES, :]

    carry_ref[...] = jax.lax.fori_loop(0, tt // SUBLANES, group, carry_ref[...],
                                       unroll=4)
    o_ref[...] = (b_ref[...] * jax.nn.gelu(gate_ref[...])).astype(o_ref.dtype)


def _rglru(z, conv_w, conv_b, w_gate, ba, bx, lam, bsz, seq, tt):
    n_t = seq // tt
    row = lambda b, c, t: (b * n_t + t, c)
    vec = lambda b, c, t: (0, c)
    return pl.pallas_call(
        functools.partial(_rglru_kernel, tt=tt),
        grid=(bsz, D_RNN // RG_TILE, n_t),
        in_specs=[pl.BlockSpec((tt, RG_TILE), row),
                  pl.BlockSpec((tt, RG_TILE), lambda b, c, t: (b * n_t + t, COL_GATE + c)),
                  pl.BlockSpec((CONV_WIDTH, RG_TILE), vec),
                  pl.BlockSpec((1, RG_TILE), vec),
                  pl.BlockSpec((RG_TILE // RG_HEAD_DIM, RG_HEAD_DIM, 2 * RG_HEAD_DIM),
                               lambda b, c, t: (c, 0, 0)),
                  pl.BlockSpec((1, RG_TILE), vec),
                  pl.BlockSpec((1, RG_TILE), vec),
                  pl.BlockSpec((1, RG_TILE), vec)],
        out_specs=pl.BlockSpec((tt, RG_TILE), row),
        out_shape=jax.ShapeDtypeStruct((bsz * seq, D_RNN), BF16),
        scratch_shapes=[pltpu.VMEM((tt + SUBLANES, RG_TILE), F32),
                        pltpu.VMEM((tt, RG_TILE), F32),
                        pltpu.VMEM((tt, RG_TILE), F32),
                        pltpu.VMEM((1, RG_TILE), F32)],
        compiler_params=_params("parallel", "parallel", "arbitrary"),
        name="rglru",
    )(z, z, conv_w, conv_b, w_gate, ba, bx, lam)


def _s5_discretize_kernel(are_ref, aim_ref, ldt_ref, lbr_ref, lbi_ref, fr_ref, fi_ref):
    dt = jnp.exp(ldt_ref[...])
    lr = jnp.minimum(are_ref[...], -1e-4)
    li = aim_ref[...]
    mag = jnp.exp(lr * dt)
    lbr = mag * jnp.cos(li * dt)
    lbi = mag * jnp.sin(li * dt)
    zr, zi = lbr - 1.0, lbi
    den = lr * lr + li * li
    lbr_ref[...] = lbr
    lbi_ref[...] = lbi
    fr_ref[...] = (zr * lr + zi * li) / den
    fi_ref[...] = (zi * lr - zr * li) / den


def _s5_discretize(a_re, a_im, log_dt):
    shp = jax.ShapeDtypeStruct((SSM_GROUPS, SSM_STATE), F32)
    return pl.pallas_call(
        _s5_discretize_kernel,
        out_shape=(shp, shp, shp, shp),
        name="s5_discretize",
    )(a_re, a_im, log_dt.reshape(SSM_GROUPS, 1))


def _cmul(ar, ai, br, bi):
    return ar * br - ai * bi, ar * bi + ai * br


def _s5_kernel(u_ref, bmat_ref, cmat_ref, lre_ref, lim_ref, d_ref, o_ref,
               hr_ref, hi_ref, pow_re_ref, pow_im_ref, step_re_ref, step_im_ref,
               cr_ref, ci_ref, *, tt):
    t = pl.program_id(2)
    ns = S5_TILE_STATES

    @pl.when(t == 0)
    def _():
        cr_ref[...] = jnp.zeros_like(cr_ref)
        ci_ref[...] = jnp.zeros_like(ci_ref)
        lr, li = lre_ref[...], lim_ref[...]
        pr, pi = lr, li
        row = jax.lax.broadcasted_iota(jnp.int32, (SUBLANES, ns), 0)
        for r in range(SUBLANES):
            pow_re_ref[r:r + 1, :] = pr
            pow_im_ref[r:r + 1, :] = pi
            if r + 1 in (1, 2, 4):
                k = (1, 2, 4).index(r + 1)
                step_re_ref[k] = jnp.where(row >= r + 1, pr, 0.0)
                step_im_ref[k] = jnp.where(row >= r + 1, pi, 0.0)
            pr, pi = _cmul(pr, pi, lr, li)

    u = u_ref[...]
    bu = jnp.dot(u.astype(BF16), bmat_ref[0], preferred_element_type=F32)
    hr_ref[...] = bu[:, :ns]
    hi_ref[...] = bu[:, ns:]

    def group(g, carry):
        cr, ci = carry
        rows = pl.ds(pl.multiple_of(g * SUBLANES, SUBLANES), SUBLANES)
        hr = hr_ref[rows, :]
        hi = hi_ref[rows, :]
        for k, d in enumerate((1, 2, 4)):
            sr, si = pltpu.roll(hr, d, 0), pltpu.roll(hi, d, 0)
            mr, mi = _cmul(step_re_ref[k], step_im_ref[k], sr, si)
            hr, hi = hr + mr, hi + mi
        mr, mi = _cmul(pow_re_ref[...], pow_im_ref[...], cr, ci)
        hr, hi = hr + mr, hi + mi
        hr_ref[rows, :] = hr
        hi_ref[rows, :] = hi
        return hr[SUBLANES - 1:SUBLANES, :], hi[SUBLANES - 1:SUBLANES, :]

    cr, ci = jax.lax.fori_loop(0, tt // SUBLANES, group, (cr_ref[...], ci_ref[...]),
                               unroll=2)
    cr_ref[...] = cr
    ci_ref[...] = ci

    y = (jnp.dot(hr_ref[...].astype(BF16), cmat_ref[0, :ns, :], preferred_element_type=F32)
         + jnp.dot(hi_ref[...].astype(BF16), cmat_ref[0, ns:, :], preferred_element_type=F32)
         + d_ref[...] * u)
    o_ref[...] = jax.nn.gelu(y).astype(o_ref.dtype)


def _s5(z, bmat, cmat, lam_re, lam_im, dvec, bsz, seq, tt):
    n_t = seq // tt
    n_tiles = D_SSM // S5_TILE
    ns = S5_TILE_STATES
    vec = lambda b, c, t: (0, c)
    return pl.pallas_call(
        functools.partial(_s5_kernel, tt=tt),
        grid=(bsz, n_tiles, n_t),
        in_specs=[pl.BlockSpec((tt, S5_TILE), lambda b, c, t: (b * n_t + t, COL_U + c)),
                  pl.BlockSpec((1, S5_TILE, 2 * ns), lambda b, c, t: (c, 0, 0)),
                  pl.BlockSpec((1, 2 * ns, S5_TILE), lambda b, c, t: (c, 0, 0)),
                  pl.BlockSpec((1, ns), vec),
                  pl.BlockSpec((1, ns), vec),
                  pl.BlockSpec((1, S5_TILE), vec)],
        out_specs=pl.BlockSpec((tt, S5_TILE), lambda b, c, t: (b * n_t + t, c)),
        out_shape=jax.ShapeDtypeStruct((bsz * seq, D_SSM), BF16),
        scratch_shapes=[pltpu.VMEM((tt, ns), F32),
                        pltpu.VMEM((tt, ns), F32),
                        pltpu.VMEM((SUBLANES, ns), F32),
                        pltpu.VMEM((SUBLANES, ns), F32),
                        pltpu.VMEM((3, SUBLANES, ns), F32),
                        pltpu.VMEM((3, SUBLANES, ns), F32),
                        pltpu.VMEM((1, ns), F32),
                        pltpu.VMEM((1, ns), F32)],
        compiler_params=_params("parallel", "parallel", "arbitrary"),
        name="s5",
    )(z, bmat, cmat, lam_re, lam_im, dvec)


def _mix_kernel(hg_ref, y_ref, ga_ref, gb_ref, wa_ref, gw_ref, gv_ref, o_ref):
    y_a = jnp.dot(hg_ref[...], wa_ref[...], preferred_element_type=F32)
    y = y_ref[...]
    y_b = (jnp.dot(y, gw_ref[...], preferred_element_type=F32)
           * jax.nn.sigmoid(jnp.dot(y, gv_ref[...], preferred_element_type=F32)))
    mix = jax.nn.sigmoid(ga_ref[...]) * y_a + jax.nn.sigmoid(gb_ref[...]) * y_b
    o_ref[...] = mix.astype(o_ref.dtype)


def _mix(hg, y, z, w_a, glu_w, glu_v, tm, tn):
    m = hg.shape[0]
    nb = tn // 256
    return pl.pallas_call(
        _mix_kernel,
        grid=(m // tm, D_MODEL // tn),
        in_specs=[pl.BlockSpec((tm, D_RNN), lambda i, j: (i, 0)),
                  pl.BlockSpec((tm, D_SSM), lambda i, j: (i, 0)),
                  pl.BlockSpec((tm, tn), lambda i, j: (i, COL_GA // nb + j)),
                  pl.BlockSpec((tm, tn), lambda i, j: (i, COL_GB // nb + j)),
                  pl.BlockSpec((D_RNN, tn), lambda i, j: (0, j)),
                  pl.BlockSpec((D_SSM, tn), lambda i, j: (0, j)),
                  pl.BlockSpec((D_SSM, tn), lambda i, j: (0, j))],
        out_specs=pl.BlockSpec((tm, tn), lambda i, j: (i, j)),
        out_shape=jax.ShapeDtypeStruct((m, D_MODEL), BF16),
        compiler_params=_params("parallel", "arbitrary"),
        name="mix",
    )(hg, y, z, z, w_a, glu_w, glu_v)


def _layernorm(v, g, b):
    mu = jnp.mean(v, axis=-1, keepdims=True)
    c = v - mu
    var = jnp.mean(c * c, axis=-1, keepdims=True)
    return c * jax.lax.rsqrt(var + LN_EPS) * g + b


def _outproj_ln_kernel(mix_ref, x_ref, w_ref, g_ref, b_ref, o_ref, ob_ref, *, alpha):
    v = alpha * x_ref[...] + jnp.dot(mix_ref[...], w_ref[...], preferred_element_type=F32)
    out = _layernorm(v, g_ref[...], b_ref[...])
    o_ref[...] = out
    ob_ref[...] = out.astype(BF16)


def _outproj_ln(mix, x, w_out, g, b, alpha, tm):
    m = mix.shape[0]
    row = lambda i: (i, 0)
    fixed = lambda i: (0, 0)
    return pl.pallas_call(
        functools.partial(_outproj_ln_kernel, alpha=alpha),
        grid=(m // tm,),
        in_specs=[pl.BlockSpec((tm, D_MODEL), row),
                  pl.BlockSpec((tm, D_MODEL), row),
                  pl.BlockSpec((D_MODEL, D_MODEL), fixed),
                  pl.BlockSpec((1, D_MODEL), fixed),
                  pl.BlockSpec((1, D_MODEL), fixed)],
        out_specs=(pl.BlockSpec((tm, D_MODEL), row), pl.BlockSpec((tm, D_MODEL), row)),
        out_shape=(jax.ShapeDtypeStruct((m, D_MODEL), F32),
                   jax.ShapeDtypeStruct((m, D_MODEL), BF16)),
        compiler_params=_params("parallel"),
        name="outproj_ln",
    )(mix, x, w_out, g, b)


def _mlp_up_kernel(x_ref, w_ref, b_ref, o_ref):
    v = jnp.dot(x_ref[...], w_ref[...], preferred_element_type=F32) + b_ref[...]
    v = jnp.maximum(v, 0.0)
    o_ref[...] = (v * v).astype(o_ref.dtype)


def _mlp_up(xb, w_up, b_up, tm, tn):
    m = xb.shape[0]
    return pl.pallas_call(
        _mlp_up_kernel,
        grid=(m // tm, D_FF // tn),
        in_specs=[pl.BlockSpec((tm, D_MODEL), lambda i, j: (i, 0)),
                  pl.BlockSpec((D_MODEL, tn), lambda i, j: (0, j)),
                  pl.BlockSpec((1, tn), lambda i, j: (0, j))],
        out_specs=pl.BlockSpec((tm, tn), lambda i, j: (i, j)),
        out_shape=jax.ShapeDtypeStruct((m, D_FF), BF16),
        compiler_params=_params("parallel", "arbitrary"),
        name="mlp_up",
    )(xb, w_up, b_up)


def _mlp_down_ln_kernel(a_ref, w_ref, x_ref, bd_ref, g_ref, b_ref, o_ref, acc_ref, *, alpha):
    k = pl.program_id(1)

    @pl.when(k == 0)
    def _():
        acc_ref[...] = jnp.zeros_like(acc_ref)

    acc_ref[...] += jnp.dot(a_ref[...], w_ref[...], preferred_element_type=F32)

    @pl.when(k == pl.num_programs(1) - 1)
    def _():
        v = alpha * x_ref[...] + acc_ref[...] + bd_ref[...]
        o_ref[...] = _layernorm(v, g_ref[...], b_ref[...])


def _mlp_down_ln(a, w_down, x1, b_down, g, b, alpha, tm, tk):
    m = a.shape[0]
    fixed = lambda i, k: (0, 0)
    return pl.pallas_call(
        functools.partial(_mlp_down_ln_kernel, alpha=alpha),
        grid=(m // tm, D_FF // tk),
        in_specs=[pl.BlockSpec((tm, tk), lambda i, k: (i, k)),
                  pl.BlockSpec((tk, D_MODEL), lambda i, k: (k, 0)),
                  pl.BlockSpec((tm, D_MODEL), lambda i, k: (i, 0)),
                  pl.BlockSpec((1, D_MODEL), fixed),
                  pl.BlockSpec((1, D_MODEL), fixed),
                  pl.BlockSpec((1, D_MODEL), fixed)],
        out_specs=pl.BlockSpec((tm, D_MODEL), lambda i, k: (i, 0)),
        out_shape=jax.ShapeDtypeStruct((m, D_MODEL), F32),
        scratch_shapes=[pltpu.VMEM((tm, D_MODEL), F32)],
        compiler_params=_params("parallel", "arbitrary"),
        name="mlp_down_ln",
    )(a, w_down, x1, b_down, g, b)


def _s5_block_matrices(fr, fi, b_re, b_im, c_re, c_im):
    bbr = fr[..., None] * b_re - fi[..., None] * b_im
    bbi = fr[..., None] * b_im + fi[..., None] * b_re
    n_tiles = D_SSM // S5_TILE
    eye = jnp.eye(S5_TILE_GROUPS, dtype=F32)

    def in_blocks(w):
        w = w.reshape(n_tiles, S5_TILE_GROUPS, SSM_STATE, SSM_GROUP)
        m = jnp.einsum("tgph,gk->tghkp", w, eye)
        return m.reshape(n_tiles, S5_TILE, S5_TILE_STATES)

    def out_blocks(w):
        w = w.reshape(n_tiles, S5_TILE_GROUPS, SSM_GROUP, SSM_STATE)
        m = jnp.einsum("tghp,gk->tgpkh", w, eye)
        return m.reshape(n_tiles, S5_TILE_STATES, S5_TILE)

    bmat = jnp.concatenate([in_blocks(bbr), in_blocks(bbi)], axis=2).astype(BF16)
    cmat = jnp.concatenate([out_blocks(c_re), out_blocks(-c_im)], axis=1).astype(BF16)
    return bmat, cmat


def kernel(x, w_in, conv_w, conv_b, rg_wa, rg_ba, rg_wx, rg_bx, rg_lambda, w_a_out, ssm_a_re, ssm_a_im, ssm_log_dt, ssm_b_re, ssm_b_im, ssm_c_re, ssm_c_im, ssm_d, glu_w, glu_v, w_out, ln1_g, ln1_b, mlp_w_up, mlp_b_up, mlp_w_down, mlp_b_down, ln2_g, ln2_b):
    bsz, seq, _ = x.shape
    m = bsz * seq
    depth = w_in.shape[0]
    alpha = (2.0 * depth) ** 0.25
    for l in range(depth):
        xf = x.reshape(m, D_MODEL)
        z = _matmul(xf.astype(BF16), w_in[l].astype(BF16), F32, tm=1024, tn=1024)

        w_gate = jnp.concatenate([rg_wa[l], rg_wx[l]], axis=-1).astype(BF16)
        hg = _rglru(z, conv_w[l], conv_b[l].reshape(1, D_RNN), w_gate,
                    rg_ba[l].reshape(1, D_RNN), rg_bx[l].reshape(1, D_RNN),
                    rg_lambda[l].reshape(1, D_RNN), bsz, seq, tt=512)

        lbr, lbi, fr, fi = _s5_discretize(ssm_a_re[l], ssm_a_im[l], ssm_log_dt[l])
        bmat, cmat = _s5_block_matrices(fr, fi, ssm_b_re[l], ssm_b_im[l],
                                        ssm_c_re[l], ssm_c_im[l])
        y = _s5(z, bmat, cmat, lbr.reshape(1, -1), lbi.reshape(1, -1),
                ssm_d[l].reshape(1, D_SSM), bsz, seq, tt=512)

        mix = _mix(hg, y, z, w_a_out[l].astype(BF16), glu_w[l].astype(BF16),
                   glu_v[l].astype(BF16), tm=1024, tn=512)
        x1, x1b = _outproj_ln(mix, xf, w_out[l].astype(BF16),
                              ln1_g[l].reshape(1, D_MODEL), ln1_b[l].reshape(1, D_MODEL),
                              alpha, tm=512)
        a = _mlp_up(x1b, mlp_w_up[l].astype(BF16), mlp_b_up[l].reshape(1, D_FF),
                    tm=1024, tn=1024)
        x2 = _mlp_down_ln(a, mlp_w_down[l].astype(BF16), x1,
                          mlp_b_down[l].reshape(1, D_MODEL),
                          ln2_g[l].reshape(1, D_MODEL), ln2_b[l].reshape(1, D_MODEL),
                          alpha, tm=512, tk=1024)
        x = x2.reshape(bsz, seq, D_MODEL)
    return x
```

```python
import functools

import numpy as np
import jax
import jax.numpy as jnp
from jax.experimental import pallas as pl
from jax.experimental.pallas import tpu as pltpu

F32 = jnp.float32
BF16 = jnp.bfloat16

D_MODEL = 2048
D_RNN = D_MODEL
RG_HEADS = 16
RG_HEAD_DIM = D_RNN // RG_HEADS
CONV_WIDTH = 4
RG_C = 8.0
D_SSM = D_MODEL // 2
SSM_GROUP = 16
SSM_GROUPS = D_SSM // SSM_GROUP
SSM_STATE = 64
D_FF = 4 * D_MODEL
D_IN = 2 * D_RNN + D_SSM + 2 * D_MODEL
LN_EPS = 1e-5

SUBLANES = 8
BF16_ROWS = 2 * SUBLANES
VMEM_LIMIT = 56 * 1024 * 1024

NSEG = SUBLANES
PB = 512
SEG = PB // NSEG
HALO = (CONV_WIDTH - 1) * SUBLANES

RG_TILE = 256
S5_TILE = 256
S5_TILE_GROUPS = S5_TILE // SSM_GROUP
S5_TILE_STATES = S5_TILE_GROUPS * SSM_STATE
COL_GATE = D_RNN // RG_TILE
COL_U = 2 * D_RNN // S5_TILE
COL_GA = (2 * D_RNN + D_SSM) // 256
COL_GB = COL_GA + D_MODEL // 256


def _params(*sem):
    return pltpu.CompilerParams(dimension_semantics=sem, vmem_limit_bytes=VMEM_LIMIT)


def _interleave_matrix():
    p = np.arange(PB)
    src = (p % NSEG) * SEG + p // NSEG
    mat = np.zeros((PB, PB), np.float32)
    mat[p, src] = 1.0
    return mat


def _row_ids(shape):
    return jax.lax.broadcasted_iota(jnp.int32, shape, 0)


def _shift_rows(v, d, fill):
    return jnp.where(_row_ids(v.shape) >= d, pltpu.roll(v, d, 0), fill)


def _in_proj_kernel(x_ref, p_ref, w_ref, o_ref, xp_ref, *, tm):
    @pl.when(pl.program_id(1) == 0)
    def _():
        for s in range(tm // PB):
            rows = slice(s * PB, (s + 1) * PB)
            xs = x_ref[rows, :].astype(BF16)
            xp_ref[rows, :] = jnp.dot(p_ref[...], xs,
                                      preferred_element_type=F32).astype(BF16)

    o_ref[...] = jnp.dot(xp_ref[...], w_ref[...],
                         preferred_element_type=F32).astype(o_ref.dtype)


def _in_proj(x, perm, w, tm, tn):
    m, k = x.shape
    n = w.shape[1]
    return pl.pallas_call(
        functools.partial(_in_proj_kernel, tm=tm),
        grid=(m // tm, n // tn),
        in_specs=[pl.BlockSpec((tm, k), lambda i, j: (i, 0)),
                  pl.BlockSpec((PB, PB), lambda i, j: (0, 0)),
                  pl.BlockSpec((k, tn), lambda i, j: (0, j))],
        out_specs=pl.BlockSpec((tm, tn), lambda i, j: (i, j)),
        out_shape=jax.ShapeDtypeStruct((m, n), BF16),
        scratch_shapes=[pltpu.VMEM((tm, k), BF16)],
        compiler_params=_params("parallel", "arbitrary"),
        name="in_proj",
    )(x, perm, w)


def _rglru_kernel(x_ref, gate_ref, cw_ref, cb_ref, w_ref, ba_ref, bx_ref, lam_ref,
                  o_ref, ext_ref, a_ref, b_ref, carry_ref):
    t = pl.program_id(2)
    grp = (SUBLANES, RG_TILE)

    @pl.when(t == 0)
    def _():
        ext_ref[PB:PB + HALO, :] = jnp.zeros((HALO, RG_TILE), F32)
        carry_ref[...] = jnp.zeros_like(carry_ref)

    x = x_ref[...].astype(F32)
    first = _row_ids(grp) == 0
    halos = []
    for k in range(CONV_WIDTH - 1):
        prev_g = ext_ref[PB + k * SUBLANES:PB + (k + 1) * SUBLANES, :]
        cur_g = x[PB - HALO + k * SUBLANES:PB - HALO + (k + 1) * SUBLANES, :]
        halos.append(jnp.where(first, pltpu.roll(prev_g, 1, 0), pltpu.roll(cur_g, 1, 0)))
    for k in range(CONV_WIDTH - 1):
        ext_ref[k * SUBLANES:(k + 1) * SUBLANES, :] = halos[k]
    ext_ref[HALO:HALO + PB, :] = x

    xc = cb_ref[...] + cw_ref[CONV_WIDTH - 1:CONV_WIDTH, :] * x
    for k in range(CONV_WIDTH - 1):
        xc = xc + cw_ref[k:k + 1, :] * ext_ref[k * SUBLANES:k * SUBLANES + PB, :]

    neg = -lam_ref[...]
    softplus = jnp.maximum(neg, 0.0) + jnp.log1p(jnp.exp(-jnp.abs(neg)))
    cvec = -RG_C * softplus

    xcb = xc.astype(BF16)
    for hh in range(RG_TILE // RG_HEAD_DIM):
        sl = slice(hh * RG_HEAD_DIM, (hh + 1) * RG_HEAD_DIM)
        pre = jnp.dot(xcb[:, sl], w_ref[hh], preferred_element_type=F32)
        r = jax.nn.sigmoid(pre[:, :RG_HEAD_DIM] + ba_ref[:, sl])
        i = jax.nn.sigmoid(pre[:, RG_HEAD_DIM:] + bx_ref[:, sl])
        log_a = cvec[:, sl] * r
        a = jnp.exp(log_a)
        a_ref[:, sl] = a
        one_minus_a2 = -jnp.tanh(log_a) * (1.0 + a * a)
        b_ref[:, sl] = jnp.sqrt(one_minus_a2) * (i * xc[:, sl])

    def step(tau, carry):
        h, acc = carry
        rows = pl.ds(pl.multiple_of(tau * SUBLANES, SUBLANES), SUBLANES)
        a = a_ref[rows, :]
        h = a * h + b_ref[rows, :]
        acc = a * acc
        b_ref[rows, :] = h
        a_ref[rows, :] = acc
        return h, acc

    h_end, a_end = jax.lax.fori_loop(
        1, SEG, step, (b_ref[0:SUBLANES, :], a_ref[0:SUBLANES, :]), unroll=8)

    for d in (1, 2, 4):
        h_end = a_end * _shift_rows(h_end, d, 0.0) + h_end
        a_end = a_end * _shift_rows(a_end, d, 1.0)
    carry = carry_ref[...]
    h_true = h_end + a_end * carry
    h_in = jnp.where(first, carry, pltpu.roll(h_true, 1, 0))
    carry_ref[...] = h_true[SUBLANES - 1:SUBLANES, :]

    h_in2 = jnp.concatenate([h_in, h_in], axis=0)

    def finish(i, _):
        rows = pl.ds(pl.multiple_of(i * BF16_ROWS, BF16_ROWS), BF16_ROWS)
        h = b_ref[rows, :] + a_ref[rows, :] * h_in2
        o_ref[rows, :] = (h * jax.nn.gelu(gate_ref[rows, :].astype(F32))).astype(o_ref.dtype)
        return 0

    jax.lax.fori_loop(0, PB // BF16_ROWS, finish, 0, unroll=4)


def _rglru(z, conv_w, conv_b, w_gate, ba, bx, lam, bsz, seq):
    n_t = seq // PB
    row = lambda b, c, t: (b * n_t + t, c)
    vec = lambda b, c, t: (0, c)
    return pl.pallas_call(
        _rglru_kernel,
        grid=(bsz, D_RNN // RG_TILE, n_t),
        in_specs=[pl.BlockSpec((PB, RG_TILE), row),
                  pl.BlockSpec((PB, RG_TILE), lambda b, c, t: (b * n_t + t, COL_GATE + c)),
                  pl.BlockSpec((CONV_WIDTH, RG_TILE), vec),
                  pl.BlockSpec((1, RG_TILE), vec),
                  pl.BlockSpec((RG_TILE // RG_HEAD_DIM, RG_HEAD_DIM, 2 * RG_HEAD_DIM),
                               lambda b, c, t: (c, 0, 0)),
                  pl.BlockSpec((1, RG_TILE), vec),
                  pl.BlockSpec((1, RG_TILE), vec),
                  pl.BlockSpec((1, RG_TILE), vec)],
        out_specs=pl.BlockSpec((PB, RG_TILE), row),
        out_shape=jax.ShapeDtypeStruct((bsz * seq, D_RNN), BF16),
        scratch_shapes=[pltpu.VMEM((HALO + PB, RG_TILE), F32),
                        pltpu.VMEM((PB, RG_TILE), F32),
                        pltpu.VMEM((PB, RG_TILE), F32),
                        pltpu.VMEM((1, RG_TILE), F32)],
        compiler_params=_params("parallel", "parallel", "arbitrary"),
        name="rglru",
    )(z, z, conv_w, conv_b, w_gate, ba, bx, lam)


def _s5_discretize_kernel(are_ref, aim_ref, ldt_ref, lbr_ref, lbi_ref, fr_ref, fi_ref):
    dt = jnp.exp(ldt_ref[...])
    lr = jnp.minimum(are_ref[...], -1e-4)
    li = aim_ref[...]
    mag = jnp.exp(lr * dt)
    lbr = mag * jnp.cos(li * dt)
    lbi = mag * jnp.sin(li * dt)
    zr, zi = lbr - 1.0, lbi
    den = lr * lr + li * li
    lbr_ref[...] = lbr
    lbi_ref[...] = lbi
    fr_ref[...] = (zr * lr + zi * li) / den
    fi_ref[...] = (zi * lr - zr * li) / den


def _s5_discretize(a_re, a_im, log_dt):
    shp = jax.ShapeDtypeStruct((SSM_GROUPS, SSM_STATE), F32)
    return pl.pallas_call(
        _s5_discretize_kernel,
        out_shape=(shp, shp, shp, shp),
        name="s5_discretize",
    )(a_re, a_im, log_dt.reshape(SSM_GROUPS, 1))


def _cmul(ar, ai, br, bi):
    return ar * br - ai * bi, ar * bi + ai * br


def _s5_kernel(u_ref, bmat_ref, cmat_ref, lre_ref, lim_ref, d_ref, o_ref,
               hr_ref, hi_ref, hb_ref, tab_re_ref, tab_im_ref,
               pow_re_ref, pow_im_ref, step_re_ref, step_im_ref, cr_ref, ci_ref):
    t = pl.program_id(2)
    ns = S5_TILE_STATES
    grp = (SUBLANES, ns)

    @pl.when(t == 0)
    def _():
        cr_ref[...] = jnp.zeros_like(cr_ref)
        ci_ref[...] = jnp.zeros_like(ci_ref)
        lr = jnp.broadcast_to(lre_ref[...], grp)
        li = jnp.broadcast_to(lim_ref[...], grp)

        def fill(tau, p):
            pr, pi = p
            rows = pl.ds(pl.multiple_of(tau * SUBLANES, SUBLANES), SUBLANES)
            tab_re_ref[rows, :] = pr
            tab_im_ref[rows, :] = pi
            return _cmul(pr, pi, lr, li)

        jax.lax.fori_loop(0, SEG, fill, (lr, li))
        mr = tab_re_ref[PB - SUBLANES:PB, :]
        mi = tab_im_ref[PB - SUBLANES:PB, :]
        row = _row_ids(grp)
        pr, pi = mr, mi
        for r in range(SUBLANES):
            pow_re_ref[r:r + 1, :] = pr[0:1, :]
            pow_im_ref[r:r + 1, :] = pi[0:1, :]
            if r + 1 in (1, 2, 4):
                k = (1, 2, 4).index(r + 1)
                step_re_ref[k] = jnp.where(row >= r + 1, pr, 0.0)
                step_im_ref[k] = jnp.where(row >= r + 1, pi, 0.0)
            pr, pi = _cmul(pr, pi, mr, mi)

    u = u_ref[...]
    bu = jnp.dot(u, bmat_ref[0], preferred_element_type=F32)
    hr_ref[...] = bu[:, :ns]
    hi_ref[...] = bu[:, ns:]

    lr = jnp.broadcast_to(lre_ref[...], grp)
    li = jnp.broadcast_to(lim_ref[...], grp)

    def step(tau, carry):
        hr, hi = carry
        rows = pl.ds(pl.multiple_of(tau * SUBLANES, SUBLANES), SUBLANES)
        nr = (lr * hr - li * hi) + hr_ref[rows, :]
        ni = (lr * hi + li * hr) + hi_ref[rows, :]
        hr_ref[rows, :] = nr
        hi_ref[rows, :] = ni
        return nr, ni

    er, ei = jax.lax.fori_loop(
        1, SEG, step, (hr_ref[0:SUBLANES, :], hi_ref[0:SUBLANES, :]), unroll=4)

    for k, d in enumerate((1, 2, 4)):
        sr, si = pltpu.roll(er, d, 0), pltpu.roll(ei, d, 0)
        mr, mi = _cmul(step_re_ref[k], step_im_ref[k], sr, si)
        er, ei = er + mr, ei + mi
    cr, ci = cr_ref[...], ci_ref[...]
    mr, mi = _cmul(pow_re_ref[...], pow_im_ref[...], cr, ci)
    er, ei = er + mr, ei + mi
    first = _row_ids(grp) == 0
    in_r = jnp.where(first, cr, pltpu.roll(er, 1, 0))
    in_i = jnp.where(first, ci, pltpu.roll(ei, 1, 0))
    cr_ref[...] = er[SUBLANES - 1:SUBLANES, :]
    ci_ref[...] = ei[SUBLANES - 1:SUBLANES, :]

    in_r2 = jnp.concatenate([in_r, in_r], axis=0)
    in_i2 = jnp.concatenate([in_i, in_i], axis=0)

    def finish(i, _):
        rows = pl.ds(pl.multiple_of(i * BF16_ROWS, BF16_ROWS), BF16_ROWS)
        fr, fi = _cmul(tab_re_ref[rows, :], tab_im_ref[rows, :], in_r2, in_i2)
        hb_ref[rows, :ns] = (hr_ref[rows, :] + fr).astype(BF16)
        hb_ref[rows, ns:] = (hi_ref[rows, :] + fi).astype(BF16)
        return 0

    jax.lax.fori_loop(0, PB // BF16_ROWS, finish, 0, unroll=2)

    y = (jnp.dot(hb_ref[...], cmat_ref[0], preferred_element_type=F32)
         + d_ref[...] * u.astype(F32))
    o_ref[...] = jax.nn.gelu(y).astype(o_ref.dtype)


def _s5(z, bmat, cmat, lam_re, lam_im, dvec, bsz, seq):
    n_t = seq // PB
    n_tiles = D_SSM // S5_TILE
    ns = S5_TILE_STATES
    vec = lambda b, c, t: (0, c)
    return pl.pallas_call(
        _s5_kernel,
        grid=(bsz, n_tiles, n_t),
        in_specs=[pl.BlockSpec((PB, S5_TILE), lambda b, c, t: (b * n_t + t, COL_U + c)),
                  pl.BlockSpec((1, S5_TILE, 2 * ns), lambda b, c, t: (c, 0, 0)),
                  pl.BlockSpec((1, 2 * ns, S5_TILE), lambda b, c, t: (c, 0, 0)),
                  pl.BlockSpec((1, ns), vec),
                  pl.BlockSpec((1, ns), vec),
                  pl.BlockSpec((1, S5_TILE), vec)],
        out_specs=pl.BlockSpec((PB, S5_TILE), lambda b, c, t: (b * n_t + t, c)),
        out_shape=jax.ShapeDtypeStruct((bsz * seq, D_SSM), BF16),
        scratch_shapes=[pltpu.VMEM((PB, ns), F32),
                        pltpu.VMEM((PB, ns), F32),
                        pltpu.VMEM((PB, 2 * ns), BF16),
                        pltpu.VMEM((PB, ns), F32),
                        pltpu.VMEM((PB, ns), F32),
                        pltpu.VMEM((SUBLANES, ns), F32),
                        pltpu.VMEM((SUBLANES, ns), F32),
                        pltpu.VMEM((3, SUBLANES, ns), F32),
                        pltpu.VMEM((3, SUBLANES, ns), F32),
                        pltpu.VMEM((1, ns), F32),
                        pltpu.VMEM((1, ns), F32)],
        compiler_params=_params("parallel", "parallel", "arbitrary"),
        name="s5",
    )(z, bmat, cmat, lam_re, lam_im, dvec)


def _mix_kernel(hg_ref, y_ref, ga_ref, gb_ref, wa_ref, gw_ref, gv_ref, o_ref):
    y_a = jnp.dot(hg_ref[...], wa_ref[...], preferred_element_type=F32)
    y = y_ref[...]
    y_b = (jnp.dot(y, gw_ref[...], preferred_element_type=F32)
           * jax.nn.sigmoid(jnp.dot(y, gv_ref[...], preferred_element_type=F32)))
    mix = (jax.nn.sigmoid(ga_ref[...].astype(F32)) * y_a
           + jax.nn.sigmoid(gb_ref[...].astype(F32)) * y_b)
    o_ref[...] = mix.astype(o_ref.dtype)


def _mix(hg, y, z, w_a, glu_w, glu_v, tm, tn):
    m = hg.shape[0]
    nb = tn // 256
    return pl.pallas_call(
        _mix_kernel,
        grid=(m // tm, D_MODEL // tn),
        in_specs=[pl.BlockSpec((tm, D_RNN), lambda i, j: (i, 0)),
                  pl.BlockSpec((tm, D_SSM), lambda i, j: (i, 0)),
                  pl.BlockSpec((tm, tn), lambda i, j: (i, COL_GA // nb + j)),
                  pl.BlockSpec((tm, tn), lambda i, j: (i, COL_GB // nb + j)),
                  pl.BlockSpec((D_RNN, tn), lambda i, j: (0, j)),
                  pl.BlockSpec((D_SSM, tn), lambda i, j: (0, j)),
                  pl.BlockSpec((D_SSM, tn), lambda i, j: (0, j))],
        out_specs=pl.BlockSpec((tm, tn), lambda i, j: (i, j)),
        out_shape=jax.ShapeDtypeStruct((m, D_MODEL), BF16),
        compiler_params=_params("parallel", "arbitrary"),
        name="mix",
    )(hg, y, z, z, w_a, glu_w, glu_v)


def _layernorm(v, g, b):
    mu = jnp.mean(v, axis=-1, keepdims=True)
    c = v - mu
    var = jnp.mean(c * c, axis=-1, keepdims=True)
    return c * jax.lax.rsqrt(var + LN_EPS) * g + b


def _outproj_ln_kernel(mix_ref, pt_ref, x_ref, w_ref, g_ref, b_ref, o_ref, ob_ref, *, alpha):
    mix = jnp.dot(pt_ref[...], mix_ref[...], preferred_element_type=F32).astype(BF16)
    v = alpha * x_ref[...] + jnp.dot(mix, w_ref[...], preferred_element_type=F32)
    out = _layernorm(v, g_ref[...], b_ref[...])
    o_ref[...] = out
    ob_ref[...] = out.astype(BF16)


def _outproj_ln(mix, perm_t, x, w_out, g, b, alpha):
    m = mix.shape[0]
    row = lambda i: (i, 0)
    fixed = lambda i: (0, 0)
    return pl.pallas_call(
        functools.partial(_outproj_ln_kernel, alpha=alpha),
        grid=(m // PB,),
        in_specs=[pl.BlockSpec((PB, D_MODEL), row),
                  pl.BlockSpec((PB, PB), fixed),
                  pl.BlockSpec((PB, D_MODEL), row),
                  pl.BlockSpec((D_MODEL, D_MODEL), fixed),
                  pl.BlockSpec((1, D_MODEL), fixed),
                  pl.BlockSpec((1, D_MODEL), fixed)],
        out_specs=(pl.BlockSpec((PB, D_MODEL), row), pl.BlockSpec((PB, D_MODEL), row)),
        out_shape=(jax.ShapeDtypeStruct((m, D_MODEL), F32),
                   jax.ShapeDtypeStruct((m, D_MODEL), BF16)),
        compiler_params=_params("parallel"),
        name="outproj_ln",
    )(mix, perm_t, x, w_out, g, b)


def _mlp_up_kernel(x_ref, w_ref, b_ref, o_ref):
    v = jnp.dot(x_ref[...], w_ref[...], preferred_element_type=F32) + b_ref[...]
    v = jnp.maximum(v, 0.0)
    o_ref[...] = (v * v).astype(o_ref.dtype)


def _mlp_up(xb, w_up, b_up, tm, tn):
    m = xb.shape[0]
    return pl.pallas_call(
        _mlp_up_kernel,
        grid=(m // tm, D_FF // tn),
        in_specs=[pl.BlockSpec((tm, D_MODEL), lambda i, j: (i, 0)),
                  pl.BlockSpec((D_MODEL, tn), lambda i, j: (0, j)),
                  pl.BlockSpec((1, tn), lambda i, j: (0, j))],
        out_specs=pl.BlockSpec((tm, tn), lambda i, j: (i, j)),
        out_shape=jax.ShapeDtypeStruct((m, D_FF), BF16),
        compiler_params=_params("parallel", "arbitrary"),
        name="mlp_up",
    )(xb, w_up, b_up)


def _mlp_down_ln_kernel(a_ref, w_ref, x_ref, bd_ref, g_ref, b_ref, o_ref, acc_ref, *, alpha):
    k = pl.program_id(1)

    @pl.when(k == 0)
    def _():
        acc_ref[...] = jnp.zeros_like(acc_ref)

    acc_ref[...] += jnp.dot(a_ref[...], w_ref[...], preferred_element_type=F32)

    @pl.when(k == pl.num_programs(1) - 1)
    def _():
        v = alpha * x_ref[...] + acc_ref[...] + bd_ref[...]
        o_ref[...] = _layernorm(v, g_ref[...], b_ref[...])


def _mlp_down_ln(a, w_down, x1, b_down, g, b, alpha, tm, tk):
    m = a.shape[0]
    fixed = lambda i, k: (0, 0)
    return pl.pallas_call(
        functools.partial(_mlp_down_ln_kernel, alpha=alpha),
        grid=(m // tm, D_FF // tk),
        in_specs=[pl.BlockSpec((tm, tk), lambda i, k: (i, k)),
                  pl.BlockSpec((tk, D_MODEL), lambda i, k: (k, 0)),
                  pl.BlockSpec((tm, D_MODEL), lambda i, k: (i, 0)),
                  pl.BlockSpec((1, D_MODEL), fixed),
                  pl.BlockSpec((1, D_MODEL), fixed),
                  pl.BlockSpec((1, D_MODEL), fixed)],
        out_specs=pl.BlockSpec((tm, D_MODEL), lambda i, k: (i, 0)),
        out_shape=jax.ShapeDtypeStruct((m, D_MODEL), F32),
        scratch_shapes=[pltpu.VMEM((tm, D_MODEL), F32)],
        compiler_params=_params("parallel", "arbitrary"),
        name="mlp_down_ln",
    )(a, w_down, x1, b_down, g, b)


def _s5_block_matrices(fr, fi, b_re, b_im, c_re, c_im):
    bbr = fr[..., None] * b_re - fi[..., None] * b_im
    bbi = fr[..., None] * b_im + fi[..., None] * b_re
    n_tiles = D_SSM // S5_TILE
    eye = jnp.eye(S5_TILE_GROUPS, dtype=F32)

    def in_blocks(w):
        w = w.reshape(n_tiles, S5_TILE_GROUPS, SSM_STATE, SSM_GROUP)
        m = jnp.einsum("tgph,gk->tghkp", w, eye)
        return m.reshape(n_tiles, S5_TILE, S5_TILE_STATES)

    def out_blocks(w):
        w = w.reshape(n_tiles, S5_TILE_GROUPS, SSM_GROUP, SSM_STATE)
        m = jnp.einsum("tghp,gk->tgpkh", w, eye)
        return m.reshape(n_tiles, S5_TILE_STATES, S5_TILE)

    bmat = jnp.concatenate([in_blocks(bbr), in_blocks(bbi)], axis=2).astype(BF16)
    cmat = jnp.concatenate([out_blocks(c_re), out_blocks(-c_im)], axis=1).astype(BF16)
    return bmat, cmat


def kernel(x, w_in, conv_w, conv_b, rg_wa, rg_ba, rg_wx, rg_bx, rg_lambda, w_a_out, ssm_a_re, ssm_a_im, ssm_log_dt, ssm_b_re, ssm_b_im, ssm_c_re, ssm_c_im, ssm_d, glu_w, glu_v, w_out, ln1_g, ln1_b, mlp_w_up, mlp_b_up, mlp_w_down, mlp_b_down, ln2_g, ln2_b):
    bsz, seq, _ = x.shape
    assert seq % PB == 0
    m = bsz * seq
    depth = w_in.shape[0]
    alpha = (2.0 * depth) ** 0.25
    perm = _interleave_matrix()
    perm_fwd = jnp.asarray(perm, BF16)
    perm_bwd = jnp.asarray(perm.T, BF16)
    for l in range(depth):
        xf = x.reshape(m, D_MODEL)
        z = _in_proj(xf, perm_fwd, w_in[l].astype(BF16), tm=1024, tn=1024)

        w_gate = jnp.concatenate([rg_wa[l], rg_wx[l]], axis=-1).astype(BF16)
        hg = _rglru(z, conv_w[l], conv_b[l].reshape(1, D_RNN), w_gate,
                    rg_ba[l].reshape(1, D_RNN), rg_bx[l].reshape(1, D_RNN),
                    rg_lambda[l].reshape(1, D_RNN), bsz, seq)

        lbr, lbi, fr, fi = _s5_discretize(ssm_a_re[l], ssm_a_im[l], ssm_log_dt[l])
        bmat, cmat = _s5_block_matrices(fr, fi, ssm_b_re[l], ssm_b_im[l],
                                        ssm_c_re[l], ssm_c_im[l])
        y = _s5(z, bmat, cmat, lbr.reshape(1, -1), lbi.reshape(1, -1),
                ssm_d[l].reshape(1, D_SSM), bsz, seq)

        mix = _mix(hg, y, z, w_a_out[l].astype(BF16), glu_w[l].astype(BF16),
                   glu_v[l].astype(BF16), tm=1024, tn=512)
        x1, x1b = _outproj_ln(mix, perm_bwd, xf, w_out[l].astype(BF16),
                              ln1_g[l].reshape(1, D_MODEL), ln1_b[l].reshape(1, D_MODEL),
                              alpha)
        a = _mlp_up(x1b, mlp_w_up[l].astype(BF16), mlp_b_up[l].reshape(1, D_FF),
                    tm=1024, tn=1024)
        x2 = _mlp_down_ln(a, mlp_w_down[l].astype(BF16), x1,
                          mlp_b_down[l].reshape(1, D_MODEL),
                          ln2_g[l].reshape(1, D_MODEL), ln2_b[l].reshape(1, D_MODEL),
                          alpha, tm=512, tk=1024)
        x = x2.reshape(bsz, seq, D_MODEL)
    return x
```

```python
import functools

import numpy as np
import jax
import jax.numpy as jnp
from jax.experimental import pallas as pl
from jax.experimental.pallas import tpu as pltpu

F32 = jnp.float32
BF16 = jnp.bfloat16

D_MODEL = 2048
D_RNN = D_MODEL
RG_HEADS = 16
RG_HEAD_DIM = D_RNN // RG_HEADS
CONV_WIDTH = 4
RG_C = 8.0
D_SSM = D_MODEL // 2
SSM_GROUP = 16
SSM_GROUPS = D_SSM // SSM_GROUP
SSM_STATE = 64
D_FF = 4 * D_MODEL
D_IN = 2 * D_RNN + D_SSM + 2 * D_MODEL
LN_EPS = 1e-5

SUBLANES = 8
BF16_ROWS = 2 * SUBLANES
VMEM_LIMIT = 56 * 1024 * 1024

NSEG = SUBLANES
PB = 512
SEG = PB // NSEG
HALO = (CONV_WIDTH - 1) * SUBLANES

RG_TILE = 512
S5_TILE = 256
S5_TILE_GROUPS = S5_TILE // SSM_GROUP
S5_TILE_STATES = S5_TILE_GROUPS * SSM_STATE
S5_FINISH_LANES = 512
COL_GATE = D_RNN // RG_TILE
COL_U = 2 * D_RNN // S5_TILE
COL_GA = (2 * D_RNN + D_SSM) // 256
COL_GB = COL_GA + D_MODEL // 256


def _params(*sem):
    return pltpu.CompilerParams(dimension_semantics=sem, vmem_limit_bytes=VMEM_LIMIT)


def _interleave_matrix():
    p = np.arange(PB)
    src = (p % NSEG) * SEG + p // NSEG
    mat = np.zeros((PB, PB), np.float32)
    mat[p, src] = 1.0
    return mat


def _row_ids(shape):
    return jax.lax.broadcasted_iota(jnp.int32, shape, 0)


def _shift_rows(v, d, fill):
    return jnp.where(_row_ids(v.shape) >= d, pltpu.roll(v, d, 0), fill)


def _in_proj_kernel(x_ref, p_ref, w_ref, o_ref, xp_ref, *, tm):
    @pl.when(pl.program_id(1) == 0)
    def _():
        for s in range(tm // PB):
            rows = slice(s * PB, (s + 1) * PB)
            xs = x_ref[rows, :].astype(BF16)
            xp_ref[rows, :] = jnp.dot(p_ref[...], xs,
                                      preferred_element_type=F32).astype(BF16)

    o_ref[...] = jnp.dot(xp_ref[...], w_ref[...],
                         preferred_element_type=F32).astype(o_ref.dtype)


def _in_proj(x, perm, w, tm, tn):
    m, k = x.shape
    n = w.shape[1]
    return pl.pallas_call(
        functools.partial(_in_proj_kernel, tm=tm),
        grid=(m // tm, n // tn),
        in_specs=[pl.BlockSpec((tm, k), lambda i, j: (i, 0)),
                  pl.BlockSpec((PB, PB), lambda i, j: (0, 0)),
                  pl.BlockSpec((k, tn), lambda i, j: (0, j))],
        out_specs=pl.BlockSpec((tm, tn), lambda i, j: (i, j)),
        out_shape=jax.ShapeDtypeStruct((m, n), BF16),
        scratch_shapes=[pltpu.VMEM((tm, k), BF16)],
        compiler_params=_params("parallel", "arbitrary"),
        name="in_proj",
    )(x, perm, w)


def _rglru_kernel(x_ref, gate_ref, cw_ref, cb_ref, w_ref, ba_ref, bx_ref, lam_ref,
                  o_ref, ext_ref, a_ref, b_ref, carry_ref):
    t = pl.program_id(2)
    grp = (SUBLANES, RG_TILE)

    @pl.when(t == 0)
    def _():
        ext_ref[PB:PB + HALO, :] = jnp.zeros((HALO, RG_TILE), F32)
        carry_ref[...] = jnp.zeros_like(carry_ref)

    x = x_ref[...].astype(F32)
    first = _row_ids(grp) == 0
    halos = []
    for k in range(CONV_WIDTH - 1):
        prev_g = ext_ref[PB + k * SUBLANES:PB + (k + 1) * SUBLANES, :]
        cur_g = x[PB - HALO + k * SUBLANES:PB - HALO + (k + 1) * SUBLANES, :]
        halos.append(jnp.where(first, pltpu.roll(prev_g, 1, 0), pltpu.roll(cur_g, 1, 0)))
    for k in range(CONV_WIDTH - 1):
        ext_ref[k * SUBLANES:(k + 1) * SUBLANES, :] = halos[k]
    ext_ref[HALO:HALO + PB, :] = x

    xc = cb_ref[...] + cw_ref[CONV_WIDTH - 1:CONV_WIDTH, :] * x
    for k in range(CONV_WIDTH - 1):
        xc = xc + cw_ref[k:k + 1, :] * ext_ref[k * SUBLANES:k * SUBLANES + PB, :]

    neg = -lam_ref[...]
    softplus = jnp.maximum(neg, 0.0) + jnp.log1p(jnp.exp(-jnp.abs(neg)))
    cvec = -RG_C * softplus

    xcb = xc.astype(BF16)
    for hh in range(RG_TILE // RG_HEAD_DIM):
        sl = slice(hh * RG_HEAD_DIM, (hh + 1) * RG_HEAD_DIM)
        pre = jnp.dot(xcb[:, sl], w_ref[hh], preferred_element_type=F32)
        r = jax.nn.sigmoid(pre[:, :RG_HEAD_DIM] + ba_ref[:, sl])
        i = jax.nn.sigmoid(pre[:, RG_HEAD_DIM:] + bx_ref[:, sl])
        log_a = cvec[:, sl] * r
        a = jnp.exp(log_a)
        a_ref[:, sl] = a
        one_minus_a2 = -jnp.tanh(log_a) * (1.0 + a * a)
        b_ref[:, sl] = jnp.sqrt(one_minus_a2) * (i * xc[:, sl])

    def step(tau, carry):
        h, acc = carry
        rows = pl.ds(pl.multiple_of(tau * SUBLANES, SUBLANES), SUBLANES)
        a = a_ref[rows, :]
        h = a * h + b_ref[rows, :]
        acc = a * acc
        b_ref[rows, :] = h
        a_ref[rows, :] = acc
        return h, acc

    h_end, a_end = jax.lax.fori_loop(
        1, SEG, step, (b_ref[0:SUBLANES, :], a_ref[0:SUBLANES, :]), unroll=8)

    for d in (1, 2, 4):
        h_end = a_end * _shift_rows(h_end, d, 0.0) + h_end
        a_end = a_end * _shift_rows(a_end, d, 1.0)
    carry = carry_ref[...]
    h_true = h_end + a_end * carry
    h_in = jnp.where(first, carry, pltpu.roll(h_true, 1, 0))
    carry_ref[...] = h_true[SUBLANES - 1:SUBLANES, :]

    h_in2 = jnp.concatenate([h_in, h_in], axis=0)

    def finish(i, _):
        rows = pl.ds(pl.multiple_of(i * BF16_ROWS, BF16_ROWS), BF16_ROWS)
        h = b_ref[rows, :] + a_ref[rows, :] * h_in2
        o_ref[rows, :] = (h * jax.nn.gelu(gate_ref[rows, :].astype(F32))).astype(o_ref.dtype)
        return 0

    jax.lax.fori_loop(0, PB // BF16_ROWS, finish, 0, unroll=4)


def _rglru(z, conv_w, conv_b, w_gate, ba, bx, lam, bsz, seq):
    n_t = seq // PB
    row = lambda b, c, t: (b * n_t + t, c)
    vec = lambda b, c, t: (0, c)
    return pl.pallas_call(
        _rglru_kernel,
        grid=(bsz, D_RNN // RG_TILE, n_t),
        in_specs=[pl.BlockSpec((PB, RG_TILE), row),
                  pl.BlockSpec((PB, RG_TILE), lambda b, c, t: (b * n_t + t, COL_GATE + c)),
                  pl.BlockSpec((CONV_WIDTH, RG_TILE), vec),
                  pl.BlockSpec((1, RG_TILE), vec),
                  pl.BlockSpec((RG_TILE // RG_HEAD_DIM, RG_HEAD_DIM, 2 * RG_HEAD_DIM),
                               lambda b, c, t: (c, 0, 0)),
                  pl.BlockSpec((1, RG_TILE), vec),
                  pl.BlockSpec((1, RG_TILE), vec),
                  pl.BlockSpec((1, RG_TILE), vec)],
        out_specs=pl.BlockSpec((PB, RG_TILE), row),
        out_shape=jax.ShapeDtypeStruct((bsz * seq, D_RNN), BF16),
        scratch_shapes=[pltpu.VMEM((HALO + PB, RG_TILE), F32),
                        pltpu.VMEM((PB, RG_TILE), F32),
                        pltpu.VMEM((PB, RG_TILE), F32),
                        pltpu.VMEM((1, RG_TILE), F32)],
        compiler_params=_params("parallel", "parallel", "arbitrary"),
        name="rglru",
    )(z, z, conv_w, conv_b, w_gate, ba, bx, lam)


def _s5_discretize_kernel(are_ref, aim_ref, ldt_ref, lbr_ref, lbi_ref, fr_ref, fi_ref):
    dt = jnp.exp(ldt_ref[...])
    lr = jnp.minimum(are_ref[...], -1e-4)
    li = aim_ref[...]
    mag = jnp.exp(lr * dt)
    lbr = mag * jnp.cos(li * dt)
    lbi = mag * jnp.sin(li * dt)
    zr, zi = lbr - 1.0, lbi
    den = lr * lr + li * li
    lbr_ref[...] = lbr
    lbi_ref[...] = lbi
    fr_ref[...] = (zr * lr + zi * li) / den
    fi_ref[...] = (zi * lr - zr * li) / den


def _s5_discretize(a_re, a_im, log_dt):
    shp = jax.ShapeDtypeStruct((SSM_GROUPS, SSM_STATE), F32)
    return pl.pallas_call(
        _s5_discretize_kernel,
        out_shape=(shp, shp, shp, shp),
        name="s5_discretize",
    )(a_re, a_im, log_dt.reshape(SSM_GROUPS, 1))


def _cmul(ar, ai, br, bi):
    return ar * br - ai * bi, ar * bi + ai * br


def _s5_kernel(u_ref, bmat_ref, cmat_ref, lre_ref, lim_ref, d_ref, o_ref,
               hr_ref, hi_ref, hb_ref, tab_re_ref, tab_im_ref,
               pow_re_ref, pow_im_ref, step_re_ref, step_im_ref, cr_ref, ci_ref):
    t = pl.program_id(2)
    ns = S5_TILE_STATES
    grp = (SUBLANES, ns)

    @pl.when(t == 0)
    def _():
        cr_ref[...] = jnp.zeros_like(cr_ref)
        ci_ref[...] = jnp.zeros_like(ci_ref)
        lr = jnp.broadcast_to(lre_ref[...], grp)
        li = jnp.broadcast_to(lim_ref[...], grp)

        def fill(tau, p):
            pr, pi = p
            rows = pl.ds(pl.multiple_of(tau * SUBLANES, SUBLANES), SUBLANES)
            tab_re_ref[rows, :] = pr
            tab_im_ref[rows, :] = pi
            return _cmul(pr, pi, lr, li)

        jax.lax.fori_loop(0, SEG, fill, (lr, li))
        mr = tab_re_ref[PB - SUBLANES:PB, :]
        mi = tab_im_ref[PB - SUBLANES:PB, :]
        row = _row_ids(grp)
        pr, pi = mr, mi
        for r in range(SUBLANES):
            pow_re_ref[r:r + 1, :] = pr[0:1, :]
            pow_im_ref[r:r + 1, :] = pi[0:1, :]
            if r + 1 in (1, 2, 4):
                k = (1, 2, 4).index(r + 1)
                step_re_ref[k] = jnp.where(row >= r + 1, pr, 0.0)
                step_im_ref[k] = jnp.where(row >= r + 1, pi, 0.0)
            pr, pi = _cmul(pr, pi, mr, mi)

    u = u_ref[...]
    bu = jnp.dot(u, bmat_ref[0], preferred_element_type=F32)
    hr_ref[...] = bu[:, :ns]
    hi_ref[...] = bu[:, ns:]

    lr = jnp.broadcast_to(lre_ref[...], grp)
    li = jnp.broadcast_to(lim_ref[...], grp)

    def step(tau, carry):
        hr, hi = carry
        rows = pl.ds(pl.multiple_of(tau * SUBLANES, SUBLANES), SUBLANES)
        nr = (lr * hr - li * hi) + hr_ref[rows, :]
        ni = (lr * hi + li * hr) + hi_ref[rows, :]
        hr_ref[rows, :] = nr
        hi_ref[rows, :] = ni
        return nr, ni

    er, ei = jax.lax.fori_loop(
        1, SEG, step, (hr_ref[0:SUBLANES, :], hi_ref[0:SUBLANES, :]), unroll=4)

    for k, d in enumerate((1, 2, 4)):
        sr, si = pltpu.roll(er, d, 0), pltpu.roll(ei, d, 0)
        mr, mi = _cmul(step_re_ref[k], step_im_ref[k], sr, si)
        er, ei = er + mr, ei + mi
    cr, ci = cr_ref[...], ci_ref[...]
    mr, mi = _cmul(pow_re_ref[...], pow_im_ref[...], cr, ci)
    er, ei = er + mr, ei + mi
    first = _row_ids(grp) == 0
    in_r = jnp.where(first, cr, pltpu.roll(er, 1, 0))
    in_i = jnp.where(first, ci, pltpu.roll(ei, 1, 0))
    cr_ref[...] = er[SUBLANES - 1:SUBLANES, :]
    ci_ref[...] = ei[SUBLANES - 1:SUBLANES, :]

    for c0 in range(0, ns, S5_FINISH_LANES):
        cols = slice(c0, c0 + S5_FINISH_LANES)
        in_r2 = jnp.concatenate([in_r[:, cols], in_r[:, cols]], axis=0)
        in_i2 = jnp.concatenate([in_i[:, cols], in_i[:, cols]], axis=0)

        def finish(i, _, c0=c0, cols=cols, in_r2=in_r2, in_i2=in_i2):
            rows = pl.ds(pl.multiple_of(i * BF16_ROWS, BF16_ROWS), BF16_ROWS)
            fr, fi = _cmul(tab_re_ref[rows, cols], tab_im_ref[rows, cols], in_r2, in_i2)
            hb_ref[rows, c0:c0 + S5_FINISH_LANES] = (hr_ref[rows, cols] + fr).astype(BF16)
            hb_ref[rows, ns + c0:ns + c0 + S5_FINISH_LANES] = (hi_ref[rows, cols] + fi).astype(BF16)
            return 0

        jax.lax.fori_loop(0, PB // BF16_ROWS, finish, 0, unroll=2)

    y = (jnp.dot(hb_ref[...], cmat_ref[0], preferred_element_type=F32)
         + d_ref[...] * u.astype(F32))
    o_ref[...] = jax.nn.gelu(y).astype(o_ref.dtype)


def _s5(z, bmat, cmat, lam_re, lam_im, dvec, bsz, seq):
    n_t = seq // PB
    n_tiles = D_SSM // S5_TILE
    ns = S5_TILE_STATES
    vec = lambda b, c, t: (0, c)
    return pl.pallas_call(
        _s5_kernel,
        grid=(bsz, n_tiles, n_t),
        in_specs=[pl.BlockSpec((PB, S5_TILE), lambda b, c, t: (b * n_t + t, COL_U + c)),
                  pl.BlockSpec((1, S5_TILE, 2 * ns), lambda b, c, t: (c, 0, 0)),
                  pl.BlockSpec((1, 2 * ns, S5_TILE), lambda b, c, t: (c, 0, 0)),
                  pl.BlockSpec((1, ns), vec),
                  pl.BlockSpec((1, ns), vec),
                  pl.BlockSpec((1, S5_TILE), vec)],
        out_specs=pl.BlockSpec((PB, S5_TILE), lambda b, c, t: (b * n_t + t, c)),
        out_shape=jax.ShapeDtypeStruct((bsz * seq, D_SSM), BF16),
        scratch_shapes=[pltpu.VMEM((PB, ns), F32),
                        pltpu.VMEM((PB, ns), F32),
                        pltpu.VMEM((PB, 2 * ns), BF16),
                        pltpu.VMEM((PB, ns), F32),
                        pltpu.VMEM((PB, ns), F32),
                        pltpu.VMEM((SUBLANES, ns), F32),
                        pltpu.VMEM((SUBLANES, ns), F32),
                        pltpu.VMEM((3, SUBLANES, ns), F32),
                        pltpu.VMEM((3, SUBLANES, ns), F32),
                        pltpu.VMEM((1, ns), F32),
                        pltpu.VMEM((1, ns), F32)],
        compiler_params=_params("parallel", "parallel", "arbitrary"),
        name="s5",
    )(z, bmat, cmat, lam_re, lam_im, dvec)


def _mix_kernel(hg_ref, y_ref, ga_ref, gb_ref, wa_ref, gw_ref, gv_ref, o_ref,
                wab_ref, gwb_ref, gvb_ref):
    @pl.when(pl.program_id(1) == 0)
    def _():
        wab_ref[...] = wa_ref[...].astype(BF16)
        gwb_ref[...] = gw_ref[...].astype(BF16)
        gvb_ref[...] = gv_ref[...].astype(BF16)

    y_a = jnp.dot(hg_ref[...], wab_ref[...], preferred_element_type=F32)
    y = y_ref[...]
    y_b = (jnp.dot(y, gwb_ref[...], preferred_element_type=F32)
           * jax.nn.sigmoid(jnp.dot(y, gvb_ref[...], preferred_element_type=F32)))
    mix = (jax.nn.sigmoid(ga_ref[...].astype(F32)) * y_a
           + jax.nn.sigmoid(gb_ref[...].astype(F32)) * y_b)
    o_ref[...] = mix.astype(o_ref.dtype)


def _mix(hg, y, z, w_a, glu_w, glu_v, tm, tn):
    m = hg.shape[0]
    nb = tn // 256
    return pl.pallas_call(
        _mix_kernel,
        grid=(D_MODEL // tn, m // tm),
        in_specs=[pl.BlockSpec((tm, D_RNN), lambda j, i: (i, 0)),
                  pl.BlockSpec((tm, D_SSM), lambda j, i: (i, 0)),
                  pl.BlockSpec((tm, tn), lambda j, i: (i, COL_GA // nb + j)),
                  pl.BlockSpec((tm, tn), lambda j, i: (i, COL_GB // nb + j)),
                  pl.BlockSpec((D_RNN, tn), lambda j, i: (0, j)),
                  pl.BlockSpec((D_SSM, tn), lambda j, i: (0, j)),
                  pl.BlockSpec((D_SSM, tn), lambda j, i: (0, j))],
        out_specs=pl.BlockSpec((tm, tn), lambda j, i: (i, j)),
        out_shape=jax.ShapeDtypeStruct((m, D_MODEL), BF16),
        scratch_shapes=[pltpu.VMEM((D_RNN, tn), BF16),
                        pltpu.VMEM((D_SSM, tn), BF16),
                        pltpu.VMEM((D_SSM, tn), BF16)],
        compiler_params=_params("parallel", "arbitrary"),
        name="mix",
    )(hg, y, z, z, w_a, glu_w, glu_v)


def _layernorm(v, g, b):
    mu = jnp.mean(v, axis=-1, keepdims=True)
    c = v - mu
    var = jnp.mean(c * c, axis=-1, keepdims=True)
    return c * jax.lax.rsqrt(var + LN_EPS) * g + b


def _outproj_ln_kernel(mix_ref, pt_ref, x_ref, w_ref, g_ref, b_ref, o_ref, ob_ref, *, alpha):
    mix = jnp.dot(pt_ref[...], mix_ref[...], preferred_element_type=F32).astype(BF16)
    v = alpha * x_ref[...] + jnp.dot(mix, w_ref[...], preferred_element_type=F32)
    out = _layernorm(v, g_ref[...], b_ref[...])
    o_ref[...] = out
    ob_ref[...] = out.astype(BF16)


def _outproj_ln(mix, perm_t, x, w_out, g, b, alpha):
    m = mix.shape[0]
    row = lambda i: (i, 0)
    fixed = lambda i: (0, 0)
    return pl.pallas_call(
        functools.partial(_outproj_ln_kernel, alpha=alpha),
        grid=(m // PB,),
        in_specs=[pl.BlockSpec((PB, D_MODEL), row),
                  pl.BlockSpec((PB, PB), fixed),
                  pl.BlockSpec((PB, D_MODEL), row),
                  pl.BlockSpec((D_MODEL, D_MODEL), fixed),
                  pl.BlockSpec((1, D_MODEL), fixed),
                  pl.BlockSpec((1, D_MODEL), fixed)],
        out_specs=(pl.BlockSpec((PB, D_MODEL), row), pl.BlockSpec((PB, D_MODEL), row)),
        out_shape=(jax.ShapeDtypeStruct((m, D_MODEL), F32),
                   jax.ShapeDtypeStruct((m, D_MODEL), BF16)),
        compiler_params=_params("parallel"),
        name="outproj_ln",
    )(mix, perm_t, x, w_out, g, b)


def _mlp_up_kernel(x_ref, w_ref, b_ref, o_ref):
    v = jnp.dot(x_ref[...], w_ref[...], preferred_element_type=F32) + b_ref[...]
    v = jnp.maximum(v, 0.0)
    o_ref[...] = (v * v).astype(o_ref.dtype)


def _mlp_up(xb, w_up, b_up, tm, tn):
    m = xb.shape[0]
    return pl.pallas_call(
        _mlp_up_kernel,
        grid=(m // tm, D_FF // tn),
        in_specs=[pl.BlockSpec((tm, D_MODEL), lambda i, j: (i, 0)),
                  pl.BlockSpec((D_MODEL, tn), lambda i, j: (0, j)),
                  pl.BlockSpec((1, tn), lambda i, j: (0, j))],
        out_specs=pl.BlockSpec((tm, tn), lambda i, j: (i, j)),
        out_shape=jax.ShapeDtypeStruct((m, D_FF), BF16),
        compiler_params=_params("parallel", "arbitrary"),
        name="mlp_up",
    )(xb, w_up, b_up)


def _mlp_down_ln_kernel(a_ref, w_ref, x_ref, bd_ref, g_ref, b_ref, o_ref, *, alpha):
    k = pl.program_id(1)
    last = pl.num_programs(1) - 1

    def part():
        return jnp.dot(a_ref[...], w_ref[...], preferred_element_type=F32)

    @pl.when(k == 0)
    def _():
        o_ref[...] = part()

    @pl.when(jnp.logical_and(k > 0, k < last))
    def _():
        o_ref[...] += part()

    @pl.when(k == last)
    def _():
        v = alpha * x_ref[...] + (o_ref[...] + part()) + bd_ref[...]
        o_ref[...] = _layernorm(v, g_ref[...], b_ref[...])


def _mlp_down_ln(a, w_down, x1, b_down, g, b, alpha, tm, tk):
    m = a.shape[0]
    fixed = lambda i, k: (0, 0)
    return pl.pallas_call(
        functools.partial(_mlp_down_ln_kernel, alpha=alpha),
        grid=(m // tm, D_FF // tk),
        in_specs=[pl.BlockSpec((tm, tk), lambda i, k: (i, k)),
                  pl.BlockSpec((tk, D_MODEL), lambda i, k: (k, 0)),
                  pl.BlockSpec((tm, D_MODEL), lambda i, k: (i, 0)),
                  pl.BlockSpec((1, D_MODEL), fixed),
                  pl.BlockSpec((1, D_MODEL), fixed),
                  pl.BlockSpec((1, D_MODEL), fixed)],
        out_specs=pl.BlockSpec((tm, D_MODEL), lambda i, k: (i, 0)),
        out_shape=jax.ShapeDtypeStruct((m, D_MODEL), F32),
        compiler_params=_params("parallel", "arbitrary"),
        name="mlp_down_ln",
    )(a, w_down, x1, b_down, g, b)


def _s5_block_matrices(fr, fi, b_re, b_im, c_re, c_im):
    bbr = fr[..., None] * b_re - fi[..., None] * b_im
    bbi = fr[..., None] * b_im + fi[..., None] * b_re
    n_tiles = D_SSM // S5_TILE
    eye = jnp.eye(S5_TILE_GROUPS, dtype=F32)

    def in_blocks(w):
        w = w.reshape(n_tiles, S5_TILE_GROUPS, SSM_STATE, SSM_GROUP)
        m = jnp.einsum("tgph,gk->tghkp", w, eye)
        return m.reshape(n_tiles, S5_TILE, S5_TILE_STATES)

    def out_blocks(w):
        w = w.reshape(n_tiles, S5_TILE_GROUPS, SSM_GROUP, SSM_STATE)
        m = jnp.einsum("tghp,gk->tgpkh", w, eye)
        return m.reshape(n_tiles, S5_TILE_STATES, S5_TILE)

    bmat = jnp.concatenate([in_blocks(bbr), in_blocks(bbi)], axis=2).astype(BF16)
    cmat = jnp.concatenate([out_blocks(c_re), out_blocks(-c_im)], axis=1).astype(BF16)
    return bmat, cmat


def kernel(x, w_in, conv_w, conv_b, rg_wa, rg_ba, rg_wx, rg_bx, rg_lambda, w_a_out, ssm_a_re, ssm_a_im, ssm_log_dt, ssm_b_re, ssm_b_im, ssm_c_re, ssm_c_im, ssm_d, glu_w, glu_v, w_out, ln1_g, ln1_b, mlp_w_up, mlp_b_up, mlp_w_down, mlp_b_down, ln2_g, ln2_b):
    bsz, seq, _ = x.shape
    assert seq % PB == 0
    m = bsz * seq
    depth = w_in.shape[0]
    alpha = (2.0 * depth) ** 0.25
    perm = _interleave_matrix()
    perm_fwd = jnp.asarray(perm, BF16)
    perm_bwd = jnp.asarray(perm.T, BF16)
    for l in range(depth):
        xf = x.reshape(m, D_MODEL)
        z = _in_proj(xf, perm_fwd, w_in[l].astype(BF16), tm=1024, tn=2304)

        w_gate = jnp.concatenate([rg_wa[l], rg_wx[l]], axis=-1).astype(BF16)
        hg = _rglru(z, conv_w[l], conv_b[l].reshape(1, D_RNN), w_gate,
                    rg_ba[l].reshape(1, D_RNN), rg_bx[l].reshape(1, D_RNN),
                    rg_lambda[l].reshape(1, D_RNN), bsz, seq)

        lbr, lbi, fr, fi = _s5_discretize(ssm_a_re[l], ssm_a_im[l], ssm_log_dt[l])
        bmat, cmat = _s5_block_matrices(fr, fi, ssm_b_re[l], ssm_b_im[l],
                                        ssm_c_re[l], ssm_c_im[l])
        y = _s5(z, bmat, cmat, lbr.reshape(1, -1), lbi.reshape(1, -1),
                ssm_d[l].reshape(1, D_SSM), bsz, seq)

        mix = _mix(hg, y, z, w_a_out[l], glu_w[l], glu_v[l], tm=1024, tn=512)
        x1, x1b = _outproj_ln(mix, perm_bwd, xf, w_out[l].astype(BF16),
                              ln1_g[l].reshape(1, D_MODEL), ln1_b[l].reshape(1, D_MODEL),
                              alpha)
        a = _mlp_up(x1b, mlp_w_up[l].astype(BF16), mlp_b_up[l].reshape(1, D_FF),
                    tm=1024, tn=2048)
        x2 = _mlp_down_ln(a, mlp_w_down[l].astype(BF16), x1,
                          mlp_b_down[l].reshape(1, D_MODEL),
                          ln2_g[l].reshape(1, D_MODEL), ln2_b[l].reshape(1, D_MODEL),
                          alpha, tm=1024, tk=1024)
        x = x2.reshape(bsz, seq, D_MODEL)
    return x
```

```python
import functools

import numpy as np
import jax
import jax.numpy as jnp
from jax.experimental import pallas as pl
from jax.experimental.pallas import tpu as pltpu

F32 = jnp.float32
BF16 = jnp.bfloat16

D_MODEL = 2048
D_RNN = D_MODEL
RG_HEADS = 16
RG_HEAD_DIM = D_RNN // RG_HEADS
CONV_WIDTH = 4
RG_C = 8.0
D_SSM = D_MODEL // 2
SSM_GROUP = 16
SSM_GROUPS = D_SSM // SSM_GROUP
SSM_STATE = 64
D_FF = 4 * D_MODEL
D_IN = 2 * D_RNN + D_SSM + 2 * D_MODEL
LN_EPS = 1e-5

SUBLANES = 8
BF16_ROWS = 2 * SUBLANES
VMEM_LIMIT = 56 * 1024 * 1024

NSEG = SUBLANES
PB = 512
SEG = PB // NSEG
HALO = (CONV_WIDTH - 1) * SUBLANES

RG_TILE = 512
S5_TILE = 256
S5_TILE_GROUPS = S5_TILE // SSM_GROUP
S5_TILE_STATES = S5_TILE_GROUPS * SSM_STATE
S5_FINISH_LANES = 512
W_COL_GATE = D_RNN
W_COL_REST = 2 * D_RNN
Z_COLS = D_SSM + 2 * D_MODEL
Z_COL_GA = D_SSM
Z_COL_GB = D_SSM + D_MODEL


def _params(*sem):
    return pltpu.CompilerParams(dimension_semantics=sem, vmem_limit_bytes=VMEM_LIMIT)


def _interleave_matrix():
    p = np.arange(PB)
    src = (p % NSEG) * SEG + p // NSEG
    mat = np.zeros((PB, PB), np.float32)
    mat[p, src] = 1.0
    return mat


def _row_ids(shape):
    return jax.lax.broadcasted_iota(jnp.int32, shape, 0)


def _shift_rows(v, d, fill):
    return jnp.where(_row_ids(v.shape) >= d, pltpu.roll(v, d, 0), fill)


def _interleave_kernel(x_ref, p_ref, o_ref):
    o_ref[...] = jnp.dot(p_ref[...], x_ref[...].astype(BF16),
                         preferred_element_type=F32).astype(BF16)


def _interleave(x, perm):
    m, k = x.shape
    return pl.pallas_call(
        _interleave_kernel,
        grid=(m // PB,),
        in_specs=[pl.BlockSpec((PB, k), lambda i: (i, 0)),
                  pl.BlockSpec((PB, PB), lambda i: (0, 0))],
        out_specs=pl.BlockSpec((PB, k), lambda i: (i, 0)),
        out_shape=jax.ShapeDtypeStruct((m, k), BF16),
        compiler_params=_params("parallel"),
        name="interleave",
    )(x, perm)


def _in_proj_kernel(x_ref, w_ref, o_ref, wb_ref):
    @pl.when(pl.program_id(1) == 0)
    def _():
        wb_ref[...] = w_ref[...].astype(BF16)

    o_ref[...] = jnp.dot(x_ref[...], wb_ref[...],
                         preferred_element_type=F32).astype(o_ref.dtype)


def _in_proj(xp, w, tm, tn):
    m, k = xp.shape
    col0 = W_COL_REST // tn
    return pl.pallas_call(
        _in_proj_kernel,
        grid=(Z_COLS // tn, m // tm),
        in_specs=[pl.BlockSpec((tm, k), lambda j, i: (i, 0)),
                  pl.BlockSpec((k, tn), lambda j, i: (0, col0 + j))],
        out_specs=pl.BlockSpec((tm, tn), lambda j, i: (i, j)),
        out_shape=jax.ShapeDtypeStruct((m, Z_COLS), BF16),
        scratch_shapes=[pltpu.VMEM((k, tn), BF16)],
        compiler_params=_params("parallel", "arbitrary"),
        name="in_proj",
    )(xp, w)


def _rglru_block(z_ref, o_ref, row0, cvec, cw_ref, cb_ref, w_ref, ba_ref, bx_ref,
                 ext_ref, a_ref, b_ref, carry_ref):
    grp = (SUBLANES, RG_TILE)
    x = z_ref[:, :RG_TILE]
    first = _row_ids(grp) == 0
    halos = []
    for k in range(CONV_WIDTH - 1):
        prev_g = ext_ref[PB + k * SUBLANES:PB + (k + 1) * SUBLANES, :]
        cur_g = x[PB - HALO + k * SUBLANES:PB - HALO + (k + 1) * SUBLANES, :]
        halos.append(jnp.where(first, pltpu.roll(prev_g, 1, 0), pltpu.roll(cur_g, 1, 0)))
    for k in range(CONV_WIDTH - 1):
        ext_ref[k * SUBLANES:(k + 1) * SUBLANES, :] = halos[k]
    ext_ref[HALO:HALO + PB, :] = x

    xc = cb_ref[...] + cw_ref[CONV_WIDTH - 1:CONV_WIDTH, :] * x
    for k in range(CONV_WIDTH - 1):
        xc = xc + cw_ref[k:k + 1, :] * ext_ref[k * SUBLANES:k * SUBLANES + PB, :]

    xcb = xc.astype(BF16)
    for hh in range(RG_TILE // RG_HEAD_DIM):
        sl = slice(hh * RG_HEAD_DIM, (hh + 1) * RG_HEAD_DIM)
        pre = jnp.dot(xcb[:, sl], w_ref[hh], preferred_element_type=F32)
        r = jax.nn.sigmoid(pre[:, :RG_HEAD_DIM] + ba_ref[:, sl])
        i = jax.nn.sigmoid(pre[:, RG_HEAD_DIM:] + bx_ref[:, sl])
        log_a = cvec[:, sl] * r
        a = jnp.exp(log_a)
        a_ref[:, sl] = a
        one_minus_a2 = -jnp.tanh(log_a) * (1.0 + a * a)
        b_ref[:, sl] = jnp.sqrt(one_minus_a2) * (i * xc[:, sl])

    h_end, a_end = b_ref[0:SUBLANES, :], a_ref[0:SUBLANES, :]
    for tau in range(1, SEG):
        rows = slice(tau * SUBLANES, (tau + 1) * SUBLANES)
        a = a_ref[rows, :]
        h_end = a * h_end + b_ref[rows, :]
        a_end = a * a_end
        b_ref[rows, :] = h_end
        a_ref[rows, :] = a_end

    for d in (1, 2, 4):
        h_end = a_end * _shift_rows(h_end, d, 0.0) + h_end
        a_end = a_end * _shift_rows(a_end, d, 1.0)
    carry = carry_ref[...]
    h_true = h_end + a_end * carry
    h_in = jnp.where(first, carry, pltpu.roll(h_true, 1, 0))
    carry_ref[...] = h_true[SUBLANES - 1:SUBLANES, :]

    h_in2 = jnp.concatenate([h_in, h_in], axis=0)
    for i in range(PB // BF16_ROWS):
        rows = slice(i * BF16_ROWS, (i + 1) * BF16_ROWS)
        h = b_ref[rows, :] + a_ref[rows, :] * h_in2
        gate = z_ref[rows, RG_TILE:]
        o_ref[row0 + i * BF16_ROWS:row0 + (i + 1) * BF16_ROWS, :] = (
            h * jax.nn.gelu(gate)).astype(o_ref.dtype)


def _rglru_kernel(x0_ref, x1_ref, x2_ref, wx_ref, wg_ref, cw_ref, cb_ref, w_ref,
                  ba_ref, bx_ref, lam_ref, o_ref,
                  wb_ref, zn_ref, zb_ref, ext_ref, a_ref, b_ref, carry_ref):
    t = pl.program_id(2)

    def project(x_ref):
        return jnp.dot(x_ref[...], wb_ref[...], preferred_element_type=F32)

    @pl.when(t == 0)
    def _():
        wb_ref[:, :RG_TILE] = wx_ref[...].astype(BF16)
        wb_ref[:, RG_TILE:] = wg_ref[...].astype(BF16)
        zn_ref[...] = project(x0_ref)
        ext_ref[PB:PB + HALO, :] = jnp.zeros((HALO, RG_TILE), F32)
        carry_ref[...] = jnp.zeros_like(carry_ref)

    neg = -lam_ref[...]
    softplus = jnp.maximum(neg, 0.0) + jnp.log1p(jnp.exp(-jnp.abs(neg)))
    cvec = -RG_C * softplus
    args = (cvec, cw_ref, cb_ref, w_ref, ba_ref, bx_ref, ext_ref, a_ref, b_ref, carry_ref)

    zb_ref[...] = project(x1_ref)
    _rglru_block(zn_ref, o_ref, 0, *args)
    zn_ref[...] = project(x2_ref)
    _rglru_block(zb_ref, o_ref, PB, *args)


def _rglru(xp, w_in, conv_w, conv_b, w_gate, ba, bx, lam, bsz, seq):
    n_t = seq // PB
    assert n_t % 2 == 0
    n_c = D_RNN // RG_TILE
    vec = lambda b, c, t: (0, c)
    return pl.pallas_call(
        _rglru_kernel,
        grid=(bsz, n_c, n_t // 2),
        in_specs=[pl.BlockSpec((PB, D_MODEL), lambda b, c, t: (b * n_t, 0)),
                  pl.BlockSpec((PB, D_MODEL), lambda b, c, t: (b * n_t + 2 * t + 1, 0)),
                  pl.BlockSpec((PB, D_MODEL),
                               lambda b, c, t: (b * n_t + jnp.minimum(2 * t + 2, n_t - 1), 0)),
                  pl.BlockSpec((D_MODEL, RG_TILE), vec),
                  pl.BlockSpec((D_MODEL, RG_TILE), lambda b, c, t: (0, W_COL_GATE // RG_TILE + c)),
                  pl.BlockSpec((CONV_WIDTH, RG_TILE), vec),
                  pl.BlockSpec((1, RG_TILE), vec),
                  pl.BlockSpec((RG_TILE // RG_HEAD_DIM, RG_HEAD_DIM, 2 * RG_HEAD_DIM),
                               lambda b, c, t: (c, 0, 0)),
                  pl.BlockSpec((1, RG_TILE), vec),
                  pl.BlockSpec((1, RG_TILE), vec),
                  pl.BlockSpec((1, RG_TILE), vec)],
        out_specs=pl.BlockSpec((2 * PB, RG_TILE), lambda b, c, t: (b * (n_t // 2) + t, c)),
        out_shape=jax.ShapeDtypeStruct((bsz * seq, D_RNN), BF16),
        scratch_shapes=[pltpu.VMEM((D_MODEL, 2 * RG_TILE), BF16),
                        pltpu.VMEM((PB, 2 * RG_TILE), F32),
                        pltpu.VMEM((PB, 2 * RG_TILE), F32),
                        pltpu.VMEM((HALO + PB, RG_TILE), F32),
                        pltpu.VMEM((PB, RG_TILE), F32),
                        pltpu.VMEM((PB, RG_TILE), F32),
                        pltpu.VMEM((1, RG_TILE), F32)],
        compiler_params=_params("parallel", "parallel", "arbitrary"),
        name="rglru",
    )(xp, xp, xp, w_in, w_in, conv_w, conv_b, w_gate, ba, bx, lam)


def _s5_discretize_kernel(are_ref, aim_ref, ldt_ref, lbr_ref, lbi_ref, fr_ref, fi_ref):
    dt = jnp.exp(ldt_ref[...])
    lr = jnp.minimum(are_ref[...], -1e-4)
    li = aim_ref[...]
    mag = jnp.exp(lr * dt)
    lbr = mag * jnp.cos(li * dt)
    lbi = mag * jnp.sin(li * dt)
    zr, zi = lbr - 1.0, lbi
    den = lr * lr + li * li
    lbr_ref[...] = lbr
    lbi_ref[...] = lbi
    fr_ref[...] = (zr * lr + zi * li) / den
    fi_ref[...] = (zi * lr - zr * li) / den


def _s5_discretize(a_re, a_im, log_dt):
    shp = jax.ShapeDtypeStruct((SSM_GROUPS, SSM_STATE), F32)
    return pl.pallas_call(
        _s5_discretize_kernel,
        out_shape=(shp, shp, shp, shp),
        name="s5_discretize",
    )(a_re, a_im, log_dt.reshape(SSM_GROUPS, 1))


def _cmul(ar, ai, br, bi):
    return ar * br - ai * bi, ar * bi + ai * br


def _s5_project(u_ref, bmat_ref, xr_ref, xi_ref):
    ns = S5_TILE_STATES
    u = u_ref[...]
    xr_ref[...] = jnp.dot(u, bmat_ref[0, :, :ns], preferred_element_type=F32)
    xi_ref[...] = jnp.dot(u, bmat_ref[0, :, ns:], preferred_element_type=F32)


def _s5_scan(xr_ref, xi_ref, hb_ref, lre_ref, lim_ref, tab_re_ref, tab_im_ref,
             pow_re_ref, pow_im_ref, step_re_ref, step_im_ref, cr_ref, ci_ref):
    ns = S5_TILE_STATES
    grp = (SUBLANES, ns)
    lr = jnp.broadcast_to(lre_ref[...], grp)
    li = jnp.broadcast_to(lim_ref[...], grp)

    er, ei = xr_ref[0:SUBLANES, :], xi_ref[0:SUBLANES, :]
    for tau in range(1, SEG):
        rows = slice(tau * SUBLANES, (tau + 1) * SUBLANES)
        nr = (lr * er - li * ei) + xr_ref[rows, :]
        ni = (lr * ei + li * er) + xi_ref[rows, :]
        xr_ref[rows, :] = nr
        xi_ref[rows, :] = ni
        er, ei = nr, ni

    for k, d in enumerate((1, 2, 4)):
        sr, si = pltpu.roll(er, d, 0), pltpu.roll(ei, d, 0)
        mr, mi = _cmul(step_re_ref[k], step_im_ref[k], sr, si)
        er, ei = er + mr, ei + mi
    cr, ci = cr_ref[...], ci_ref[...]
    mr, mi = _cmul(pow_re_ref[...], pow_im_ref[...], cr, ci)
    er, ei = er + mr, ei + mi
    first = _row_ids(grp) == 0
    in_r = jnp.where(first, cr, pltpu.roll(er, 1, 0))
    in_i = jnp.where(first, ci, pltpu.roll(ei, 1, 0))
    cr_ref[...] = er[SUBLANES - 1:SUBLANES, :]
    ci_ref[...] = ei[SUBLANES - 1:SUBLANES, :]

    for c0 in range(0, ns, S5_FINISH_LANES):
        cols = slice(c0, c0 + S5_FINISH_LANES)
        in_r2 = jnp.concatenate([in_r[:, cols], in_r[:, cols]], axis=0)
        in_i2 = jnp.concatenate([in_i[:, cols], in_i[:, cols]], axis=0)
        for i in range(PB // BF16_ROWS):
            rows = slice(i * BF16_ROWS, (i + 1) * BF16_ROWS)
            fr, fi = _cmul(tab_re_ref[rows, cols], tab_im_ref[rows, cols], in_r2, in_i2)
            hb_ref[rows, c0:c0 + S5_FINISH_LANES] = (xr_ref[rows, cols] + fr).astype(BF16)
            hb_ref[rows, ns + c0:ns + c0 + S5_FINISH_LANES] = (xi_ref[rows, cols] + fi).astype(BF16)


def _s5_output(hb_ref, u_ref, cmat_ref, d_ref, o_ref):
    ns = S5_TILE_STATES
    y = (jnp.dot(hb_ref[:, :ns], cmat_ref[0, :ns, :], preferred_element_type=F32)
         + jnp.dot(hb_ref[:, ns:], cmat_ref[0, ns:, :], preferred_element_type=F32)
         + d_ref[...] * u_ref[...].astype(F32))
    o_ref[...] = jax.nn.gelu(y).astype(o_ref.dtype)


def _s5_kernel(u0_ref, un_ref, up_ref, bmat_ref, cmat_ref, lre_ref, lim_ref, d_ref, o_ref,
               x0r_ref, x0i_ref, x1r_ref, x1i_ref, h0_ref, h1_ref, tab_re_ref, tab_im_ref,
               pow_re_ref, pow_im_ref, step_re_ref, step_im_ref, cr_ref, ci_ref):
    t = pl.program_id(2)
    ns = S5_TILE_STATES
    grp = (SUBLANES, ns)

    @pl.when(t == 0)
    def _():
        _s5_project(u0_ref, bmat_ref, x0r_ref, x0i_ref)
        h1_ref[...] = jnp.zeros_like(h1_ref)
        cr_ref[...] = jnp.zeros_like(cr_ref)
        ci_ref[...] = jnp.zeros_like(ci_ref)
        lr = jnp.broadcast_to(lre_ref[...], grp)
        li = jnp.broadcast_to(lim_ref[...], grp)

        def fill(tau, p):
            pr, pi = p
            rows = pl.ds(pl.multiple_of(tau * SUBLANES, SUBLANES), SUBLANES)
            tab_re_ref[rows, :] = pr
            tab_im_ref[rows, :] = pi
            return _cmul(pr, pi, lr, li)

        jax.lax.fori_loop(0, SEG, fill, (lr, li))
        mr = tab_re_ref[PB - SUBLANES:PB, :]
        mi = tab_im_ref[PB - SUBLANES:PB, :]
        row = _row_ids(grp)
        pr, pi = mr, mi
        for r in range(SUBLANES):
            pow_re_ref[r:r + 1, :] = pr[0:1, :]
            pow_im_ref[r:r + 1, :] = pi[0:1, :]
            if r + 1 in (1, 2, 4):
                k = (1, 2, 4).index(r + 1)
                step_re_ref[k] = jnp.where(row >= r + 1, pr, 0.0)
                step_im_ref[k] = jnp.where(row >= r + 1, pi, 0.0)
            pr, pi = _cmul(pr, pi, mr, mi)

    scan_refs = (lre_ref, lim_ref, tab_re_ref, tab_im_ref, pow_re_ref, pow_im_ref,
                 step_re_ref, step_im_ref, cr_ref, ci_ref)

    def step(xr_cur, xi_cur, h_cur, xr_nxt, xi_nxt, h_prv):
        _s5_project(un_ref, bmat_ref, xr_nxt, xi_nxt)
        _s5_scan(xr_cur, xi_cur, h_cur, *scan_refs)
        _s5_output(h_prv, up_ref, cmat_ref, d_ref, o_ref)

    @pl.when(t % 2 == 0)
    def _():
        step(x0r_ref, x0i_ref, h0_ref, x1r_ref, x1i_ref, h1_ref)

    @pl.when(t % 2 == 1)
    def _():
        step(x1r_ref, x1i_ref, h1_ref, x0r_ref, x0i_ref, h0_ref)


def _s5(z, bmat, cmat, lam_re, lam_im, dvec, bsz, seq):
    n_t = seq // PB
    n_tiles = D_SSM // S5_TILE
    ns = S5_TILE_STATES
    vec = lambda b, c, t: (0, c)
    blk = lambda f: pl.BlockSpec((PB, S5_TILE), lambda b, c, t: (b * n_t + f(t), c))
    return pl.pallas_call(
        _s5_kernel,
        grid=(bsz, n_tiles, n_t + 1),
        in_specs=[blk(lambda t: 0),
                  blk(lambda t: jnp.minimum(t + 1, n_t - 1)),
                  blk(lambda t: jnp.maximum(t - 1, 0)),
                  pl.BlockSpec((1, S5_TILE, 2 * ns), lambda b, c, t: (c, 0, 0)),
                  pl.BlockSpec((1, 2 * ns, S5_TILE), lambda b, c, t: (c, 0, 0)),
                  pl.BlockSpec((1, ns), vec),
                  pl.BlockSpec((1, ns), vec),
                  pl.BlockSpec((1, S5_TILE), vec)],
        out_specs=blk(lambda t: jnp.maximum(t - 1, 0)),
        out_shape=jax.ShapeDtypeStruct((bsz * seq, D_SSM), BF16),
        scratch_shapes=[pltpu.VMEM((PB, ns), F32),
                        pltpu.VMEM((PB, ns), F32),
                        pltpu.VMEM((PB, ns), F32),
                        pltpu.VMEM((PB, ns), F32),
                        pltpu.VMEM((PB, 2 * ns), BF16),
                        pltpu.VMEM((PB, 2 * ns), BF16),
                        pltpu.VMEM((PB, ns), F32),
                        pltpu.VMEM((PB, ns), F32),
                        pltpu.VMEM((SUBLANES, ns), F32),
                        pltpu.VMEM((SUBLANES, ns), F32),
                        pltpu.VMEM((3, SUBLANES, ns), F32),
                        pltpu.VMEM((3, SUBLANES, ns), F32),
                        pltpu.VMEM((1, ns), F32),
                        pltpu.VMEM((1, ns), F32)],
        compiler_params=_params("parallel", "parallel", "arbitrary"),
        name="s5",
    )(z, z, z, bmat, cmat, lam_re, lam_im, dvec)


def _mix_kernel(hg_ref, y_ref, ga_ref, gb_ref, wa_ref, gw_ref, gv_ref, o_ref,
                wab_ref, gwb_ref, gvb_ref):
    @pl.when(pl.program_id(1) == 0)
    def _():
        wab_ref[...] = wa_ref[...].astype(BF16)
        gwb_ref[...] = gw_ref[...].astype(BF16)
        gvb_ref[...] = gv_ref[...].astype(BF16)

    y_a = jnp.dot(hg_ref[...], wab_ref[...], preferred_element_type=F32)
    y = y_ref[...]
    y_b = (jnp.dot(y, gwb_ref[...], preferred_element_type=F32)
           * jax.nn.sigmoid(jnp.dot(y, gvb_ref[...], preferred_element_type=F32)))
    mix = (jax.nn.sigmoid(ga_ref[...].astype(F32)) * y_a
           + jax.nn.sigmoid(gb_ref[...].astype(F32)) * y_b)
    o_ref[...] = mix.astype(o_ref.dtype)


def _mix(hg, y, z, w_a, glu_w, glu_v, tm, tn):
    m = hg.shape[0]
    return pl.pallas_call(
        _mix_kernel,
        grid=(D_MODEL // tn, m // tm),
        in_specs=[pl.BlockSpec((tm, D_RNN), lambda j, i: (i, 0)),
                  pl.BlockSpec((tm, D_SSM), lambda j, i: (i, 0)),
                  pl.BlockSpec((tm, tn), lambda j, i: (i, Z_COL_GA // tn + j)),
                  pl.BlockSpec((tm, tn), lambda j, i: (i, Z_COL_GB // tn + j)),
                  pl.BlockSpec((D_RNN, tn), lambda j, i: (0, j)),
                  pl.BlockSpec((D_SSM, tn), lambda j, i: (0, j)),
                  pl.BlockSpec((D_SSM, tn), lambda j, i: (0, j))],
        out_specs=pl.BlockSpec((tm, tn), lambda j, i: (i, j)),
        out_shape=jax.ShapeDtypeStruct((m, D_MODEL), BF16),
        scratch_shapes=[pltpu.VMEM((D_RNN, tn), BF16),
                        pltpu.VMEM((D_SSM, tn), BF16),
                        pltpu.VMEM((D_SSM, tn), BF16)],
        compiler_params=_params("parallel", "arbitrary"),
        name="mix",
    )(hg, y, z, z, w_a, glu_w, glu_v)


def _layernorm(v, g, b):
    mu = jnp.mean(v, axis=-1, keepdims=True)
    c = v - mu
    var = jnp.mean(c * c, axis=-1, keepdims=True)
    return c * jax.lax.rsqrt(var + LN_EPS) * g + b


def _outproj_ln_kernel(mix_ref, pt_ref, x_ref, w_ref, g_ref, b_ref, o_ref, ob_ref, *, alpha):
    mix = jnp.dot(pt_ref[...], mix_ref[...], preferred_element_type=F32).astype(BF16)
    v = alpha * x_ref[...] + jnp.dot(mix, w_ref[...], preferred_element_type=F32)
    out = _layernorm(v, g_ref[...], b_ref[...])
    o_ref[...] = out
    ob_ref[...] = out.astype(BF16)


def _outproj_ln(mix, perm_t, x, w_out, g, b, alpha):
    m = mix.shape[0]
    row = lambda i: (i, 0)
    fixed = lambda i: (0, 0)
    return pl.pallas_call(
        functools.partial(_outproj_ln_kernel, alpha=alpha),
        grid=(m // PB,),
        in_specs=[pl.BlockSpec((PB, D_MODEL), row),
                  pl.BlockSpec((PB, PB), fixed),
                  pl.BlockSpec((PB, D_MODEL), row),
                  pl.BlockSpec((D_MODEL, D_MODEL), fixed),
                  pl.BlockSpec((1, D_MODEL), fixed),
                  pl.BlockSpec((1, D_MODEL), fixed)],
        out_specs=(pl.BlockSpec((PB, D_MODEL), row), pl.BlockSpec((PB, D_MODEL), row)),
        out_shape=(jax.ShapeDtypeStruct((m, D_MODEL), F32),
                   jax.ShapeDtypeStruct((m, D_MODEL), BF16)),
        compiler_params=_params("parallel"),
        name="outproj_ln",
    )(mix, perm_t, x, w_out, g, b)


def _mlp_up_kernel(x_ref, w_ref, b_ref, o_ref):
    v = jnp.dot(x_ref[...], w_ref[...], preferred_element_type=F32) + b_ref[...]
    v = jnp.maximum(v, 0.0)
    o_ref[...] = (v * v).astype(o_ref.dtype)


def _mlp_up(xb, w_up, b_up, tm, tn):
    m = xb.shape[0]
    return pl.pallas_call(
        _mlp_up_kernel,
        grid=(m // tm, D_FF // tn),
        in_specs=[pl.BlockSpec((tm, D_MODEL), lambda i, j: (i, 0)),
                  pl.BlockSpec((D_MODEL, tn), lambda i, j: (0, j)),
                  pl.BlockSpec((1, tn), lambda i, j: (0, j))],
        out_specs=pl.BlockSpec((tm, tn), lambda i, j: (i, j)),
        out_shape=jax.ShapeDtypeStruct((m, D_FF), BF16),
        compiler_params=_params("parallel", "arbitrary"),
        name="mlp_up",
    )(xb, w_up, b_up)


def _mlp_down_ln_kernel(a_ref, w_ref, x_ref, bd_ref, g_ref, b_ref, o_ref, *, alpha):
    k = pl.program_id(1)
    last = pl.num_programs(1) - 1

    def part():
        return jnp.dot(a_ref[...], w_ref[...], preferred_element_type=F32)

    @pl.when(k == 0)
    def _():
        o_ref[...] = part()

    @pl.when(jnp.logical_and(k > 0, k < last))
    def _():
        o_ref[...] += part()

    @pl.when(k == last)
    def _():
        v = alpha * x_ref[...] + (o_ref[...] + part()) + bd_ref[...]
        o_ref[...] = _layernorm(v, g_ref[...], b_ref[...])


def _mlp_down_ln(a, w_down, x1, b_down, g, b, alpha, tm, tk):
    m = a.shape[0]
    fixed = lambda i, k: (0, 0)
    return pl.pallas_call(
        functools.partial(_mlp_down_ln_kernel, alpha=alpha),
        grid=(m // tm, D_FF // tk),
        in_specs=[pl.BlockSpec((tm, tk), lambda i, k: (i, k)),
                  pl.BlockSpec((tk, D_MODEL), lambda i, k: (k, 0)),
                  pl.BlockSpec((tm, D_MODEL), lambda i, k: (i, 0)),
                  pl.BlockSpec((1, D_MODEL), fixed),
                  pl.BlockSpec((1, D_MODEL), fixed),
                  pl.BlockSpec((1, D_MODEL), fixed)],
        out_specs=pl.BlockSpec((tm, D_MODEL), lambda i, k: (i, 0)),
        out_shape=jax.ShapeDtypeStruct((m, D_MODEL), F32),
        compiler_params=_params("parallel", "arbitrary"),
        name="mlp_down_ln",
    )(a, w_down, x1, b_down, g, b)


def _s5_block_matrices(fr, fi, b_re, b_im, c_re, c_im):
    bbr = fr[..., None] * b_re - fi[..., None] * b_im
    bbi = fr[..., None] * b_im + fi[..., None] * b_re
    n_tiles = D_SSM // S5_TILE
    eye = jnp.eye(S5_TILE_GROUPS, dtype=F32)

    def in_blocks(w):
        w = w.reshape(n_tiles, S5_TILE_GROUPS, SSM_STATE, SSM_GROUP)
        m = jnp.einsum("tgph,gk->tghkp", w, eye)
        return m.reshape(n_tiles, S5_TILE, S5_TILE_STATES)

    def out_blocks(w):
        w = w.reshape(n_tiles, S5_TILE_GROUPS, SSM_GROUP, SSM_STATE)
        m = jnp.einsum("tghp,gk->tgpkh", w, eye)
        return m.reshape(n_tiles, S5_TILE_STATES, S5_TILE)

    bmat = jnp.concatenate([in_blocks(bbr), in_blocks(bbi)], axis=2).astype(BF16)
    cmat = jnp.concatenate([out_blocks(c_re), out_blocks(-c_im)], axis=1).astype(BF16)
    return bmat, cmat


def kernel(x, w_in, conv_w, conv_b, rg_wa, rg_ba, rg_wx, rg_bx, rg_lambda, w_a_out, ssm_a_re, ssm_a_im, ssm_log_dt, ssm_b_re, ssm_b_im, ssm_c_re, ssm_c_im, ssm_d, glu_w, glu_v, w_out, ln1_g, ln1_b, mlp_w_up, mlp_b_up, mlp_w_down, mlp_b_down, ln2_g, ln2_b):
    bsz, seq, _ = x.shape
    assert seq % PB == 0
    m = bsz * seq
    depth = w_in.shape[0]
    alpha = (2.0 * depth) ** 0.25
    perm = _interleave_matrix()
    perm_fwd = jnp.asarray(perm, BF16)
    perm_bwd = jnp.asarray(perm.T, BF16)
    for l in range(depth):
        xf = x.reshape(m, D_MODEL)
        xp = _interleave(xf, perm_fwd)
        z = _in_proj(xp, w_in[l], tm=1024, tn=1024)

        w_gate = jnp.concatenate([rg_wa[l], rg_wx[l]], axis=-1).astype(BF16)
        hg = _rglru(xp, w_in[l], conv_w[l], conv_b[l].reshape(1, D_RNN), w_gate,
                    rg_ba[l].reshape(1, D_RNN), rg_bx[l].reshape(1, D_RNN),
                    rg_lambda[l].reshape(1, D_RNN), bsz, seq)

        lbr, lbi, fr, fi = _s5_discretize(ssm_a_re[l], ssm_a_im[l], ssm_log_dt[l])
        bmat, cmat = _s5_block_matrices(fr, fi, ssm_b_re[l], ssm_b_im[l],
                                        ssm_c_re[l], ssm_c_im[l])
        y = _s5(z, bmat, cmat, lbr.reshape(1, -1), lbi.reshape(1, -1),
                ssm_d[l].reshape(1, D_SSM), bsz, seq)

        mix = _mix(hg, y, z, w_a_out[l], glu_w[l], glu_v[l], tm=1024, tn=512)
        x1, x1b = _outproj_ln(mix, perm_bwd, xf, w_out[l].astype(BF16),
                              ln1_g[l].reshape(1, D_MODEL), ln1_b[l].reshape(1, D_MODEL),
                              alpha)
        a = _mlp_up(x1b, mlp_w_up[l].astype(BF16), mlp_b_up[l].reshape(1, D_FF),
                    tm=1024, tn=2048)
        x2 = _mlp_down_ln(a, mlp_w_down[l].astype(BF16), x1,
                          mlp_b_down[l].reshape(1, D_MODEL),
                          ln2_g[l].reshape(1, D_MODEL), ln2_b[l].reshape(1, D_MODEL),
                          alpha, tm=1024, tk=1024)
        x = x2.reshape(bsz, seq, D_MODEL)
    return x
```

```python
import functools

import numpy as np
import jax
import jax.numpy as jnp
from jax.experimental import pallas as pl
from jax.experimental.pallas import tpu as pltpu

F32 = jnp.float32
BF16 = jnp.bfloat16

D_MODEL = 2048
D_RNN = D_MODEL
RG_HEADS = 16
RG_HEAD_DIM = D_RNN // RG_HEADS
CONV_WIDTH = 4
RG_C = 8.0
D_SSM = D_MODEL // 2
SSM_GROUP = 16
SSM_GROUPS = D_SSM // SSM_GROUP
SSM_STATE = 64
D_FF = 4 * D_MODEL
D_IN = 2 * D_RNN + D_SSM + 2 * D_MODEL
LN_EPS = 1e-5

SUBLANES = 8
BF16_ROWS = 2 * SUBLANES
VMEM_LIMIT = 56 * 1024 * 1024

NSEG = SUBLANES
PB = 512
SEG = PB // NSEG
HALO = (CONV_WIDTH - 1) * SUBLANES

RG_TILE = 512
S5_TILE = 256
S5_TILE_GROUPS = S5_TILE // SSM_GROUP
S5_TILE_STATES = S5_TILE_GROUPS * SSM_STATE
S5_FINISH_LANES = 512
W_COL_GATE = D_RNN
W_COL_REST = 2 * D_RNN
Z_COLS = D_SSM + 2 * D_MODEL
Z_COL_GA = D_SSM
Z_COL_GB = D_SSM + D_MODEL


def _params(*sem):
    return pltpu.CompilerParams(dimension_semantics=sem, vmem_limit_bytes=VMEM_LIMIT)


def _interleave_matrix():
    p = np.arange(PB)
    src = (p % NSEG) * SEG + p // NSEG
    mat = np.zeros((PB, PB), np.float32)
    mat[p, src] = 1.0
    return mat


def _row_ids(shape):
    return jax.lax.broadcasted_iota(jnp.int32, shape, 0)


def _shift_rows(v, d, fill):
    return jnp.where(_row_ids(v.shape) >= d, pltpu.roll(v, d, 0), fill)


def _interleave_kernel(x_ref, p_ref, o_ref):
    o_ref[...] = jnp.dot(p_ref[...], x_ref[...].astype(BF16),
                         preferred_element_type=F32).astype(BF16)


def _interleave(x, perm):
    m, k = x.shape
    return pl.pallas_call(
        _interleave_kernel,
        grid=(m // PB,),
        in_specs=[pl.BlockSpec((PB, k), lambda i: (i, 0)),
                  pl.BlockSpec((PB, PB), lambda i: (0, 0))],
        out_specs=pl.BlockSpec((PB, k), lambda i: (i, 0)),
        out_shape=jax.ShapeDtypeStruct((m, k), BF16),
        compiler_params=_params("parallel"),
        name="interleave",
    )(x, perm)


def _in_proj_kernel(x_ref, w_ref, o_ref, wb_ref):
    @pl.when(pl.program_id(1) == 0)
    def _():
        wb_ref[...] = w_ref[...].astype(BF16)

    o_ref[...] = jnp.dot(x_ref[...], wb_ref[...],
                         preferred_element_type=F32).astype(o_ref.dtype)


def _in_proj(xp, w, tm, tn):
    m, k = xp.shape
    col0 = W_COL_REST // tn
    return pl.pallas_call(
        _in_proj_kernel,
        grid=(Z_COLS // tn, m // tm),
        in_specs=[pl.BlockSpec((tm, k), lambda j, i: (i, 0)),
                  pl.BlockSpec((k, tn), lambda j, i: (0, col0 + j))],
        out_specs=pl.BlockSpec((tm, tn), lambda j, i: (i, j)),
        out_shape=jax.ShapeDtypeStruct((m, Z_COLS), BF16),
        scratch_shapes=[pltpu.VMEM((k, tn), BF16)],
        compiler_params=_params("parallel", "arbitrary"),
        name="in_proj",
    )(xp, w)


def _rglru_gates(hh, z_ref, cw_ref, cb_ref, w_ref, ext_ref, xc_ref, pre_ref, gg_ref):
    sl = slice(hh * RG_HEAD_DIM, (hh + 1) * RG_HEAD_DIM)
    gg_ref[:, sl] = jax.nn.gelu(
        z_ref[:, RG_TILE + hh * RG_HEAD_DIM:RG_TILE + (hh + 1) * RG_HEAD_DIM])
    grp = (SUBLANES, RG_HEAD_DIM)
    x = z_ref[:, sl]
    first = _row_ids(grp) == 0
    halos = []
    for k in range(CONV_WIDTH - 1):
        prev_g = ext_ref[PB + k * SUBLANES:PB + (k + 1) * SUBLANES, sl]
        cur_g = x[PB - HALO + k * SUBLANES:PB - HALO + (k + 1) * SUBLANES, :]
        halos.append(jnp.where(first, pltpu.roll(prev_g, 1, 0), pltpu.roll(cur_g, 1, 0)))
    for k in range(CONV_WIDTH - 1):
        ext_ref[k * SUBLANES:(k + 1) * SUBLANES, sl] = halos[k]
    ext_ref[HALO:HALO + PB, sl] = x

    xc = cb_ref[:, sl] + cw_ref[CONV_WIDTH - 1:CONV_WIDTH, sl] * x
    for k in range(CONV_WIDTH - 1):
        xc = xc + cw_ref[k:k + 1, sl] * ext_ref[k * SUBLANES:k * SUBLANES + PB, sl]

    xc_ref[:, sl] = xc
    pre_ref[:, 2 * hh * RG_HEAD_DIM:2 * (hh + 1) * RG_HEAD_DIM] = jnp.dot(
        xc.astype(BF16), w_ref[hh], preferred_element_type=F32)


def _rglru_head(hh, o_ref, row0, cvec, ba_ref, bx_ref, xc_ref, pre_ref, gg_ref,
                a_ref, b_ref, carry_ref):
    sl = slice(hh * RG_HEAD_DIM, (hh + 1) * RG_HEAD_DIM)
    grp = (SUBLANES, RG_HEAD_DIM)
    first = _row_ids(grp) == 0
    xc = xc_ref[:, sl]
    r = jax.nn.sigmoid(pre_ref[:, 2 * hh * RG_HEAD_DIM:(2 * hh + 1) * RG_HEAD_DIM] + ba_ref[:, sl])
    i = jax.nn.sigmoid(pre_ref[:, (2 * hh + 1) * RG_HEAD_DIM:(2 * hh + 2) * RG_HEAD_DIM]
                       + bx_ref[:, sl])
    log_a = cvec[:, sl] * r
    a = jnp.exp(log_a)
    a_ref[:, sl] = a
    one_minus_a2 = -jnp.tanh(log_a) * (1.0 + a * a)
    b_ref[:, sl] = jnp.sqrt(one_minus_a2) * (i * xc)

    h_end, a_end = b_ref[0:SUBLANES, sl], a_ref[0:SUBLANES, sl]
    for tau in range(1, SEG):
        rows = slice(tau * SUBLANES, (tau + 1) * SUBLANES)
        a = a_ref[rows, sl]
        h_end = a * h_end + b_ref[rows, sl]
        a_end = a * a_end
        b_ref[rows, sl] = h_end
        a_ref[rows, sl] = a_end

    for d in (1, 2, 4):
        h_end = a_end * _shift_rows(h_end, d, 0.0) + h_end
        a_end = a_end * _shift_rows(a_end, d, 1.0)
    carry = carry_ref[:, sl]
    h_true = h_end + a_end * carry
    h_in = jnp.where(first, carry, pltpu.roll(h_true, 1, 0))
    carry_ref[:, sl] = h_true[SUBLANES - 1:SUBLANES, :]

    h_in2 = jnp.concatenate([h_in, h_in], axis=0)
    for j in range(PB // BF16_ROWS):
        rows = slice(j * BF16_ROWS, (j + 1) * BF16_ROWS)
        h = b_ref[rows, sl] + a_ref[rows, sl] * h_in2
        o_ref[row0 + j * BF16_ROWS:row0 + (j + 1) * BF16_ROWS, sl] = (
            h * gg_ref[rows, sl]).astype(o_ref.dtype)


def _rglru_kernel(x0_ref, x1_ref, x2_ref, wx_ref, wg_ref, cw_ref, cb_ref, w_ref,
                  ba_ref, bx_ref, lam_ref, o_ref,
                  wb_ref, z_ref, ext_ref, xca_ref, prea_ref, gga_ref, a_ref, b_ref, carry_ref):
    t = pl.program_id(2)
    n_heads = RG_TILE // RG_HEAD_DIM
    bufs = (xca_ref, prea_ref, gga_ref)

    def project(x_ref):
        z_ref[...] = jnp.dot(x_ref[...], wb_ref[...], preferred_element_type=F32)

    def gates():
        for hh in range(n_heads):
            _rglru_gates(hh, z_ref, cw_ref, cb_ref, w_ref, ext_ref, *bufs)

    @pl.when(t == 0)
    def _():
        wb_ref[:, :RG_TILE] = wx_ref[...].astype(BF16)
        wb_ref[:, RG_TILE:] = wg_ref[...].astype(BF16)
        ext_ref[PB:PB + HALO, :] = jnp.zeros((HALO, RG_TILE), F32)
        carry_ref[...] = jnp.zeros_like(carry_ref)
        project(x0_ref)
        gates()

    neg = -lam_ref[...]
    softplus = jnp.maximum(neg, 0.0) + jnp.log1p(jnp.exp(-jnp.abs(neg)))
    cvec = -RG_C * softplus

    def back(row0):
        for hh in range(n_heads):
            _rglru_head(hh, o_ref, row0, cvec, ba_ref, bx_ref, *bufs, a_ref, b_ref, carry_ref)

    project(x1_ref)
    back(0)
    gates()
    project(x2_ref)
    back(PB)
    gates()


def _rglru(xp, w_in, conv_w, conv_b, w_gate, ba, bx, lam, bsz, seq):
    n_t = seq // PB
    assert n_t % 2 == 0
    n_c = D_RNN // RG_TILE
    vec = lambda b, c, t: (0, c)
    return pl.pallas_call(
        _rglru_kernel,
        grid=(bsz, n_c, n_t // 2),
        in_specs=[pl.BlockSpec((PB, D_MODEL), lambda b, c, t: (b * n_t, 0)),
                  pl.BlockSpec((PB, D_MODEL), lambda b, c, t: (b * n_t + 2 * t + 1, 0)),
                  pl.BlockSpec((PB, D_MODEL),
                               lambda b, c, t: (b * n_t + jnp.minimum(2 * t + 2, n_t - 1), 0)),
                  pl.BlockSpec((D_MODEL, RG_TILE), vec),
                  pl.BlockSpec((D_MODEL, RG_TILE), lambda b, c, t: (0, W_COL_GATE // RG_TILE + c)),
                  pl.BlockSpec((CONV_WIDTH, RG_TILE), vec),
                  pl.BlockSpec((1, RG_TILE), vec),
                  pl.BlockSpec((RG_TILE // RG_HEAD_DIM, RG_HEAD_DIM, 2 * RG_HEAD_DIM),
                               lambda b, c, t: (c, 0, 0)),
                  pl.BlockSpec((1, RG_TILE), vec),
                  pl.BlockSpec((1, RG_TILE), vec),
                  pl.BlockSpec((1, RG_TILE), vec)],
        out_specs=pl.BlockSpec((2 * PB, RG_TILE), lambda b, c, t: (b * (n_t // 2) + t, c)),
        out_shape=jax.ShapeDtypeStruct((bsz * seq, D_RNN), BF16),
        scratch_shapes=[pltpu.VMEM((D_MODEL, 2 * RG_TILE), BF16),
                        pltpu.VMEM((PB, 2 * RG_TILE), F32),
                        pltpu.VMEM((HALO + PB, RG_TILE), F32),
                        pltpu.VMEM((PB, RG_TILE), F32),
                        pltpu.VMEM((PB, 2 * RG_TILE), F32),
                        pltpu.VMEM((PB, RG_TILE), F32),
                        pltpu.VMEM((PB, RG_TILE), F32),
                        pltpu.VMEM((PB, RG_TILE), F32),
                        pltpu.VMEM((1, RG_TILE), F32)],
        compiler_params=_params("parallel", "parallel", "arbitrary"),
        name="rglru",
    )(xp, xp, xp, w_in, w_in, conv_w, conv_b, w_gate, ba, bx, lam)


def _s5_discretize_kernel(are_ref, aim_ref, ldt_ref, lbr_ref, lbi_ref, fr_ref, fi_ref):
    dt = jnp.exp(ldt_ref[...])
    lr = jnp.minimum(are_ref[...], -1e-4)
    li = aim_ref[...]
    mag = jnp.exp(lr * dt)
    lbr = mag * jnp.cos(li * dt)
    lbi = mag * jnp.sin(li * dt)
    zr, zi = lbr - 1.0, lbi
    den = lr * lr + li * li
    lbr_ref[...] = lbr
    lbi_ref[...] = lbi
    fr_ref[...] = (zr * lr + zi * li) / den
    fi_ref[...] = (zi * lr - zr * li) / den


def _s5_discretize(a_re, a_im, log_dt):
    shp = jax.ShapeDtypeStruct((SSM_GROUPS, SSM_STATE), F32)
    return pl.pallas_call(
        _s5_discretize_kernel,
        out_shape=(shp, shp, shp, shp),
        name="s5_discretize",
    )(a_re, a_im, log_dt.reshape(SSM_GROUPS, 1))


def _cmul(ar, ai, br, bi):
    return ar * br - ai * bi, ar * bi + ai * br


def _s5_project(u_ref, bmat_ref, xr_ref, xi_ref):
    ns = S5_TILE_STATES
    u = u_ref[...]
    xr_ref[...] = jnp.dot(u, bmat_ref[0, :, :ns], preferred_element_type=F32)
    xi_ref[...] = jnp.dot(u, bmat_ref[0, :, ns:], preferred_element_type=F32)


def _s5_scan(xr_ref, xi_ref, hb_ref, lre_ref, lim_ref, tab_re_ref, tab_im_ref,
             pow_re_ref, pow_im_ref, step_re_ref, step_im_ref, cr_ref, ci_ref):
    ns = S5_TILE_STATES
    grp = (SUBLANES, ns)
    lr = jnp.broadcast_to(lre_ref[...], grp)
    li = jnp.broadcast_to(lim_ref[...], grp)

    er, ei = xr_ref[0:SUBLANES, :], xi_ref[0:SUBLANES, :]
    for tau in range(1, SEG):
        rows = slice(tau * SUBLANES, (tau + 1) * SUBLANES)
        nr = (lr * er - li * ei) + xr_ref[rows, :]
        ni = (lr * ei + li * er) + xi_ref[rows, :]
        xr_ref[rows, :] = nr
        xi_ref[rows, :] = ni
        er, ei = nr, ni

    for k, d in enumerate((1, 2, 4)):
        sr, si = pltpu.roll(er, d, 0), pltpu.roll(ei, d, 0)
        mr, mi = _cmul(step_re_ref[k], step_im_ref[k], sr, si)
        er, ei = er + mr, ei + mi
    cr, ci = cr_ref[...], ci_ref[...]
    mr, mi = _cmul(pow_re_ref[...], pow_im_ref[...], cr, ci)
    er, ei = er + mr, ei + mi
    first = _row_ids(grp) == 0
    in_r = jnp.where(first, cr, pltpu.roll(er, 1, 0))
    in_i = jnp.where(first, ci, pltpu.roll(ei, 1, 0))
    cr_ref[...] = er[SUBLANES - 1:SUBLANES, :]
    ci_ref[...] = ei[SUBLANES - 1:SUBLANES, :]

    for c0 in range(0, ns, S5_FINISH_LANES):
        cols = slice(c0, c0 + S5_FINISH_LANES)
        in_r2 = jnp.concatenate([in_r[:, cols], in_r[:, cols]], axis=0)
        in_i2 = jnp.concatenate([in_i[:, cols], in_i[:, cols]], axis=0)
        for i in range(PB // BF16_ROWS):
            rows = slice(i * BF16_ROWS, (i + 1) * BF16_ROWS)
            fr, fi = _cmul(tab_re_ref[rows, cols], tab_im_ref[rows, cols], in_r2, in_i2)
            hb_ref[rows, c0:c0 + S5_FINISH_LANES] = (xr_ref[rows, cols] + fr).astype(BF16)
            hb_ref[rows, ns + c0:ns + c0 + S5_FINISH_LANES] = (xi_ref[rows, cols] + fi).astype(BF16)


def _s5_output(hb_ref, u_ref, cmat_ref, d_ref, o_ref):
    ns = S5_TILE_STATES
    y = (jnp.dot(hb_ref[:, :ns], cmat_ref[0, :ns, :], preferred_element_type=F32)
         + jnp.dot(hb_ref[:, ns:], cmat_ref[0, ns:, :], preferred_element_type=F32)
         + d_ref[...] * u_ref[...].astype(F32))
    o_ref[...] = jax.nn.gelu(y).astype(o_ref.dtype)


def _s5_kernel(u0_ref, un_ref, up_ref, bmat_ref, cmat_ref, lre_ref, lim_ref, d_ref, o_ref,
               x0r_ref, x0i_ref, x1r_ref, x1i_ref, h0_ref, h1_ref, tab_re_ref, tab_im_ref,
               pow_re_ref, pow_im_ref, step_re_ref, step_im_ref, cr_ref, ci_ref):
    t = pl.program_id(2)
    ns = S5_TILE_STATES
    grp = (SUBLANES, ns)

    @pl.when(t == 0)
    def _():
        _s5_project(u0_ref, bmat_ref, x0r_ref, x0i_ref)
        h1_ref[...] = jnp.zeros_like(h1_ref)
        cr_ref[...] = jnp.zeros_like(cr_ref)
        ci_ref[...] = jnp.zeros_like(ci_ref)
        lr = jnp.broadcast_to(lre_ref[...], grp)
        li = jnp.broadcast_to(lim_ref[...], grp)

        def fill(tau, p):
            pr, pi = p
            rows = pl.ds(pl.multiple_of(tau * SUBLANES, SUBLANES), SUBLANES)
            tab_re_ref[rows, :] = pr
            tab_im_ref[rows, :] = pi
            return _cmul(pr, pi, lr, li)

        jax.lax.fori_loop(0, SEG, fill, (lr, li))
        mr = tab_re_ref[PB - SUBLANES:PB, :]
        mi = tab_im_ref[PB - SUBLANES:PB, :]
        row = _row_ids(grp)
        pr, pi = mr, mi
        for r in range(SUBLANES):
            pow_re_ref[r:r + 1, :] = pr[0:1, :]
            pow_im_ref[r:r + 1, :] = pi[0:1, :]
            if r + 1 in (1, 2, 4):
                k = (1, 2, 4).index(r + 1)
                step_re_ref[k] = jnp.where(row >= r + 1, pr, 0.0)
                step_im_ref[k] = jnp.where(row >= r + 1, pi, 0.0)
            pr, pi = _cmul(pr, pi, mr, mi)

    scan_refs = (lre_ref, lim_ref, tab_re_ref, tab_im_ref, pow_re_ref, pow_im_ref,
                 step_re_ref, step_im_ref, cr_ref, ci_ref)

    def step(xr_cur, xi_cur, h_cur, xr_nxt, xi_nxt, h_prv):
        _s5_project(un_ref, bmat_ref, xr_nxt, xi_nxt)
        _s5_scan(xr_cur, xi_cur, h_cur, *scan_refs)
        _s5_output(h_prv, up_ref, cmat_ref, d_ref, o_ref)

    @pl.when(t % 2 == 0)
    def _():
        step(x0r_ref, x0i_ref, h0_ref, x1r_ref, x1i_ref, h1_ref)

    @pl.when(t % 2 == 1)
    def _():
        step(x1r_ref, x1i_ref, h1_ref, x0r_ref, x0i_ref, h0_ref)


def _s5(z, bmat, cmat, lam_re, lam_im, dvec, bsz, seq):
    n_t = seq // PB
    n_tiles = D_SSM // S5_TILE
    ns = S5_TILE_STATES
    vec = lambda b, c, t: (0, c)
    blk = lambda f: pl.BlockSpec((PB, S5_TILE), lambda b, c, t: (b * n_t + f(t), c))
    return pl.pallas_call(
        _s5_kernel,
        grid=(bsz, n_tiles, n_t + 1),
        in_specs=[blk(lambda t: 0),
                  blk(lambda t: jnp.minimum(t + 1, n_t - 1)),
                  blk(lambda t: jnp.maximum(t - 1, 0)),
                  pl.BlockSpec((1, S5_TILE, 2 * ns), lambda b, c, t: (c, 0, 0)),
                  pl.BlockSpec((1, 2 * ns, S5_TILE), lambda b, c, t: (c, 0, 0)),
                  pl.BlockSpec((1, ns), vec),
                  pl.BlockSpec((1, ns), vec),
                  pl.BlockSpec((1, S5_TILE), vec)],
        out_specs=blk(lambda t: jnp.maximum(t - 1, 0)),
        out_shape=jax.ShapeDtypeStruct((bsz * seq, D_SSM), BF16),
        scratch_shapes=[pltpu.VMEM((PB, ns), F32),
                        pltpu.VMEM((PB, ns), F32),
                        pltpu.VMEM((PB, ns), F32),
                        pltpu.VMEM((PB, ns), F32),
                        pltpu.VMEM((PB, 2 * ns), BF16),
                        pltpu.VMEM((PB, 2 * ns), BF16),
                        pltpu.VMEM((PB, ns), F32),
                        pltpu.VMEM((PB, ns), F32),
                        pltpu.VMEM((SUBLANES, ns), F32),
                        pltpu.VMEM((SUBLANES, ns), F32),
                        pltpu.VMEM((3, SUBLANES, ns), F32),
                        pltpu.VMEM((3, SUBLANES, ns), F32),
                        pltpu.VMEM((1, ns), F32),
                        pltpu.VMEM((1, ns), F32)],
        compiler_params=_params("parallel", "parallel", "arbitrary"),
        name="s5",
    )(z, z, z, bmat, cmat, lam_re, lam_im, dvec)


def _mix_kernel(hg_ref, y_ref, ga_ref, gb_ref, wa_ref, gw_ref, gv_ref, o_ref,
                wab_ref, gwb_ref, gvb_ref):
    @pl.when(pl.program_id(1) == 0)
    def _():
        wab_ref[...] = wa_ref[...].astype(BF16)
        gwb_ref[...] = gw_ref[...].astype(BF16)
        gvb_ref[...] = gv_ref[...].astype(BF16)

    y_a = jnp.dot(hg_ref[...], wab_ref[...], preferred_element_type=F32)
    y = y_ref[...]
    y_b = (jnp.dot(y, gwb_ref[...], preferred_element_type=F32)
           * jax.nn.sigmoid(jnp.dot(y, gvb_ref[...], preferred_element_type=F32)))
    mix = (jax.nn.sigmoid(ga_ref[...].astype(F32)) * y_a
           + jax.nn.sigmoid(gb_ref[...].astype(F32)) * y_b)
    o_ref[...] = mix.astype(o_ref.dtype)


def _mix(hg, y, z, w_a, glu_w, glu_v, tm, tn):
    m = hg.shape[0]
    return pl.pallas_call(
        _mix_kernel,
        grid=(D_MODEL // tn, m // tm),
        in_specs=[pl.BlockSpec((tm, D_RNN), lambda j, i: (i, 0)),
                  pl.BlockSpec((tm, D_SSM), lambda j, i: (i, 0)),
                  pl.BlockSpec((tm, tn), lambda j, i: (i, Z_COL_GA // tn + j)),
                  pl.BlockSpec((tm, tn), lambda j, i: (i, Z_COL_GB // tn + j)),
                  pl.BlockSpec((D_RNN, tn), lambda j, i: (0, j)),
                  pl.BlockSpec((D_SSM, tn), lambda j, i: (0, j)),
                  pl.BlockSpec((D_SSM, tn), lambda j, i: (0, j))],
        out_specs=pl.BlockSpec((tm, tn), lambda j, i: (i, j)),
        out_shape=jax.ShapeDtypeStruct((m, D_MODEL), BF16),
        scratch_shapes=[pltpu.VMEM((D_RNN, tn), BF16),
                        pltpu.VMEM((D_SSM, tn), BF16),
                        pltpu.VMEM((D_SSM, tn), BF16)],
        compiler_params=_params("parallel", "arbitrary"),
        name="mix",
    )(hg, y, z, z, w_a, glu_w, glu_v)


def _layernorm(v, g, b):
    mu = jnp.mean(v, axis=-1, keepdims=True)
    c = v - mu
    var = jnp.mean(c * c, axis=-1, keepdims=True)
    return c * jax.lax.rsqrt(var + LN_EPS) * g + b


def _outproj_ln_kernel(mix_ref, pt_ref, x_ref, w_ref, g_ref, b_ref, o_ref, ob_ref, *, alpha):
    mix = jnp.dot(pt_ref[...], mix_ref[...], preferred_element_type=F32).astype(BF16)
    v = alpha * x_ref[...] + jnp.dot(mix, w_ref[...], preferred_element_type=F32)
    out = _layernorm(v, g_ref[...], b_ref[...])
    o_ref[...] = out
    ob_ref[...] = out.astype(BF16)


def _outproj_ln(mix, perm_t, x, w_out, g, b, alpha):
    m = mix.shape[0]
    row = lambda i: (i, 0)
    fixed = lambda i: (0, 0)
    return pl.pallas_call(
        functools.partial(_outproj_ln_kernel, alpha=alpha),
        grid=(m // PB,),
        in_specs=[pl.BlockSpec((PB, D_MODEL), row),
                  pl.BlockSpec((PB, PB), fixed),
                  pl.BlockSpec((PB, D_MODEL), row),
                  pl.BlockSpec((D_MODEL, D_MODEL), fixed),
                  pl.BlockSpec((1, D_MODEL), fixed),
                  pl.BlockSpec((1, D_MODEL), fixed)],
        out_specs=(pl.BlockSpec((PB, D_MODEL), row), pl.BlockSpec((PB, D_MODEL), row)),
        out_shape=(jax.ShapeDtypeStruct((m, D_MODEL), F32),
                   jax.ShapeDtypeStruct((m, D_MODEL), BF16)),
        compiler_params=_params("parallel"),
        name="outproj_ln",
    )(mix, perm_t, x, w_out, g, b)


def _mlp_up_kernel(x_ref, w_ref, b_ref, o_ref, wb_ref):
    @pl.when(pl.program_id(1) == 0)
    def _():
        wb_ref[...] = w_ref[...].astype(BF16)

    v = jnp.dot(x_ref[...], wb_ref[...], preferred_element_type=F32) + b_ref[...]
    v = jnp.maximum(v, 0.0)
    o_ref[...] = (v * v).astype(o_ref.dtype)


def _mlp_up(xb, w_up, b_up, tm, tn):
    m = xb.shape[0]
    return pl.pallas_call(
        _mlp_up_kernel,
        grid=(D_FF // tn, m // tm),
        in_specs=[pl.BlockSpec((tm, D_MODEL), lambda j, i: (i, 0)),
                  pl.BlockSpec((D_MODEL, tn), lambda j, i: (0, j)),
                  pl.BlockSpec((1, tn), lambda j, i: (0, j))],
        out_specs=pl.BlockSpec((tm, tn), lambda j, i: (i, j)),
        out_shape=jax.ShapeDtypeStruct((m, D_FF), BF16),
        scratch_shapes=[pltpu.VMEM((D_MODEL, tn), BF16)],
        compiler_params=_params("parallel", "arbitrary"),
        name="mlp_up",
    )(xb, w_up, b_up)


def _mlp_down_ln_kernel(a_ref, w_ref, x_ref, bd_ref, g_ref, b_ref, o_ref, *, alpha):
    k = pl.program_id(1)
    last = pl.num_programs(1) - 1

    def part():
        return jnp.dot(a_ref[...], w_ref[...], preferred_element_type=F32)

    @pl.when(k == 0)
    def _():
        o_ref[...] = part()

    @pl.when(jnp.logical_and(k > 0, k < last))
    def _():
        o_ref[...] += part()

    @pl.when(k == last)
    def _():
        v = alpha * x_ref[...] + (o_ref[...] + part()) + bd_ref[...]
        o_ref[...] = _layernorm(v, g_ref[...], b_ref[...])


def _mlp_down_ln(a, w_down, x1, b_down, g, b, alpha, tm, tk):
    m = a.shape[0]
    fixed = lambda i, k: (0, 0)
    return pl.pallas_call(
        functools.partial(_mlp_down_ln_kernel, alpha=alpha),
        grid=(m // tm, D_FF // tk),
        in_specs=[pl.BlockSpec((tm, tk), lambda i, k: (i, k)),
                  pl.BlockSpec((tk, D_MODEL), lambda i, k: (k, 0)),
                  pl.BlockSpec((tm, D_MODEL), lambda i, k: (i, 0)),
                  pl.BlockSpec((1, D_MODEL), fixed),
                  pl.BlockSpec((1, D_MODEL), fixed),
                  pl.BlockSpec((1, D_MODEL), fixed)],
        out_specs=pl.BlockSpec((tm, D_MODEL), lambda i, k: (i, 0)),
        out_shape=jax.ShapeDtypeStruct((m, D_MODEL), F32),
        compiler_params=_params("parallel", "arbitrary"),
        name="mlp_down_ln",
    )(a, w_down, x1, b_down, g, b)


def _s5_block_matrices(fr, fi, b_re, b_im, c_re, c_im):
    bbr = fr[..., None] * b_re - fi[..., None] * b_im
    bbi = fr[..., None] * b_im + fi[..., None] * b_re
    n_tiles = D_SSM // S5_TILE
    eye = jnp.eye(S5_TILE_GROUPS, dtype=F32)

    def in_blocks(w):
        w = w.reshape(n_tiles, S5_TILE_GROUPS, SSM_STATE, SSM_GROUP)
        m = jnp.einsum("tgph,gk->tghkp", w, eye)
        return m.reshape(n_tiles, S5_TILE, S5_TILE_STATES)

    def out_blocks(w):
        w = w.reshape(n_tiles, S5_TILE_GROUPS, SSM_GROUP, SSM_STATE)
        m = jnp.einsum("tghp,gk->tgpkh", w, eye)
        return m.reshape(n_tiles, S5_TILE_STATES, S5_TILE)

    bmat = jnp.concatenate([in_blocks(bbr), in_blocks(bbi)], axis=2).astype(BF16)
    cmat = jnp.concatenate([out_blocks(c_re), out_blocks(-c_im)], axis=1).astype(BF16)
    return bmat, cmat


def kernel(x, w_in, conv_w, conv_b, rg_wa, rg_ba, rg_wx, rg_bx, rg_lambda, w_a_out, ssm_a_re, ssm_a_im, ssm_log_dt, ssm_b_re, ssm_b_im, ssm_c_re, ssm_c_im, ssm_d, glu_w, glu_v, w_out, ln1_g, ln1_b, mlp_w_up, mlp_b_up, mlp_w_down, mlp_b_down, ln2_g, ln2_b):
    bsz, seq, _ = x.shape
    assert seq % PB == 0
    m = bsz * seq
    depth = w_in.shape[0]
    alpha = (2.0 * depth) ** 0.25
    perm = _interleave_matrix()
    perm_fwd = jnp.asarray(perm, BF16)
    perm_bwd = jnp.asarray(perm.T, BF16)
    for l in range(depth):
        xf = x.reshape(m, D_MODEL)
        xp = _interleave(xf, perm_fwd)
        z = _in_proj(xp, w_in[l], tm=1024, tn=1024)

        w_gate = jnp.concatenate([rg_wa[l], rg_wx[l]], axis=-1).astype(BF16)
        hg = _rglru(xp, w_in[l], conv_w[l], conv_b[l].reshape(1, D_RNN), w_gate,
                    rg_ba[l].reshape(1, D_RNN), rg_bx[l].reshape(1, D_RNN),
                    rg_lambda[l].reshape(1, D_RNN), bsz, seq)

        lbr, lbi, fr, fi = _s5_discretize(ssm_a_re[l], ssm_a_im[l], ssm_log_dt[l])
        bmat, cmat = _s5_block_matrices(fr, fi, ssm_b_re[l], ssm_b_im[l],
                                        ssm_c_re[l], ssm_c_im[l])
        y = _s5(z, bmat, cmat, lbr.reshape(1, -1), lbi.reshape(1, -1),
                ssm_d[l].reshape(1, D_SSM), bsz, seq)

        mix = _mix(hg, y, z, w_a_out[l], glu_w[l], glu_v[l], tm=1024, tn=512)
        x1, x1b = _outproj_ln(mix, perm_bwd, xf, w_out[l].astype(BF16),
                              ln1_g[l].reshape(1, D_MODEL), ln1_b[l].reshape(1, D_MODEL),
                              alpha)
        a = _mlp_up(x1b, mlp_w_up[l], mlp_b_up[l].reshape(1, D_FF), tm=1024, tn=1024)
        x2 = _mlp_down_ln(a, mlp_w_down[l].astype(BF16), x1,
                          mlp_b_down[l].reshape(1, D_MODEL),
                          ln2_g[l].reshape(1, D_MODEL), ln2_b[l].reshape(1, D_MODEL),
                          alpha, tm=1024, tk=1024)
        x = x2.reshape(bsz, seq, D_MODEL)
    return x
```

```python
import functools

import numpy as np
import jax
import jax.numpy as jnp
from jax.experimental import pallas as pl
from jax.experimental.pallas import tpu as pltpu

F32 = jnp.float32
BF16 = jnp.bfloat16

D_MODEL = 2048
D_RNN = D_MODEL
RG_HEADS = 16
RG_HEAD_DIM = D_RNN // RG_HEADS
CONV_WIDTH = 4
RG_C = 8.0
D_SSM = D_MODEL // 2
SSM_GROUP = 16
SSM_GROUPS = D_SSM // SSM_GROUP
SSM_STATE = 64
D_FF = 4 * D_MODEL
D_IN = 2 * D_RNN + D_SSM + 2 * D_MODEL
LN_EPS = 1e-5

SUBLANES = 8
BF16_ROWS = 2 * SUBLANES
VMEM_LIMIT = 56 * 1024 * 1024

NSEG = SUBLANES
PB = 512
SEG = PB // NSEG
HALO = (CONV_WIDTH - 1) * SUBLANES
ROW_CHUNK = 256

RG_TILE = 512
S5_TILE = 256
S5_TILE_GROUPS = S5_TILE // SSM_GROUP
S5_TILE_STATES = S5_TILE_GROUPS * SSM_STATE
S5_SCAN_LANES = 256
W_COL_GATE = D_RNN
W_COL_REST = 2 * D_RNN
Z_COLS = D_SSM + 2 * D_MODEL
Z_COL_GA = D_SSM
Z_COL_GB = D_SSM + D_MODEL


def _params(*sem):
    return pltpu.CompilerParams(dimension_semantics=sem, vmem_limit_bytes=VMEM_LIMIT)


def _interleave_matrix():
    p = np.arange(PB)
    src = (p % NSEG) * SEG + p // NSEG
    mat = np.zeros((PB, PB), np.float32)
    mat[p, src] = 1.0
    return mat


def _row_ids(shape):
    return jax.lax.broadcasted_iota(jnp.int32, shape, 0)


def _shift_rows(v, d, fill):
    return jnp.where(_row_ids(v.shape) >= d, pltpu.roll(v, d, 0), fill)


def _interleave_kernel(x_ref, p_ref, o_ref):
    o_ref[...] = jnp.dot(p_ref[...], x_ref[...].astype(BF16),
                         preferred_element_type=F32).astype(BF16)


def _interleave(x, perm):
    m, k = x.shape
    return pl.pallas_call(
        _interleave_kernel,
        grid=(m // PB,),
        in_specs=[pl.BlockSpec((PB, k), lambda i: (i, 0)),
                  pl.BlockSpec((PB, PB), lambda i: (0, 0))],
        out_specs=pl.BlockSpec((PB, k), lambda i: (i, 0)),
        out_shape=jax.ShapeDtypeStruct((m, k), BF16),
        compiler_params=_params("parallel"),
        name="interleave",
    )(x, perm)


def _in_proj_kernel(x_ref, w_ref, o_ref, wb_ref):
    @pl.when(pl.program_id(1) == 0)
    def _():
        wb_ref[...] = w_ref[...].astype(BF16)

    o_ref[...] = jnp.dot(x_ref[...], wb_ref[...],
                         preferred_element_type=F32).astype(o_ref.dtype)


def _in_proj(xp, w, tm, tn):
    m, k = xp.shape
    col0 = W_COL_REST // tn
    return pl.pallas_call(
        _in_proj_kernel,
        grid=(Z_COLS // tn, m // tm),
        in_specs=[pl.BlockSpec((tm, k), lambda j, i: (i, 0)),
                  pl.BlockSpec((k, tn), lambda j, i: (0, col0 + j))],
        out_specs=pl.BlockSpec((tm, tn), lambda j, i: (i, j)),
        out_shape=jax.ShapeDtypeStruct((m, Z_COLS), BF16),
        scratch_shapes=[pltpu.VMEM((k, tn), BF16)],
        compiler_params=_params("parallel", "arbitrary"),
        name="in_proj",
    )(xp, w)


def _rglru_gates(hh, z_ref, cw_ref, cb_ref, w_ref, ext_ref, xc_ref, pre_ref, gg_ref):
    sl = slice(hh * RG_HEAD_DIM, (hh + 1) * RG_HEAD_DIM)
    gg_ref[:, sl] = jax.nn.gelu(
        z_ref[:, RG_TILE + hh * RG_HEAD_DIM:RG_TILE + (hh + 1) * RG_HEAD_DIM])
    grp = (SUBLANES, RG_HEAD_DIM)
    x = z_ref[:, sl]
    first = _row_ids(grp) == 0
    halos = []
    for k in range(CONV_WIDTH - 1):
        prev_g = ext_ref[PB + k * SUBLANES:PB + (k + 1) * SUBLANES, sl]
        cur_g = x[PB - HALO + k * SUBLANES:PB - HALO + (k + 1) * SUBLANES, :]
        halos.append(jnp.where(first, pltpu.roll(prev_g, 1, 0), pltpu.roll(cur_g, 1, 0)))
    for k in range(CONV_WIDTH - 1):
        ext_ref[k * SUBLANES:(k + 1) * SUBLANES, sl] = halos[k]
    ext_ref[HALO:HALO + PB, sl] = x

    xc = cb_ref[:, sl] + cw_ref[CONV_WIDTH - 1:CONV_WIDTH, sl] * x
    for k in range(CONV_WIDTH - 1):
        xc = xc + cw_ref[k:k + 1, sl] * ext_ref[k * SUBLANES:k * SUBLANES + PB, sl]

    xc_ref[:, sl] = xc
    pre_ref[:, 2 * hh * RG_HEAD_DIM:2 * (hh + 1) * RG_HEAD_DIM] = jnp.dot(
        xc.astype(BF16), w_ref[hh], preferred_element_type=F32)


def _rglru_head(hh, o_ref, row0, cvec, ba_ref, bx_ref, xc_ref, pre_ref, gg_ref,
                a_ref, b_ref, carry_ref):
    sl = slice(hh * RG_HEAD_DIM, (hh + 1) * RG_HEAD_DIM)
    grp = (SUBLANES, RG_HEAD_DIM)
    first = _row_ids(grp) == 0
    xc = xc_ref[:, sl]
    r = jax.nn.sigmoid(pre_ref[:, 2 * hh * RG_HEAD_DIM:(2 * hh + 1) * RG_HEAD_DIM] + ba_ref[:, sl])
    i = jax.nn.sigmoid(pre_ref[:, (2 * hh + 1) * RG_HEAD_DIM:(2 * hh + 2) * RG_HEAD_DIM]
                       + bx_ref[:, sl])
    log_a = cvec[:, sl] * r
    a = jnp.exp(log_a)
    a_ref[:, sl] = a
    one_minus_a2 = -jnp.tanh(log_a) * (1.0 + a * a)
    b_ref[:, sl] = jnp.sqrt(one_minus_a2) * (i * xc)

    h_end, a_end = b_ref[0:SUBLANES, sl], a_ref[0:SUBLANES, sl]
    for tau in range(1, SEG):
        rows = slice(tau * SUBLANES, (tau + 1) * SUBLANES)
        a = a_ref[rows, sl]
        h_end = a * h_end + b_ref[rows, sl]
        a_end = a * a_end
        b_ref[rows, sl] = h_end
        a_ref[rows, sl] = a_end

    for d in (1, 2, 4):
        h_end = a_end * _shift_rows(h_end, d, 0.0) + h_end
        a_end = a_end * _shift_rows(a_end, d, 1.0)
    carry = carry_ref[:, sl]
    h_true = h_end + a_end * carry
    h_in = jnp.where(first, carry, pltpu.roll(h_true, 1, 0))
    carry_ref[:, sl] = h_true[SUBLANES - 1:SUBLANES, :]

    h_in2 = jnp.concatenate([h_in, h_in], axis=0)
    for j in range(PB // BF16_ROWS):
        rows = slice(j * BF16_ROWS, (j + 1) * BF16_ROWS)
        h = b_ref[rows, sl] + a_ref[rows, sl] * h_in2
        o_ref[row0 + j * BF16_ROWS:row0 + (j + 1) * BF16_ROWS, sl] = (
            h * gg_ref[rows, sl]).astype(o_ref.dtype)


def _rglru_kernel(x0_ref, x1_ref, x2_ref, wx_ref, wg_ref, cw_ref, cb_ref, w_ref,
                  ba_ref, bx_ref, lam_ref, o_ref,
                  wb_ref, z_ref, ext_ref, xca_ref, prea_ref, gga_ref, a_ref, b_ref, carry_ref):
    t = pl.program_id(2)
    n_heads = RG_TILE // RG_HEAD_DIM
    bufs = (xca_ref, prea_ref, gga_ref)

    def project(x_ref):
        z_ref[...] = jnp.dot(x_ref[...], wb_ref[...], preferred_element_type=F32)

    def gates():
        for hh in range(n_heads):
            _rglru_gates(hh, z_ref, cw_ref, cb_ref, w_ref, ext_ref, *bufs)

    @pl.when(t == 0)
    def _():
        wb_ref[:, :RG_TILE] = wx_ref[...].astype(BF16)
        wb_ref[:, RG_TILE:] = wg_ref[...].astype(BF16)
        ext_ref[PB:PB + HALO, :] = jnp.zeros((HALO, RG_TILE), F32)
        carry_ref[...] = jnp.zeros_like(carry_ref)
        project(x0_ref)
        gates()

    neg = -lam_ref[...]
    softplus = jnp.maximum(neg, 0.0) + jnp.log1p(jnp.exp(-jnp.abs(neg)))
    cvec = -RG_C * softplus

    def back(row0):
        for hh in range(n_heads):
            _rglru_head(hh, o_ref, row0, cvec, ba_ref, bx_ref, *bufs, a_ref, b_ref, carry_ref)

    project(x1_ref)
    back(0)
    gates()
    project(x2_ref)
    back(PB)
    gates()


def _rglru(xp, w_in, conv_w, conv_b, w_gate, ba, bx, lam, bsz, seq):
    n_t = seq // PB
    assert n_t % 2 == 0
    n_c = D_RNN // RG_TILE
    vec = lambda b, c, t: (0, c)
    return pl.pallas_call(
        _rglru_kernel,
        grid=(bsz, n_c, n_t // 2),
        in_specs=[pl.BlockSpec((PB, D_MODEL), lambda b, c, t: (b * n_t, 0)),
                  pl.BlockSpec((PB, D_MODEL), lambda b, c, t: (b * n_t + 2 * t + 1, 0)),
                  pl.BlockSpec((PB, D_MODEL),
                               lambda b, c, t: (b * n_t + jnp.minimum(2 * t + 2, n_t - 1), 0)),
                  pl.BlockSpec((D_MODEL, RG_TILE), vec),
                  pl.BlockSpec((D_MODEL, RG_TILE), lambda b, c, t: (0, W_COL_GATE // RG_TILE + c)),
                  pl.BlockSpec((CONV_WIDTH, RG_TILE), vec),
                  pl.BlockSpec((1, RG_TILE), vec),
                  pl.BlockSpec((RG_TILE // RG_HEAD_DIM, RG_HEAD_DIM, 2 * RG_HEAD_DIM),
                               lambda b, c, t: (c, 0, 0)),
                  pl.BlockSpec((1, RG_TILE), vec),
                  pl.BlockSpec((1, RG_TILE), vec),
                  pl.BlockSpec((1, RG_TILE), vec)],
        out_specs=pl.BlockSpec((2 * PB, RG_TILE), lambda b, c, t: (b * (n_t // 2) + t, c)),
        out_shape=jax.ShapeDtypeStruct((bsz * seq, D_RNN), BF16),
        scratch_shapes=[pltpu.VMEM((D_MODEL, 2 * RG_TILE), BF16),
                        pltpu.VMEM((PB, 2 * RG_TILE), F32),
                        pltpu.VMEM((HALO + PB, RG_TILE), F32),
                        pltpu.VMEM((PB, RG_TILE), F32),
                        pltpu.VMEM((PB, 2 * RG_TILE), F32),
                        pltpu.VMEM((PB, RG_TILE), F32),
                        pltpu.VMEM((PB, RG_TILE), F32),
                        pltpu.VMEM((PB, RG_TILE), F32),
                        pltpu.VMEM((1, RG_TILE), F32)],
        compiler_params=_params("parallel", "parallel", "arbitrary"),
        name="rglru",
    )(xp, xp, xp, w_in, w_in, conv_w, conv_b, w_gate, ba, bx, lam)


def _s5_discretize_kernel(are_ref, aim_ref, ldt_ref, lbr_ref, lbi_ref, fr_ref, fi_ref):
    dt = jnp.exp(ldt_ref[...])
    lr = jnp.minimum(are_ref[...], -1e-4)
    li = aim_ref[...]
    mag = jnp.exp(lr * dt)
    lbr = mag * jnp.cos(li * dt)
    lbi = mag * jnp.sin(li * dt)
    zr, zi = lbr - 1.0, lbi
    den = lr * lr + li * li
    lbr_ref[...] = lbr
    lbi_ref[...] = lbi
    fr_ref[...] = (zr * lr + zi * li) / den
    fi_ref[...] = (zi * lr - zr * li) / den


def _s5_discretize(a_re, a_im, log_dt):
    shp = jax.ShapeDtypeStruct((SSM_GROUPS, SSM_STATE), F32)
    return pl.pallas_call(
        _s5_discretize_kernel,
        out_shape=(shp, shp, shp, shp),
        name="s5_discretize",
    )(a_re, a_im, log_dt.reshape(SSM_GROUPS, 1))


def _cmul(ar, ai, br, bi):
    return ar * br - ai * bi, ar * bi + ai * br


def _s5_project(u_ref, bmat_ref, xr_ref, xi_ref, cols):
    ns = S5_TILE_STATES
    u = u_ref[...]
    xr_ref[:, cols] = jnp.dot(u, bmat_ref[0, :, cols], preferred_element_type=F32)
    xi_ref[:, cols] = jnp.dot(u, bmat_ref[0, :, ns + cols.start:ns + cols.stop],
                              preferred_element_type=F32)


def _s5_scan(cols, xr_ref, xi_ref, hb_ref, lre_ref, lim_ref, tab_re_ref, tab_im_ref,
             pow_re_ref, pow_im_ref, step_re_ref, step_im_ref, cr_ref, ci_ref):
    ns = S5_TILE_STATES
    grp = (SUBLANES, S5_SCAN_LANES)
    first = _row_ids(grp) == 0
    lr = jnp.broadcast_to(lre_ref[:, cols], grp)
    li = jnp.broadcast_to(lim_ref[:, cols], grp)

    er, ei = xr_ref[0:SUBLANES, cols], xi_ref[0:SUBLANES, cols]
    for tau in range(1, SEG):
        rows = slice(tau * SUBLANES, (tau + 1) * SUBLANES)
        nr = (lr * er - li * ei) + xr_ref[rows, cols]
        ni = (lr * ei + li * er) + xi_ref[rows, cols]
        xr_ref[rows, cols] = nr
        xi_ref[rows, cols] = ni
        er, ei = nr, ni

    for k, d in enumerate((1, 2, 4)):
        sr, si = pltpu.roll(er, d, 0), pltpu.roll(ei, d, 0)
        mr, mi = _cmul(step_re_ref[k, :, cols], step_im_ref[k, :, cols], sr, si)
        er, ei = er + mr, ei + mi
    cr, ci = cr_ref[:, cols], ci_ref[:, cols]
    mr, mi = _cmul(pow_re_ref[:, cols], pow_im_ref[:, cols], cr, ci)
    er, ei = er + mr, ei + mi
    in_r = jnp.where(first, cr, pltpu.roll(er, 1, 0))
    in_i = jnp.where(first, ci, pltpu.roll(ei, 1, 0))
    cr_ref[:, cols] = er[SUBLANES - 1:SUBLANES, :]
    ci_ref[:, cols] = ei[SUBLANES - 1:SUBLANES, :]

    in_r2 = jnp.concatenate([in_r, in_r], axis=0)
    in_i2 = jnp.concatenate([in_i, in_i], axis=0)
    for i in range(PB // BF16_ROWS):
        rows = slice(i * BF16_ROWS, (i + 1) * BF16_ROWS)
        fr, fi = _cmul(tab_re_ref[rows, cols], tab_im_ref[rows, cols], in_r2, in_i2)
        hb_ref[rows, cols] = (xr_ref[rows, cols] + fr).astype(BF16)
        hb_ref[rows, ns + cols.start:ns + cols.stop] = (xi_ref[rows, cols] + fi).astype(BF16)


def _s5_kernel(u0_ref, uc_ref, un_ref, bmat_ref, cmat_ref, lre_ref, lim_ref, d_ref, o_ref,
               xr_ref, xi_ref, hb_ref, tab_re_ref, tab_im_ref,
               pow_re_ref, pow_im_ref, step_re_ref, step_im_ref, cr_ref, ci_ref):
    t = pl.program_id(2)
    ns = S5_TILE_STATES
    grp = (SUBLANES, ns)
    chunks = [slice(c0, c0 + S5_SCAN_LANES) for c0 in range(0, ns, S5_SCAN_LANES)]

    @pl.when(t == 0)
    def _():
        for cols in chunks:
            _s5_project(u0_ref, bmat_ref, xr_ref, xi_ref, cols)
        cr_ref[...] = jnp.zeros_like(cr_ref)
        ci_ref[...] = jnp.zeros_like(ci_ref)
        lr = jnp.broadcast_to(lre_ref[...], grp)
        li = jnp.broadcast_to(lim_ref[...], grp)

        def fill(tau, p):
            pr, pi = p
            rows = pl.ds(pl.multiple_of(tau * SUBLANES, SUBLANES), SUBLANES)
            tab_re_ref[rows, :] = pr
            tab_im_ref[rows, :] = pi
            return _cmul(pr, pi, lr, li)

        jax.lax.fori_loop(0, SEG, fill, (lr, li))
        mr = tab_re_ref[PB - SUBLANES:PB, :]
        mi = tab_im_ref[PB - SUBLANES:PB, :]
        row = _row_ids(grp)
        pr, pi = mr, mi
        for r in range(SUBLANES):
            pow_re_ref[r:r + 1, :] = pr[0:1, :]
            pow_im_ref[r:r + 1, :] = pi[0:1, :]
            if r + 1 in (1, 2, 4):
                k = (1, 2, 4).index(r + 1)
                step_re_ref[k] = jnp.where(row >= r + 1, pr, 0.0)
                step_im_ref[k] = jnp.where(row >= r + 1, pi, 0.0)
            pr, pi = _cmul(pr, pi, mr, mi)

    scan_refs = (lre_ref, lim_ref, tab_re_ref, tab_im_ref, pow_re_ref, pow_im_ref,
                 step_re_ref, step_im_ref, cr_ref, ci_ref)
    y = d_ref[...] * uc_ref[...].astype(F32)
    for cols in chunks:
        _s5_scan(cols, xr_ref, xi_ref, hb_ref, *scan_refs)
        y = y + jnp.dot(hb_ref[:, cols], cmat_ref[0, cols, :], preferred_element_type=F32)
        y = y + jnp.dot(hb_ref[:, ns + cols.start:ns + cols.stop],
                        cmat_ref[0, ns + cols.start:ns + cols.stop, :],
                        preferred_element_type=F32)
        _s5_project(un_ref, bmat_ref, xr_ref, xi_ref, cols)
    o_ref[...] = jax.nn.gelu(y).astype(o_ref.dtype)


def _s5(z, bmat, cmat, lam_re, lam_im, dvec, bsz, seq):
    n_t = seq // PB
    n_tiles = D_SSM // S5_TILE
    ns = S5_TILE_STATES
    vec = lambda b, c, t: (0, c)
    blk = lambda f: pl.BlockSpec((PB, S5_TILE), lambda b, c, t: (b * n_t + f(t), c))
    return pl.pallas_call(
        _s5_kernel,
        grid=(bsz, n_tiles, n_t),
        in_specs=[blk(lambda t: 0),
                  blk(lambda t: t),
                  blk(lambda t: jnp.minimum(t + 1, n_t - 1)),
                  pl.BlockSpec((1, S5_TILE, 2 * ns), lambda b, c, t: (c, 0, 0)),
                  pl.BlockSpec((1, 2 * ns, S5_TILE), lambda b, c, t: (c, 0, 0)),
                  pl.BlockSpec((1, ns), vec),
                  pl.BlockSpec((1, ns), vec),
                  pl.BlockSpec((1, S5_TILE), vec)],
        out_specs=blk(lambda t: t),
        out_shape=jax.ShapeDtypeStruct((bsz * seq, D_SSM), BF16),
        scratch_shapes=[pltpu.VMEM((PB, ns), F32),
                        pltpu.VMEM((PB, ns), F32),
                        pltpu.VMEM((PB, 2 * ns), BF16),
                        pltpu.VMEM((PB, ns), F32),
                        pltpu.VMEM((PB, ns), F32),
                        pltpu.VMEM((SUBLANES, ns), F32),
                        pltpu.VMEM((SUBLANES, ns), F32),
                        pltpu.VMEM((3, SUBLANES, ns), F32),
                        pltpu.VMEM((3, SUBLANES, ns), F32),
                        pltpu.VMEM((1, ns), F32),
                        pltpu.VMEM((1, ns), F32)],
        compiler_params=_params("parallel", "parallel", "arbitrary"),
        name="s5",
    )(z, z, z, bmat, cmat, lam_re, lam_im, dvec)


def _mix_kernel(hg_ref, y_ref, ga_ref, gb_ref, wa_ref, gw_ref, gv_ref, o_ref,
                wab_ref, gwb_ref, gvb_ref):
    @pl.when(pl.program_id(1) == 0)
    def _():
        wab_ref[...] = wa_ref[...].astype(BF16)
        gwb_ref[...] = gw_ref[...].astype(BF16)
        gvb_ref[...] = gv_ref[...].astype(BF16)

    for c in range(o_ref.shape[0] // ROW_CHUNK):
        rows = slice(c * ROW_CHUNK, (c + 1) * ROW_CHUNK)
        y_a = jnp.dot(hg_ref[rows, :], wab_ref[...], preferred_element_type=F32)
        y = y_ref[rows, :]
        y_b = (jnp.dot(y, gwb_ref[...], preferred_element_type=F32)
               * jax.nn.sigmoid(jnp.dot(y, gvb_ref[...], preferred_element_type=F32)))
        mix = (jax.nn.sigmoid(ga_ref[rows, :].astype(F32)) * y_a
               + jax.nn.sigmoid(gb_ref[rows, :].astype(F32)) * y_b)
        o_ref[rows, :] = mix.astype(o_ref.dtype)


def _mix(hg, y, z, w_a, glu_w, glu_v, tm, tn):
    m = hg.shape[0]
    return pl.pallas_call(
        _mix_kernel,
        grid=(D_MODEL // tn, m // tm),
        in_specs=[pl.BlockSpec((tm, D_RNN), lambda j, i: (i, 0)),
                  pl.BlockSpec((tm, D_SSM), lambda j, i: (i, 0)),
                  pl.BlockSpec((tm, tn), lambda j, i: (i, Z_COL_GA // tn + j)),
                  pl.BlockSpec((tm, tn), lambda j, i: (i, Z_COL_GB // tn + j)),
                  pl.BlockSpec((D_RNN, tn), lambda j, i: (0, j)),
                  pl.BlockSpec((D_SSM, tn), lambda j, i: (0, j)),
                  pl.BlockSpec((D_SSM, tn), lambda j, i: (0, j))],
        out_specs=pl.BlockSpec((tm, tn), lambda j, i: (i, j)),
        out_shape=jax.ShapeDtypeStruct((m, D_MODEL), BF16),
        scratch_shapes=[pltpu.VMEM((D_RNN, tn), BF16),
                        pltpu.VMEM((D_SSM, tn), BF16),
                        pltpu.VMEM((D_SSM, tn), BF16)],
        compiler_params=_params("parallel", "arbitrary"),
        name="mix",
    )(hg, y, z, z, w_a, glu_w, glu_v)


def _layernorm(v, g, b):
    mu = jnp.mean(v, axis=-1, keepdims=True)
    c = v - mu
    var = jnp.mean(c * c, axis=-1, keepdims=True)
    return c * jax.lax.rsqrt(var + LN_EPS) * g + b


def _outproj_ln_kernel(mix_ref, pt_ref, x_ref, w_ref, g_ref, b_ref, o_ref, ob_ref, *, alpha):
    for c in range(PB // ROW_CHUNK):
        rows = slice(c * ROW_CHUNK, (c + 1) * ROW_CHUNK)
        mix = jnp.dot(pt_ref[rows, :], mix_ref[...], preferred_element_type=F32).astype(BF16)
        v = alpha * x_ref[rows, :] + jnp.dot(mix, w_ref[...], preferred_element_type=F32)
        out = _layernorm(v, g_ref[...], b_ref[...])
        o_ref[rows, :] = out
        ob_ref[rows, :] = out.astype(BF16)


def _outproj_ln(mix, perm_t, x, w_out, g, b, alpha):
    m = mix.shape[0]
    row = lambda i: (i, 0)
    fixed = lambda i: (0, 0)
    return pl.pallas_call(
        functools.partial(_outproj_ln_kernel, alpha=alpha),
        grid=(m // PB,),
        in_specs=[pl.BlockSpec((PB, D_MODEL), row),
                  pl.BlockSpec((PB, PB), fixed),
                  pl.BlockSpec((PB, D_MODEL), row),
                  pl.BlockSpec((D_MODEL, D_MODEL), fixed),
                  pl.BlockSpec((1, D_MODEL), fixed),
                  pl.BlockSpec((1, D_MODEL), fixed)],
        out_specs=(pl.BlockSpec((PB, D_MODEL), row), pl.BlockSpec((PB, D_MODEL), row)),
        out_shape=(jax.ShapeDtypeStruct((m, D_MODEL), F32),
                   jax.ShapeDtypeStruct((m, D_MODEL), BF16)),
        compiler_params=_params("parallel"),
        name="outproj_ln",
    )(mix, perm_t, x, w_out, g, b)


def _mlp_up_kernel(x_ref, w_ref, b_ref, o_ref, wb_ref):
    @pl.when(pl.program_id(1) == 0)
    def _():
        wb_ref[...] = w_ref[...].astype(BF16)

    v = jnp.dot(x_ref[...], wb_ref[...], preferred_element_type=F32) + b_ref[...]
    v = jnp.maximum(v, 0.0)
    o_ref[...] = (v * v).astype(o_ref.dtype)


def _mlp_up(xb, w_up, b_up, tm, tn):
    m = xb.shape[0]
    return pl.pallas_call(
        _mlp_up_kernel,
        grid=(D_FF // tn, m // tm),
        in_specs=[pl.BlockSpec((tm, D_MODEL), lambda j, i: (i, 0)),
                  pl.BlockSpec((D_MODEL, tn), lambda j, i: (0, j)),
                  pl.BlockSpec((1, tn), lambda j, i: (0, j))],
        out_specs=pl.BlockSpec((tm, tn), lambda j, i: (i, j)),
        out_shape=jax.ShapeDtypeStruct((m, D_FF), BF16),
        scratch_shapes=[pltpu.VMEM((D_MODEL, tn), BF16)],
        compiler_params=_params("parallel", "arbitrary"),
        name="mlp_up",
    )(xb, w_up, b_up)


def _mlp_down_ln_kernel(a_ref, w_ref, x_ref, bd_ref, g_ref, b_ref, o_ref, *, alpha):
    k = pl.program_id(1)
    last = pl.num_programs(1) - 1

    def part():
        return jnp.dot(a_ref[...], w_ref[...], preferred_element_type=F32)

    @pl.when(k == 0)
    def _():
        o_ref[...] = part()

    @pl.when(jnp.logical_and(k > 0, k < last))
    def _():
        o_ref[...] += part()

    @pl.when(k == last)
    def _():
        for c in range(o_ref.shape[0] // ROW_CHUNK):
            rows = slice(c * ROW_CHUNK, (c + 1) * ROW_CHUNK)
            acc = o_ref[rows, :] + jnp.dot(a_ref[rows, :], w_ref[...],
                                           preferred_element_type=F32)
            v = alpha * x_ref[rows, :] + acc + bd_ref[...]
            o_ref[rows, :] = _layernorm(v, g_ref[...], b_ref[...])


def _mlp_down_ln(a, w_down, x1, b_down, g, b, alpha, tm, tk):
    m = a.shape[0]
    fixed = lambda i, k: (0, 0)
    return pl.pallas_call(
        functools.partial(_mlp_down_ln_kernel, alpha=alpha),
        grid=(m // tm, D_FF // tk),
        in_specs=[pl.BlockSpec((tm, tk), lambda i, k: (i, k)),
                  pl.BlockSpec((tk, D_MODEL), lambda i, k: (k, 0)),
                  pl.BlockSpec((tm, D_MODEL), lambda i, k: (i, 0)),
                  pl.BlockSpec((1, D_MODEL), fixed),
                  pl.BlockSpec((1, D_MODEL), fixed),
                  pl.BlockSpec((1, D_MODEL), fixed)],
        out_specs=pl.BlockSpec((tm, D_MODEL), lambda i, k: (i, 0)),
        out_shape=jax.ShapeDtypeStruct((m, D_MODEL), F32),
        compiler_params=_params("parallel", "arbitrary"),
        name="mlp_down_ln",
    )(a, w_down, x1, b_down, g, b)


def _s5_block_matrices(fr, fi, b_re, b_im, c_re, c_im):
    bbr = fr[..., None] * b_re - fi[..., None] * b_im
    bbi = fr[..., None] * b_im + fi[..., None] * b_re
    n_tiles = D_SSM // S5_TILE
    eye = jnp.eye(S5_TILE_GROUPS, dtype=F32)

    def in_blocks(w):
        w = w.reshape(n_tiles, S5_TILE_GROUPS, SSM_STATE, SSM_GROUP)
        m = jnp.einsum("tgph,gk->tghkp", w, eye)
        return m.reshape(n_tiles, S5_TILE, S5_TILE_STATES)

    def out_blocks(w):
        w = w.reshape(n_tiles, S5_TILE_GROUPS, SSM_GROUP, SSM_STATE)
        m = jnp.einsum("tghp,gk->tgpkh", w, eye)
        return m.reshape(n_tiles, S5_TILE_STATES, S5_TILE)

    bmat = jnp.concatenate([in_blocks(bbr), in_blocks(bbi)], axis=2).astype(BF16)
    cmat = jnp.concatenate([out_blocks(c_re), out_blocks(-c_im)], axis=1).astype(BF16)
    return bmat, cmat


def kernel(x, w_in, conv_w, conv_b, rg_wa, rg_ba, rg_wx, rg_bx, rg_lambda, w_a_out, ssm_a_re, ssm_a_im, ssm_log_dt, ssm_b_re, ssm_b_im, ssm_c_re, ssm_c_im, ssm_d, glu_w, glu_v, w_out, ln1_g, ln1_b, mlp_w_up, mlp_b_up, mlp_w_down, mlp_b_down, ln2_g, ln2_b):
    bsz, seq, _ = x.shape
    assert seq % PB == 0
    m = bsz * seq
    depth = w_in.shape[0]
    alpha = (2.0 * depth) ** 0.25
    perm = _interleave_matrix()
    perm_fwd = jnp.asarray(perm, BF16)
    perm_bwd = jnp.asarray(perm.T, BF16)
    for l in range(depth):
        xf = x.reshape(m, D_MODEL)
        xp = _interleave(xf, perm_fwd)
        z = _in_proj(xp, w_in[l], tm=1024, tn=1024)

        w_gate = jnp.concatenate([rg_wa[l], rg_wx[l]], axis=-1).astype(BF16)
        hg = _rglru(xp, w_in[l], conv_w[l], conv_b[l].reshape(1, D_RNN), w_gate,
                    rg_ba[l].reshape(1, D_RNN), rg_bx[l].reshape(1, D_RNN),
                    rg_lambda[l].reshape(1, D_RNN), bsz, seq)

        lbr, lbi, fr, fi = _s5_discretize(ssm_a_re[l], ssm_a_im[l], ssm_log_dt[l])
        bmat, cmat = _s5_block_matrices(fr, fi, ssm_b_re[l], ssm_b_im[l],
                                        ssm_c_re[l], ssm_c_im[l])
        y = _s5(z, bmat, cmat, lbr.reshape(1, -1), lbi.reshape(1, -1),
                ssm_d[l].reshape(1, D_SSM), bsz, seq)

        mix = _mix(hg, y, z, w_a_out[l], glu_w[l], glu_v[l], tm=1024, tn=512)
        x1, x1b = _outproj_ln(mix, perm_bwd, xf, w_out[l].astype(BF16),
                              ln1_g[l].reshape(1, D_MODEL), ln1_b[l].reshape(1, D_MODEL),
                              alpha)
        a = _mlp_up(x1b, mlp_w_up[l], mlp_b_up[l].reshape(1, D_FF), tm=2048, tn=1024)
        x2 = _mlp_down_ln(a, mlp_w_down[l].astype(BF16), x1,
                          mlp_b_down[l].reshape(1, D_MODEL),
                          ln2_g[l].reshape(1, D_MODEL), ln2_b[l].reshape(1, D_MODEL),
                          alpha, tm=1024, tk=1024)
        x = x2.reshape(bsz, seq, D_MODEL)
    return x
```

```python
import functools

import numpy as np
import jax
import jax.numpy as jnp
from jax.experimental import pallas as pl
from jax.experimental.pallas import tpu as pltpu

F32 = jnp.float32
BF16 = jnp.bfloat16

D_MODEL = 2048
D_RNN = D_MODEL
RG_HEADS = 16
RG_HEAD_DIM = D_RNN // RG_HEADS
CONV_WIDTH = 4
RG_C = 8.0
D_SSM = D_MODEL // 2
SSM_GROUP = 16
SSM_GROUPS = D_SSM // SSM_GROUP
SSM_STATE = 64
D_FF = 4 * D_MODEL
D_IN = 2 * D_RNN + D_SSM + 2 * D_MODEL
LN_EPS = 1e-5

SUBLANES = 8
BF16_ROWS = 2 * SUBLANES
VMEM_LIMIT = 56 * 1024 * 1024

NSEG = SUBLANES
PB = 512
SEG = PB // NSEG
HALO = (CONV_WIDTH - 1) * SUBLANES
ROW_CHUNK = 256

RG_TILE = 512
S5_TILE = 256
S5_TILE_GROUPS = S5_TILE // SSM_GROUP
S5_TILE_STATES = S5_TILE_GROUPS * SSM_STATE
S5_SCAN_LANES = 256
W_COL_GATE = D_RNN
W_COL_REST = 2 * D_RNN
Z_COLS = D_SSM + 2 * D_MODEL
Z_COL_GA = D_SSM
Z_COL_GB = D_SSM + D_MODEL


def _params(*sem):
    return pltpu.CompilerParams(dimension_semantics=sem, vmem_limit_bytes=VMEM_LIMIT)


def _interleave_matrix():
    p = np.arange(PB)
    src = (p % NSEG) * SEG + p // NSEG
    mat = np.zeros((PB, PB), np.float32)
    mat[p, src] = 1.0
    return mat


def _row_ids(shape):
    return jax.lax.broadcasted_iota(jnp.int32, shape, 0)


def _shift_rows(v, d, fill):
    return jnp.where(_row_ids(v.shape) >= d, pltpu.roll(v, d, 0), fill)


def _interleave_kernel(x_ref, p_ref, o_ref):
    o_ref[...] = jnp.dot(p_ref[...], x_ref[...].astype(BF16),
                         preferred_element_type=F32).astype(BF16)


def _interleave(x, perm):
    m, k = x.shape
    return pl.pallas_call(
        _interleave_kernel,
        grid=(m // PB,),
        in_specs=[pl.BlockSpec((PB, k), lambda i: (i, 0)),
                  pl.BlockSpec((PB, PB), lambda i: (0, 0))],
        out_specs=pl.BlockSpec((PB, k), lambda i: (i, 0)),
        out_shape=jax.ShapeDtypeStruct((m, k), BF16),
        compiler_params=_params("parallel"),
        name="interleave",
    )(x, perm)


def _in_proj_kernel(x_ref, w_ref, o_ref, wb_ref):
    @pl.when(pl.program_id(1) == 0)
    def _():
        wb_ref[...] = w_ref[...].astype(BF16)

    o_ref[...] = jnp.dot(x_ref[...], wb_ref[...],
                         preferred_element_type=F32).astype(o_ref.dtype)


def _in_proj(xp, w, tm, tn):
    m, k = xp.shape
    col0 = W_COL_REST // tn
    return pl.pallas_call(
        _in_proj_kernel,
        grid=(Z_COLS // tn, m // tm),
        in_specs=[pl.BlockSpec((tm, k), lambda j, i: (i, 0)),
                  pl.BlockSpec((k, tn), lambda j, i: (0, col0 + j))],
        out_specs=pl.BlockSpec((tm, tn), lambda j, i: (i, j)),
        out_shape=jax.ShapeDtypeStruct((m, Z_COLS), BF16),
        scratch_shapes=[pltpu.VMEM((k, tn), BF16)],
        compiler_params=_params("parallel", "arbitrary"),
        name="in_proj",
    )(xp, w)


def _rglru_gates(hh, z_ref, cw_ref, cb_ref, w_ref, ext_ref, xc_ref, pre_ref, gg_ref):
    sl = slice(hh * RG_HEAD_DIM, (hh + 1) * RG_HEAD_DIM)
    gg_ref[:, sl] = jax.nn.gelu(
        z_ref[:, RG_TILE + hh * RG_HEAD_DIM:RG_TILE + (hh + 1) * RG_HEAD_DIM])
    grp = (SUBLANES, RG_HEAD_DIM)
    x = z_ref[:, sl]
    first = _row_ids(grp) == 0
    halos = []
    for k in range(CONV_WIDTH - 1):
        prev_g = ext_ref[PB + k * SUBLANES:PB + (k + 1) * SUBLANES, sl]
        cur_g = x[PB - HALO + k * SUBLANES:PB - HALO + (k + 1) * SUBLANES, :]
        halos.append(jnp.where(first, pltpu.roll(prev_g, 1, 0), pltpu.roll(cur_g, 1, 0)))
    for k in range(CONV_WIDTH - 1):
        ext_ref[k * SUBLANES:(k + 1) * SUBLANES, sl] = halos[k]
    ext_ref[HALO:HALO + PB, sl] = x

    xc = cb_ref[:, sl] + cw_ref[CONV_WIDTH - 1:CONV_WIDTH, sl] * x
    for k in range(CONV_WIDTH - 1):
        xc = xc + cw_ref[k:k + 1, sl] * ext_ref[k * SUBLANES:k * SUBLANES + PB, sl]

    xc_ref[:, sl] = xc
    pre_ref[:, 2 * hh * RG_HEAD_DIM:2 * (hh + 1) * RG_HEAD_DIM] = jnp.dot(
        xc.astype(BF16), w_ref[hh], preferred_element_type=F32)


def _rglru_head(hh, o_ref, row0, cvec, ba_ref, bx_ref, xc_ref, pre_ref, gg_ref,
                a_ref, b_ref, carry_ref):
    sl = slice(hh * RG_HEAD_DIM, (hh + 1) * RG_HEAD_DIM)
    grp = (SUBLANES, RG_HEAD_DIM)
    first = _row_ids(grp) == 0
    xc = xc_ref[:, sl]
    r = jax.nn.sigmoid(pre_ref[:, 2 * hh * RG_HEAD_DIM:(2 * hh + 1) * RG_HEAD_DIM] + ba_ref[:, sl])
    i = jax.nn.sigmoid(pre_ref[:, (2 * hh + 1) * RG_HEAD_DIM:(2 * hh + 2) * RG_HEAD_DIM]
                       + bx_ref[:, sl])
    log_a = cvec[:, sl] * r
    a = jnp.exp(log_a)
    a_ref[:, sl] = a
    one_minus_a2 = -jnp.tanh(log_a) * (1.0 + a * a)
    b_ref[:, sl] = jnp.sqrt(one_minus_a2) * (i * xc)

    h_end, a_end = b_ref[0:SUBLANES, sl], a_ref[0:SUBLANES, sl]
    for tau in range(1, SEG):
        rows = slice(tau * SUBLANES, (tau + 1) * SUBLANES)
        a = a_ref[rows, sl]
        h_end = a * h_end + b_ref[rows, sl]
        a_end = a * a_end
        b_ref[rows, sl] = h_end
        a_ref[rows, sl] = a_end

    for d in (1, 2, 4):
        h_end = a_end * _shift_rows(h_end, d, 0.0) + h_end
        a_end = a_end * _shift_rows(a_end, d, 1.0)
    carry = carry_ref[:, sl]
    h_true = h_end + a_end * carry
    h_in = jnp.where(first, carry, pltpu.roll(h_true, 1, 0))
    carry_ref[:, sl] = h_true[SUBLANES - 1:SUBLANES, :]

    h_in2 = jnp.concatenate([h_in, h_in], axis=0)
    for j in range(PB // BF16_ROWS):
        rows = slice(j * BF16_ROWS, (j + 1) * BF16_ROWS)
        h = b_ref[rows, sl] + a_ref[rows, sl] * h_in2
        o_ref[row0 + j * BF16_ROWS:row0 + (j + 1) * BF16_ROWS, sl] = (
            h * gg_ref[rows, sl]).astype(o_ref.dtype)


def _rglru_kernel(x0_ref, x1_ref, x2_ref, wx_ref, wg_ref, cw_ref, cb_ref, w_ref,
                  ba_ref, bx_ref, lam_ref, o_ref,
                  wb_ref, z_ref, ext_ref, xca_ref, prea_ref, gga_ref, a_ref, b_ref, carry_ref):
    t = pl.program_id(2)
    n_heads = RG_TILE // RG_HEAD_DIM
    bufs = (xca_ref, prea_ref, gga_ref)

    def project(x_ref):
        z_ref[...] = jnp.dot(x_ref[...], wb_ref[...], preferred_element_type=F32)

    def gates():
        for hh in range(n_heads):
            _rglru_gates(hh, z_ref, cw_ref, cb_ref, w_ref, ext_ref, *bufs)

    @pl.when(t == 0)
    def _():
        wb_ref[:, :RG_TILE] = wx_ref[...].astype(BF16)
        wb_ref[:, RG_TILE:] = wg_ref[...].astype(BF16)
        ext_ref[PB:PB + HALO, :] = jnp.zeros((HALO, RG_TILE), F32)
        carry_ref[...] = jnp.zeros_like(carry_ref)
        project(x0_ref)
        gates()

    neg = -lam_ref[...]
    softplus = jnp.maximum(neg, 0.0) + jnp.log1p(jnp.exp(-jnp.abs(neg)))
    cvec = -RG_C * softplus

    def back(row0):
        for hh in range(n_heads):
            _rglru_head(hh, o_ref, row0, cvec, ba_ref, bx_ref, *bufs, a_ref, b_ref, carry_ref)

    project(x1_ref)
    back(0)
    gates()
    project(x2_ref)
    back(PB)
    gates()


def _rglru(xp, w_in, conv_w, conv_b, w_gate, ba, bx, lam, bsz, seq):
    n_t = seq // PB
    assert n_t % 2 == 0
    n_c = D_RNN // RG_TILE
    vec = lambda b, c, t: (0, c)
    return pl.pallas_call(
        _rglru_kernel,
        grid=(bsz, n_c, n_t // 2),
        in_specs=[pl.BlockSpec((PB, D_MODEL), lambda b, c, t: (b * n_t, 0)),
                  pl.BlockSpec((PB, D_MODEL), lambda b, c, t: (b * n_t + 2 * t + 1, 0)),
                  pl.BlockSpec((PB, D_MODEL),
                               lambda b, c, t: (b * n_t + jnp.minimum(2 * t + 2, n_t - 1), 0)),
                  pl.BlockSpec((D_MODEL, RG_TILE), vec),
                  pl.BlockSpec((D_MODEL, RG_TILE), lambda b, c, t: (0, W_COL_GATE // RG_TILE + c)),
                  pl.BlockSpec((CONV_WIDTH, RG_TILE), vec),
                  pl.BlockSpec((1, RG_TILE), vec),
                  pl.BlockSpec((RG_TILE // RG_HEAD_DIM, RG_HEAD_DIM, 2 * RG_HEAD_DIM),
                               lambda b, c, t: (c, 0, 0)),
                  pl.BlockSpec((1, RG_TILE), vec),
                  pl.BlockSpec((1, RG_TILE), vec),
                  pl.BlockSpec((1, RG_TILE), vec)],
        out_specs=pl.BlockSpec((2 * PB, RG_TILE), lambda b, c, t: (b * (n_t // 2) + t, c)),
        out_shape=jax.ShapeDtypeStruct((bsz * seq, D_RNN), BF16),
        scratch_shapes=[pltpu.VMEM((D_MODEL, 2 * RG_TILE), BF16),
                        pltpu.VMEM((PB, 2 * RG_TILE), F32),
                        pltpu.VMEM((HALO + PB, RG_TILE), F32),
                        pltpu.VMEM((PB, RG_TILE), F32),
                        pltpu.VMEM((PB, 2 * RG_TILE), F32),
                        pltpu.VMEM((PB, RG_TILE), F32),
                        pltpu.VMEM((PB, RG_TILE), F32),
                        pltpu.VMEM((PB, RG_TILE), F32),
                        pltpu.VMEM((1, RG_TILE), F32)],
        compiler_params=_params("parallel", "parallel", "arbitrary"),
        name="rglru",
    )(xp, xp, xp, w_in, w_in, conv_w, conv_b, w_gate, ba, bx, lam)


def _s5_discretize_kernel(are_ref, aim_ref, ldt_ref, lbr_ref, lbi_ref, fr_ref, fi_ref):
    dt = jnp.exp(ldt_ref[...])
    lr = jnp.minimum(are_ref[...], -1e-4)
    li = aim_ref[...]
    mag = jnp.exp(lr * dt)
    lbr = mag * jnp.cos(li * dt)
    lbi = mag * jnp.sin(li * dt)
    zr, zi = lbr - 1.0, lbi
    den = lr * lr + li * li
    lbr_ref[...] = lbr
    lbi_ref[...] = lbi
    fr_ref[...] = (zr * lr + zi * li) / den
    fi_ref[...] = (zi * lr - zr * li) / den


def _s5_discretize(a_re, a_im, log_dt):
    shp = jax.ShapeDtypeStruct((SSM_GROUPS, SSM_STATE), F32)
    return pl.pallas_call(
        _s5_discretize_kernel,
        out_shape=(shp, shp, shp, shp),
        name="s5_discretize",
    )(a_re, a_im, log_dt.reshape(SSM_GROUPS, 1))


def _cmul(ar, ai, br, bi):
    return ar * br - ai * bi, ar * bi + ai * br


def _s5_project(u_ref, bmat_ref, xr_ref, xi_ref, cols):
    ns = S5_TILE_STATES
    u = u_ref[...]
    xr_ref[:, cols] = jnp.dot(u, bmat_ref[0, :, cols], preferred_element_type=F32)
    xi_ref[:, cols] = jnp.dot(u, bmat_ref[0, :, ns + cols.start:ns + cols.stop],
                              preferred_element_type=F32)


def _s5_scan(cols, xr_ref, xi_ref, hb_ref, lre_ref, lim_ref, tab_re_ref, tab_im_ref,
             pow_re_ref, pow_im_ref, step_re_ref, step_im_ref, cr_ref, ci_ref):
    ns = S5_TILE_STATES
    grp = (SUBLANES, S5_SCAN_LANES)
    first = _row_ids(grp) == 0
    lr = jnp.broadcast_to(lre_ref[:, cols], grp)
    li = jnp.broadcast_to(lim_ref[:, cols], grp)

    er, ei = xr_ref[0:SUBLANES, cols], xi_ref[0:SUBLANES, cols]
    for tau in range(1, SEG):
        rows = slice(tau * SUBLANES, (tau + 1) * SUBLANES)
        nr = (lr * er - li * ei) + xr_ref[rows, cols]
        ni = (lr * ei + li * er) + xi_ref[rows, cols]
        xr_ref[rows, cols] = nr
        xi_ref[rows, cols] = ni
        er, ei = nr, ni

    for k, d in enumerate((1, 2, 4)):
        sr, si = pltpu.roll(er, d, 0), pltpu.roll(ei, d, 0)
        mr, mi = _cmul(step_re_ref[k, :, cols], step_im_ref[k, :, cols], sr, si)
        er, ei = er + mr, ei + mi
    cr, ci = cr_ref[:, cols], ci_ref[:, cols]
    mr, mi = _cmul(pow_re_ref[:, cols], pow_im_ref[:, cols], cr, ci)
    er, ei = er + mr, ei + mi
    in_r = jnp.where(first, cr, pltpu.roll(er, 1, 0))
    in_i = jnp.where(first, ci, pltpu.roll(ei, 1, 0))
    cr_ref[:, cols] = er[SUBLANES - 1:SUBLANES, :]
    ci_ref[:, cols] = ei[SUBLANES - 1:SUBLANES, :]

    in_r2 = jnp.concatenate([in_r, in_r], axis=0)
    in_i2 = jnp.concatenate([in_i, in_i], axis=0)
    for i in range(PB // BF16_ROWS):
        rows = slice(i * BF16_ROWS, (i + 1) * BF16_ROWS)
        fr, fi = _cmul(tab_re_ref[rows, cols], tab_im_ref[rows, cols], in_r2, in_i2)
        hb_ref[rows, cols] = (xr_ref[rows, cols] + fr).astype(BF16)
        hb_ref[rows, ns + cols.start:ns + cols.stop] = (xi_ref[rows, cols] + fi).astype(BF16)


def _s5_kernel(u0_ref, uc_ref, un_ref, bmat_ref, cmat_ref, lre_ref, lim_ref, d_ref, o_ref,
               xr_ref, xi_ref, hb_ref, tab_re_ref, tab_im_ref,
               pow_re_ref, pow_im_ref, step_re_ref, step_im_ref, cr_ref, ci_ref):
    t = pl.program_id(2)
    ns = S5_TILE_STATES
    grp = (SUBLANES, ns)
    chunks = [slice(c0, c0 + S5_SCAN_LANES) for c0 in range(0, ns, S5_SCAN_LANES)]

    @pl.when(t == 0)
    def _():
        for cols in chunks:
            _s5_project(u0_ref, bmat_ref, xr_ref, xi_ref, cols)
        cr_ref[...] = jnp.zeros_like(cr_ref)
        ci_ref[...] = jnp.zeros_like(ci_ref)
        lr = jnp.broadcast_to(lre_ref[...], grp)
        li = jnp.broadcast_to(lim_ref[...], grp)

        def fill(tau, p):
            pr, pi = p
            rows = pl.ds(pl.multiple_of(tau * SUBLANES, SUBLANES), SUBLANES)
            tab_re_ref[rows, :] = pr
            tab_im_ref[rows, :] = pi
            return _cmul(pr, pi, lr, li)

        jax.lax.fori_loop(0, SEG, fill, (lr, li))
        mr = tab_re_ref[PB - SUBLANES:PB, :]
        mi = tab_im_ref[PB - SUBLANES:PB, :]
        row = _row_ids(grp)
        pr, pi = mr, mi
        for r in range(SUBLANES):
            pow_re_ref[r:r + 1, :] = pr[0:1, :]
            pow_im_ref[r:r + 1, :] = pi[0:1, :]
            if r + 1 in (1, 2, 4):
                k = (1, 2, 4).index(r + 1)
                step_re_ref[k] = jnp.where(row >= r + 1, pr, 0.0)
                step_im_ref[k] = jnp.where(row >= r + 1, pi, 0.0)
            pr, pi = _cmul(pr, pi, mr, mi)

    scan_refs = (lre_ref, lim_ref, tab_re_ref, tab_im_ref, pow_re_ref, pow_im_ref,
                 step_re_ref, step_im_ref, cr_ref, ci_ref)
    y = d_ref[...] * uc_ref[...].astype(F32)
    for cols in chunks:
        _s5_scan(cols, xr_ref, xi_ref, hb_ref, *scan_refs)
        y = y + jnp.dot(hb_ref[:, cols], cmat_ref[0, cols, :], preferred_element_type=F32)
        y = y + jnp.dot(hb_ref[:, ns + cols.start:ns + cols.stop],
                        cmat_ref[0, ns + cols.start:ns + cols.stop, :],
                        preferred_element_type=F32)
        _s5_project(un_ref, bmat_ref, xr_ref, xi_ref, cols)
    o_ref[...] = jax.nn.gelu(y).astype(o_ref.dtype)


def _s5(z, bmat, cmat, lam_re, lam_im, dvec, bsz, seq):
    n_t = seq // PB
    n_tiles = D_SSM // S5_TILE
    ns = S5_TILE_STATES
    vec = lambda b, c, t: (0, c)
    blk = lambda f: pl.BlockSpec((PB, S5_TILE), lambda b, c, t: (b * n_t + f(t), c))
    return pl.pallas_call(
        _s5_kernel,
        grid=(bsz, n_tiles, n_t),
        in_specs=[blk(lambda t: 0),
                  blk(lambda t: t),
                  blk(lambda t: jnp.minimum(t + 1, n_t - 1)),
                  pl.BlockSpec((1, S5_TILE, 2 * ns), lambda b, c, t: (c, 0, 0)),
                  pl.BlockSpec((1, 2 * ns, S5_TILE), lambda b, c, t: (c, 0, 0)),
                  pl.BlockSpec((1, ns), vec),
                  pl.BlockSpec((1, ns), vec),
                  pl.BlockSpec((1, S5_TILE), vec)],
        out_specs=blk(lambda t: t),
        out_shape=jax.ShapeDtypeStruct((bsz * seq, D_SSM), BF16),
        scratch_shapes=[pltpu.VMEM((PB, ns), F32),
                        pltpu.VMEM((PB, ns), F32),
                        pltpu.VMEM((PB, 2 * ns), BF16),
                        pltpu.VMEM((PB, ns), F32),
                        pltpu.VMEM((PB, ns), F32),
                        pltpu.VMEM((SUBLANES, ns), F32),
                        pltpu.VMEM((SUBLANES, ns), F32),
                        pltpu.VMEM((3, SUBLANES, ns), F32),
                        pltpu.VMEM((3, SUBLANES, ns), F32),
                        pltpu.VMEM((1, ns), F32),
                        pltpu.VMEM((1, ns), F32)],
        compiler_params=_params("parallel", "parallel", "arbitrary"),
        name="s5",
    )(z, z, z, bmat, cmat, lam_re, lam_im, dvec)


def _mix_kernel(hg_ref, y_ref, ga_ref, gb_ref, wa_ref, gw_ref, gv_ref, o_ref,
                wab_ref, gwb_ref, gvb_ref):
    @pl.when(pl.program_id(1) == 0)
    def _():
        wab_ref[...] = wa_ref[...].astype(BF16)
        gwb_ref[...] = gw_ref[...].astype(BF16)
        gvb_ref[...] = gv_ref[...].astype(BF16)

    for c in range(o_ref.shape[0] // ROW_CHUNK):
        rows = slice(c * ROW_CHUNK, (c + 1) * ROW_CHUNK)
        y_a = jnp.dot(hg_ref[rows, :], wab_ref[...], preferred_element_type=F32)
        y = y_ref[rows, :]
        y_b = (jnp.dot(y, gwb_ref[...], preferred_element_type=F32)
               * jax.nn.sigmoid(jnp.dot(y, gvb_ref[...], preferred_element_type=F32)))
        mix = (jax.nn.sigmoid(ga_ref[rows, :].astype(F32)) * y_a
               + jax.nn.sigmoid(gb_ref[rows, :].astype(F32)) * y_b)
        o_ref[rows, :] = mix.astype(o_ref.dtype)


def _mix(hg, y, z, w_a, glu_w, glu_v, tm, tn):
    m = hg.shape[0]
    return pl.pallas_call(
        _mix_kernel,
        grid=(D_MODEL // tn, m // tm),
        in_specs=[pl.BlockSpec((tm, D_RNN), lambda j, i: (i, 0)),
                  pl.BlockSpec((tm, D_SSM), lambda j, i: (i, 0)),
                  pl.BlockSpec((tm, tn), lambda j, i: (i, Z_COL_GA // tn + j)),
                  pl.BlockSpec((tm, tn), lambda j, i: (i, Z_COL_GB // tn + j)),
                  pl.BlockSpec((D_RNN, tn), lambda j, i: (0, j)),
                  pl.BlockSpec((D_SSM, tn), lambda j, i: (0, j)),
                  pl.BlockSpec((D_SSM, tn), lambda j, i: (0, j))],
        out_specs=pl.BlockSpec((tm, tn), lambda j, i: (i, j)),
        out_shape=jax.ShapeDtypeStruct((m, D_MODEL), BF16),
        scratch_shapes=[pltpu.VMEM((D_RNN, tn), BF16),
                        pltpu.VMEM((D_SSM, tn), BF16),
                        pltpu.VMEM((D_SSM, tn), BF16)],
        compiler_params=_params("parallel", "arbitrary"),
        name="mix",
    )(hg, y, z, z, w_a, glu_w, glu_v)


def _layernorm(v, g, b):
    mu = jnp.mean(v, axis=-1, keepdims=True)
    c = v - mu
    var = jnp.mean(c * c, axis=-1, keepdims=True)
    return c * jax.lax.rsqrt(var + LN_EPS) * g + b


def _outproj_ln_kernel(mix_ref, pt_ref, x_ref, w_ref, g_ref, b_ref, o_ref, ob_ref, *, alpha):
    for c in range(PB // ROW_CHUNK):
        rows = slice(c * ROW_CHUNK, (c + 1) * ROW_CHUNK)
        mix = jnp.dot(pt_ref[rows, :], mix_ref[...], preferred_element_type=F32).astype(BF16)
        v = alpha * x_ref[rows, :] + jnp.dot(mix, w_ref[...], preferred_element_type=F32)
        out = _layernorm(v, g_ref[...], b_ref[...])
        o_ref[rows, :] = out
        ob_ref[rows, :] = out.astype(BF16)


def _outproj_ln(mix, perm_t, x, w_out, g, b, alpha):
    m = mix.shape[0]
    row = lambda i: (i, 0)
    fixed = lambda i: (0, 0)
    return pl.pallas_call(
        functools.partial(_outproj_ln_kernel, alpha=alpha),
        grid=(m // PB,),
        in_specs=[pl.BlockSpec((PB, D_MODEL), row),
                  pl.BlockSpec((PB, PB), fixed),
                  pl.BlockSpec((PB, D_MODEL), row),
                  pl.BlockSpec((D_MODEL, D_MODEL), fixed),
                  pl.BlockSpec((1, D_MODEL), fixed),
                  pl.BlockSpec((1, D_MODEL), fixed)],
        out_specs=(pl.BlockSpec((PB, D_MODEL), row), pl.BlockSpec((PB, D_MODEL), row)),
        out_shape=(jax.ShapeDtypeStruct((m, D_MODEL), F32),
                   jax.ShapeDtypeStruct((m, D_MODEL), BF16)),
        compiler_params=_params("parallel"),
        name="outproj_ln",
    )(mix, perm_t, x, w_out, g, b)


def _mlp_up_kernel(x_ref, w_ref, b_ref, o_ref, wb_ref):
    @pl.when(pl.program_id(1) == 0)
    def _():
        wb_ref[...] = w_ref[...].astype(BF16)

    v = jnp.dot(x_ref[...], wb_ref[...], preferred_element_type=F32) + b_ref[...]
    v = jnp.maximum(v, 0.0)
    o_ref[...] = (v * v).astype(o_ref.dtype)


def _mlp_up(xb, w_up, b_up, tm, tn):
    m = xb.shape[0]
    return pl.pallas_call(
        _mlp_up_kernel,
        grid=(D_FF // tn, m // tm),
        in_specs=[pl.BlockSpec((tm, D_MODEL), lambda j, i: (i, 0)),
                  pl.BlockSpec((D_MODEL, tn), lambda j, i: (0, j)),
                  pl.BlockSpec((1, tn), lambda j, i: (0, j))],
        out_specs=pl.BlockSpec((tm, tn), lambda j, i: (i, j)),
        out_shape=jax.ShapeDtypeStruct((m, D_FF), BF16),
        scratch_shapes=[pltpu.VMEM((D_MODEL, tn), BF16)],
        compiler_params=_params("parallel", "arbitrary"),
        name="mlp_up",
    )(xb, w_up, b_up)


def _mlp_down_ln_kernel(a_ref, w_ref, x_ref, bd_ref, g_ref, b_ref, o_ref, *, alpha):
    k = pl.program_id(1)
    last = pl.num_programs(1) - 1
    slab = x_ref.shape[0]
    slab_rows = pl.ds(pl.multiple_of(k * slab, slab), slab)

    def part():
        return jnp.dot(a_ref[...], w_ref[...], preferred_element_type=F32)

    @pl.when(k == 0)
    def _():
        o_ref[...] = part()
        o_ref[slab_rows, :] += alpha * x_ref[...]

    @pl.when(jnp.logical_and(k > 0, k < last))
    def _():
        o_ref[...] += part()
        o_ref[slab_rows, :] += alpha * x_ref[...]

    @pl.when(k == last)
    def _():
        o_ref[slab_rows, :] += alpha * x_ref[...]
        for c in range(o_ref.shape[0] // ROW_CHUNK):
            rows = slice(c * ROW_CHUNK, (c + 1) * ROW_CHUNK)
            v = (o_ref[rows, :] + jnp.dot(a_ref[rows, :], w_ref[...],
                                          preferred_element_type=F32)) + bd_ref[...]
            o_ref[rows, :] = _layernorm(v, g_ref[...], b_ref[...])


def _mlp_down_ln(a, w_down, x1, b_down, g, b, alpha, tm, tk):
    m = a.shape[0]
    n_k = D_FF // tk
    fixed = lambda i, k: (0, 0)
    return pl.pallas_call(
        functools.partial(_mlp_down_ln_kernel, alpha=alpha),
        grid=(m // tm, n_k),
        in_specs=[pl.BlockSpec((tm, tk), lambda i, k: (i, k)),
                  pl.BlockSpec((tk, D_MODEL), lambda i, k: (k, 0)),
                  pl.BlockSpec((tm // n_k, D_MODEL), lambda i, k: (i * n_k + k, 0)),
                  pl.BlockSpec((1, D_MODEL), fixed),
                  pl.BlockSpec((1, D_MODEL), fixed),
                  pl.BlockSpec((1, D_MODEL), fixed)],
        out_specs=pl.BlockSpec((tm, D_MODEL), lambda i, k: (i, 0)),
        out_shape=jax.ShapeDtypeStruct((m, D_MODEL), F32),
        compiler_params=_params("parallel", "arbitrary"),
        name="mlp_down_ln",
    )(a, w_down, x1, b_down, g, b)


def _s5_block_matrices(fr, fi, b_re, b_im, c_re, c_im):
    bbr = fr[..., None] * b_re - fi[..., None] * b_im
    bbi = fr[..., None] * b_im + fi[..., None] * b_re
    n_tiles = D_SSM // S5_TILE
    eye = jnp.eye(S5_TILE_GROUPS, dtype=F32)

    def in_blocks(w):
        w = w.reshape(n_tiles, S5_TILE_GROUPS, SSM_STATE, SSM_GROUP)
        m = jnp.einsum("tgph,gk->tghkp", w, eye)
        return m.reshape(n_tiles, S5_TILE, S5_TILE_STATES)

    def out_blocks(w):
        w = w.reshape(n_tiles, S5_TILE_GROUPS, SSM_GROUP, SSM_STATE)
        m = jnp.einsum("tghp,gk->tgpkh", w, eye)
        return m.reshape(n_tiles, S5_TILE_STATES, S5_TILE)

    bmat = jnp.concatenate([in_blocks(bbr), in_blocks(bbi)], axis=2).astype(BF16)
    cmat = jnp.concatenate([out_blocks(c_re), out_blocks(-c_im)], axis=1).astype(BF16)
    return bmat, cmat


def kernel(x, w_in, conv_w, conv_b, rg_wa, rg_ba, rg_wx, rg_bx, rg_lambda, w_a_out, ssm_a_re, ssm_a_im, ssm_log_dt, ssm_b_re, ssm_b_im, ssm_c_re, ssm_c_im, ssm_d, glu_w, glu_v, w_out, ln1_g, ln1_b, mlp_w_up, mlp_b_up, mlp_w_down, mlp_b_down, ln2_g, ln2_b):
    bsz, seq, _ = x.shape
    assert seq % PB == 0
    m = bsz * seq
    depth = w_in.shape[0]
    alpha = (2.0 * depth) ** 0.25
    perm = _interleave_matrix()
    perm_fwd = jnp.asarray(perm, BF16)
    perm_bwd = jnp.asarray(perm.T, BF16)
    for l in range(depth):
        xf = x.reshape(m, D_MODEL)
        xp = _interleave(xf, perm_fwd)
        z = _in_proj(xp, w_in[l], tm=2048, tn=1024)

        w_gate = jnp.concatenate([rg_wa[l], rg_wx[l]], axis=-1).astype(BF16)
        hg = _rglru(xp, w_in[l], conv_w[l], conv_b[l].reshape(1, D_RNN), w_gate,
                    rg_ba[l].reshape(1, D_RNN), rg_bx[l].reshape(1, D_RNN),
                    rg_lambda[l].reshape(1, D_RNN), bsz, seq)

        lbr, lbi, fr, fi = _s5_discretize(ssm_a_re[l], ssm_a_im[l], ssm_log_dt[l])
        bmat, cmat = _s5_block_matrices(fr, fi, ssm_b_re[l], ssm_b_im[l],
                                        ssm_c_re[l], ssm_c_im[l])
        y = _s5(z, bmat, cmat, lbr.reshape(1, -1), lbi.reshape(1, -1),
                ssm_d[l].reshape(1, D_SSM), bsz, seq)

        mix = _mix(hg, y, z, w_a_out[l], glu_w[l], glu_v[l], tm=1024, tn=512)
        x1, x1b = _outproj_ln(mix, perm_bwd, xf, w_out[l].astype(BF16),
                              ln1_g[l].reshape(1, D_MODEL), ln1_b[l].reshape(1, D_MODEL),
                              alpha)
        a = _mlp_up(x1b, mlp_w_up[l], mlp_b_up[l].reshape(1, D_FF), tm=2048, tn=1024)
        x2 = _mlp_down_ln(a, mlp_w_down[l].astype(BF16), x1,
                          mlp_b_down[l].reshape(1, D_MODEL),
                          ln2_g[l].reshape(1, D_MODEL), ln2_b[l].reshape(1, D_MODEL),
                          alpha, tm=1024, tk=2048)
        x = x2.reshape(bsz, seq, D_MODEL)
    return x
```

```python
import functools

import numpy as np
import jax
import jax.numpy as jnp
from jax.experimental import pallas as pl
from jax.experimental.pallas import tpu as pltpu

F32 = jnp.float32
BF16 = jnp.bfloat16

D_MODEL = 2048
D_RNN = D_MODEL
RG_HEADS = 16
RG_HEAD_DIM = D_RNN // RG_HEADS
CONV_WIDTH = 4
RG_C = 8.0
D_SSM = D_MODEL // 2
SSM_GROUP = 16
SSM_GROUPS = D_SSM // SSM_GROUP
SSM_STATE = 64
D_FF = 4 * D_MODEL
D_IN = 2 * D_RNN + D_SSM + 2 * D_MODEL
LN_EPS = 1e-5

SUBLANES = 8
BF16_ROWS = 2 * SUBLANES
VMEM_LIMIT = 56 * 1024 * 1024

NSEG = SUBLANES
PB = 512
SEG = PB // NSEG
HALO = (CONV_WIDTH - 1) * SUBLANES
ROW_CHUNK = 256

RG_TILE = 512
S5_TILE = 256
S5_TILE_GROUPS = S5_TILE // SSM_GROUP
S5_TILE_STATES = S5_TILE_GROUPS * SSM_STATE
S5_SCAN_LANES = 256
W_COL_GATE = D_RNN
W_COL_REST = 2 * D_RNN
Z_COLS = D_SSM + 2 * D_MODEL
Z_COL_GA = D_SSM
Z_COL_GB = D_SSM + D_MODEL


def _params(*sem):
    return pltpu.CompilerParams(dimension_semantics=sem, vmem_limit_bytes=VMEM_LIMIT)


def _interleave_matrix():
    p = np.arange(PB)
    src = (p % NSEG) * SEG + p // NSEG
    mat = np.zeros((PB, PB), np.float32)
    mat[p, src] = 1.0
    return mat


def _row_ids(shape):
    return jax.lax.broadcasted_iota(jnp.int32, shape, 0)


def _shift_rows(v, d, fill):
    return jnp.where(_row_ids(v.shape) >= d, pltpu.roll(v, d, 0), fill)


def _interleave_kernel(x_ref, p_ref, o_ref):
    o_ref[...] = jnp.dot(p_ref[...], x_ref[...].astype(BF16),
                         preferred_element_type=F32).astype(BF16)


def _interleave(x, perm):
    m, k = x.shape
    return pl.pallas_call(
        _interleave_kernel,
        grid=(m // PB,),
        in_specs=[pl.BlockSpec((PB, k), lambda i: (i, 0)),
                  pl.BlockSpec((PB, PB), lambda i: (0, 0))],
        out_specs=pl.BlockSpec((PB, k), lambda i: (i, 0)),
        out_shape=jax.ShapeDtypeStruct((m, k), BF16),
        compiler_params=_params("parallel"),
        name="interleave",
    )(x, perm)


def _in_proj_kernel(x_ref, w_ref, o_ref, wb_ref):
    @pl.when(pl.program_id(1) == 0)
    def _():
        wb_ref[...] = w_ref[...].astype(BF16)

    o_ref[...] = jnp.dot(x_ref[...], wb_ref[...],
                         preferred_element_type=F32).astype(o_ref.dtype)


def _in_proj(xp, w, tm, tn):
    m, k = xp.shape
    col0 = W_COL_REST // tn
    return pl.pallas_call(
        _in_proj_kernel,
        grid=(Z_COLS // tn, m // tm),
        in_specs=[pl.BlockSpec((tm, k), lambda j, i: (i, 0)),
                  pl.BlockSpec((k, tn), lambda j, i: (0, col0 + j))],
        out_specs=pl.BlockSpec((tm, tn), lambda j, i: (i, j)),
        out_shape=jax.ShapeDtypeStruct((m, Z_COLS), BF16),
        scratch_shapes=[pltpu.VMEM((k, tn), BF16)],
        compiler_params=_params("parallel", "arbitrary"),
        name="in_proj",
    )(xp, w)


def _rglru_gates(hh, z_ref, cw_ref, cb_ref, w_ref, ext_ref, xc_ref, pre_ref, gg_ref):
    sl = slice(hh * RG_HEAD_DIM, (hh + 1) * RG_HEAD_DIM)
    gg_ref[:, sl] = jax.nn.gelu(
        z_ref[:, RG_TILE + hh * RG_HEAD_DIM:RG_TILE + (hh + 1) * RG_HEAD_DIM])
    grp = (SUBLANES, RG_HEAD_DIM)
    x = z_ref[:, sl]
    first = _row_ids(grp) == 0
    halos = []
    for k in range(CONV_WIDTH - 1):
        prev_g = ext_ref[PB + k * SUBLANES:PB + (k + 1) * SUBLANES, sl]
        cur_g = x[PB - HALO + k * SUBLANES:PB - HALO + (k + 1) * SUBLANES, :]
        halos.append(jnp.where(first, pltpu.roll(prev_g, 1, 0), pltpu.roll(cur_g, 1, 0)))
    for k in range(CONV_WIDTH - 1):
        ext_ref[k * SUBLANES:(k + 1) * SUBLANES, sl] = halos[k]
    ext_ref[HALO:HALO + PB, sl] = x

    xc = cb_ref[:, sl] + cw_ref[CONV_WIDTH - 1:CONV_WIDTH, sl] * x
    for k in range(CONV_WIDTH - 1):
        xc = xc + cw_ref[k:k + 1, sl] * ext_ref[k * SUBLANES:k * SUBLANES + PB, sl]

    xc_ref[:, sl] = xc
    pre_ref[:, 2 * hh * RG_HEAD_DIM:2 * (hh + 1) * RG_HEAD_DIM] = jnp.dot(
        xc.astype(BF16), w_ref[hh], preferred_element_type=F32)


def _rglru_head(hh, o_ref, row0, cvec, ba_ref, bx_ref, xc_ref, pre_ref, gg_ref,
                a_ref, b_ref, carry_ref):
    sl = slice(hh * RG_HEAD_DIM, (hh + 1) * RG_HEAD_DIM)
    grp = (SUBLANES, RG_HEAD_DIM)
    first = _row_ids(grp) == 0
    xc = xc_ref[:, sl]
    r = jax.nn.sigmoid(pre_ref[:, 2 * hh * RG_HEAD_DIM:(2 * hh + 1) * RG_HEAD_DIM] + ba_ref[:, sl])
    i = jax.nn.sigmoid(pre_ref[:, (2 * hh + 1) * RG_HEAD_DIM:(2 * hh + 2) * RG_HEAD_DIM]
                       + bx_ref[:, sl])
    log_a = cvec[:, sl] * r
    a = jnp.exp(log_a)
    a_ref[:, sl] = a
    one_minus_a2 = -jnp.tanh(log_a) * (1.0 + a * a)
    b_ref[:, sl] = jnp.sqrt(one_minus_a2) * (i * xc)

    h_end, a_end = b_ref[0:SUBLANES, sl], a_ref[0:SUBLANES, sl]
    for tau in range(1, SEG):
        rows = slice(tau * SUBLANES, (tau + 1) * SUBLANES)
        a = a_ref[rows, sl]
        h_end = a * h_end + b_ref[rows, sl]
        a_end = a * a_end
        b_ref[rows, sl] = h_end
        a_ref[rows, sl] = a_end

    for d in (1, 2, 4):
        h_end = a_end * _shift_rows(h_end, d, 0.0) + h_end
        a_end = a_end * _shift_rows(a_end, d, 1.0)
    carry = carry_ref[:, sl]
    h_true = h_end + a_end * carry
    h_in = jnp.where(first, carry, pltpu.roll(h_true, 1, 0))
    carry_ref[:, sl] = h_true[SUBLANES - 1:SUBLANES, :]

    h_in2 = jnp.concatenate([h_in, h_in], axis=0)
    for j in range(PB // BF16_ROWS):
        rows = slice(j * BF16_ROWS, (j + 1) * BF16_ROWS)
        h = b_ref[rows, sl] + a_ref[rows, sl] * h_in2
        o_ref[row0 + j * BF16_ROWS:row0 + (j + 1) * BF16_ROWS, sl] = (
            h * gg_ref[rows, sl]).astype(o_ref.dtype)


def _rglru_kernel(x0_ref, x1_ref, x2_ref, wx_ref, wg_ref, cw_ref, cb_ref, w_ref,
                  ba_ref, bx_ref, lam_ref, o_ref,
                  wb_ref, z_ref, ext_ref, xca_ref, prea_ref, gga_ref, a_ref, b_ref, carry_ref):
    t = pl.program_id(2)
    n_heads = RG_TILE // RG_HEAD_DIM
    bufs = (xca_ref, prea_ref, gga_ref)

    def project(x_ref):
        z_ref[...] = jnp.dot(x_ref[...], wb_ref[...], preferred_element_type=F32)

    def gates():
        for hh in range(n_heads):
            _rglru_gates(hh, z_ref, cw_ref, cb_ref, w_ref, ext_ref, *bufs)

    @pl.when(t == 0)
    def _():
        wb_ref[:, :RG_TILE] = wx_ref[...].astype(BF16)
        wb_ref[:, RG_TILE:] = wg_ref[...].astype(BF16)
        ext_ref[PB:PB + HALO, :] = jnp.zeros((HALO, RG_TILE), F32)
        carry_ref[...] = jnp.zeros_like(carry_ref)
        project(x0_ref)
        gates()

    neg = -lam_ref[...]
    softplus = jnp.maximum(neg, 0.0) + jnp.log1p(jnp.exp(-jnp.abs(neg)))
    cvec = -RG_C * softplus

    def back(row0):
        for hh in range(n_heads):
            _rglru_head(hh, o_ref, row0, cvec, ba_ref, bx_ref, *bufs, a_ref, b_ref, carry_ref)

    project(x1_ref)
    back(0)
    gates()
    project(x2_ref)
    back(PB)
    gates()


def _rglru(xp, w_in, conv_w, conv_b, w_gate, ba, bx, lam, bsz, seq):
    n_t = seq // PB
    assert n_t % 2 == 0
    n_c = D_RNN // RG_TILE
    vec = lambda b, c, t: (0, c)
    return pl.pallas_call(
        _rglru_kernel,
        grid=(bsz, n_c, n_t // 2),
        in_specs=[pl.BlockSpec((PB, D_MODEL), lambda b, c, t: (b * n_t, 0)),
                  pl.BlockSpec((PB, D_MODEL), lambda b, c, t: (b * n_t + 2 * t + 1, 0)),
                  pl.BlockSpec((PB, D_MODEL),
                               lambda b, c, t: (b * n_t + jnp.minimum(2 * t + 2, n_t - 1), 0)),
                  pl.BlockSpec((D_MODEL, RG_TILE), vec),
                  pl.BlockSpec((D_MODEL, RG_TILE), lambda b, c, t: (0, W_COL_GATE // RG_TILE + c)),
                  pl.BlockSpec((CONV_WIDTH, RG_TILE), vec),
                  pl.BlockSpec((1, RG_TILE), vec),
                  pl.BlockSpec((RG_TILE // RG_HEAD_DIM, RG_HEAD_DIM, 2 * RG_HEAD_DIM),
                               lambda b, c, t: (c, 0, 0)),
                  pl.BlockSpec((1, RG_TILE), vec),
                  pl.BlockSpec((1, RG_TILE), vec),
                  pl.BlockSpec((1, RG_TILE), vec)],
        out_specs=pl.BlockSpec((2 * PB, RG_TILE), lambda b, c, t: (b * (n_t // 2) + t, c)),
        out_shape=jax.ShapeDtypeStruct((bsz * seq, D_RNN), BF16),
        scratch_shapes=[pltpu.VMEM((D_MODEL, 2 * RG_TILE), BF16),
                        pltpu.VMEM((PB, 2 * RG_TILE), F32),
                        pltpu.VMEM((HALO + PB, RG_TILE), F32),
                        pltpu.VMEM((PB, RG_TILE), F32),
                        pltpu.VMEM((PB, 2 * RG_TILE), F32),
                        pltpu.VMEM((PB, RG_TILE), F32),
                        pltpu.VMEM((PB, RG_TILE), F32),
                        pltpu.VMEM((PB, RG_TILE), F32),
                        pltpu.VMEM((1, RG_TILE), F32)],
        compiler_params=_params("parallel", "parallel", "arbitrary"),
        name="rglru",
    )(xp, xp, xp, w_in, w_in, conv_w, conv_b, w_gate, ba, bx, lam)


def _s5_discretize_kernel(are_ref, aim_ref, ldt_ref, lbr_ref, lbi_ref, fr_ref, fi_ref):
    dt = jnp.exp(ldt_ref[...])
    lr = jnp.minimum(are_ref[...], -1e-4)
    li = aim_ref[...]
    mag = jnp.exp(lr * dt)
    lbr = mag * jnp.cos(li * dt)
    lbi = mag * jnp.sin(li * dt)
    zr, zi = lbr - 1.0, lbi
    den = lr * lr + li * li
    lbr_ref[...] = lbr
    lbi_ref[...] = lbi
    fr_ref[...] = (zr * lr + zi * li) / den
    fi_ref[...] = (zi * lr - zr * li) / den


def _s5_discretize(a_re, a_im, log_dt):
    shp = jax.ShapeDtypeStruct((SSM_GROUPS, SSM_STATE), F32)
    return pl.pallas_call(
        _s5_discretize_kernel,
        out_shape=(shp, shp, shp, shp),
        name="s5_discretize",
    )(a_re, a_im, log_dt.reshape(SSM_GROUPS, 1))


def _cmul(ar, ai, br, bi):
    return ar * br - ai * bi, ar * bi + ai * br


def _s5_project(u_ref, bmat_ref, xr_ref, xi_ref, cols):
    ns = S5_TILE_STATES
    u = u_ref[...]
    xr_ref[:, cols] = jnp.dot(u, bmat_ref[0, :, cols], preferred_element_type=F32)
    xi_ref[:, cols] = jnp.dot(u, bmat_ref[0, :, ns + cols.start:ns + cols.stop],
                              preferred_element_type=F32)


def _s5_scan(cols, xr_ref, xi_ref, hb_ref, lre_ref, lim_ref, tab_re_ref, tab_im_ref,
             pow_re_ref, pow_im_ref, step_re_ref, step_im_ref, cr_ref, ci_ref):
    ns = S5_TILE_STATES
    grp = (SUBLANES, S5_SCAN_LANES)
    first = _row_ids(grp) == 0
    lr = jnp.broadcast_to(lre_ref[:, cols], grp)
    li = jnp.broadcast_to(lim_ref[:, cols], grp)

    er, ei = xr_ref[0:SUBLANES, cols], xi_ref[0:SUBLANES, cols]
    for tau in range(1, SEG):
        rows = slice(tau * SUBLANES, (tau + 1) * SUBLANES)
        nr = (lr * er - li * ei) + xr_ref[rows, cols]
        ni = (lr * ei + li * er) + xi_ref[rows, cols]
        xr_ref[rows, cols] = nr
        xi_ref[rows, cols] = ni
        er, ei = nr, ni

    for k, d in enumerate((1, 2, 4)):
        sr, si = pltpu.roll(er, d, 0), pltpu.roll(ei, d, 0)
        mr, mi = _cmul(step_re_ref[k, :, cols], step_im_ref[k, :, cols], sr, si)
        er, ei = er + mr, ei + mi
    cr, ci = cr_ref[:, cols], ci_ref[:, cols]
    mr, mi = _cmul(pow_re_ref[:, cols], pow_im_ref[:, cols], cr, ci)
    er, ei = er + mr, ei + mi
    in_r = jnp.where(first, cr, pltpu.roll(er, 1, 0))
    in_i = jnp.where(first, ci, pltpu.roll(ei, 1, 0))
    cr_ref[:, cols] = er[SUBLANES - 1:SUBLANES, :]
    ci_ref[:, cols] = ei[SUBLANES - 1:SUBLANES, :]

    in_r2 = jnp.concatenate([in_r, in_r], axis=0).astype(BF16)
    in_i2 = jnp.concatenate([in_i, in_i], axis=0).astype(BF16)
    for i in range(PB // BF16_ROWS):
        rows = slice(i * BF16_ROWS, (i + 1) * BF16_ROWS)
        fr, fi = _cmul(tab_re_ref[rows, cols], tab_im_ref[rows, cols], in_r2, in_i2)
        hb_ref[rows, cols] = xr_ref[rows, cols].astype(BF16) + fr
        hb_ref[rows, ns + cols.start:ns + cols.stop] = xi_ref[rows, cols].astype(BF16) + fi


def _s5_kernel(u0_ref, uc_ref, un_ref, bmat_ref, cmat_ref, lre_ref, lim_ref, d_ref, o_ref,
               xr_ref, xi_ref, hb_ref, tab_re_ref, tab_im_ref, tabb_re_ref, tabb_im_ref,
               pow_re_ref, pow_im_ref, step_re_ref, step_im_ref, cr_ref, ci_ref):
    t = pl.program_id(2)
    ns = S5_TILE_STATES
    grp = (SUBLANES, ns)
    chunks = [slice(c0, c0 + S5_SCAN_LANES) for c0 in range(0, ns, S5_SCAN_LANES)]

    @pl.when(t == 0)
    def _():
        for cols in chunks:
            _s5_project(u0_ref, bmat_ref, xr_ref, xi_ref, cols)
        cr_ref[...] = jnp.zeros_like(cr_ref)
        ci_ref[...] = jnp.zeros_like(ci_ref)
        lr = jnp.broadcast_to(lre_ref[...], grp)
        li = jnp.broadcast_to(lim_ref[...], grp)

        def fill(tau, p):
            pr, pi = p
            rows = pl.ds(pl.multiple_of(tau * SUBLANES, SUBLANES), SUBLANES)
            tab_re_ref[rows, :] = pr
            tab_im_ref[rows, :] = pi
            return _cmul(pr, pi, lr, li)

        jax.lax.fori_loop(0, SEG, fill, (lr, li))
        tabb_re_ref[...] = tab_re_ref[...].astype(BF16)
        tabb_im_ref[...] = tab_im_ref[...].astype(BF16)
        mr = tab_re_ref[PB - SUBLANES:PB, :]
        mi = tab_im_ref[PB - SUBLANES:PB, :]
        row = _row_ids(grp)
        pr, pi = mr, mi
        for r in range(SUBLANES):
            pow_re_ref[r:r + 1, :] = pr[0:1, :]
            pow_im_ref[r:r + 1, :] = pi[0:1, :]
            if r + 1 in (1, 2, 4):
                k = (1, 2, 4).index(r + 1)
                step_re_ref[k] = jnp.where(row >= r + 1, pr, 0.0)
                step_im_ref[k] = jnp.where(row >= r + 1, pi, 0.0)
            pr, pi = _cmul(pr, pi, mr, mi)

    scan_refs = (lre_ref, lim_ref, tabb_re_ref, tabb_im_ref, pow_re_ref, pow_im_ref,
                 step_re_ref, step_im_ref, cr_ref, ci_ref)
    y = d_ref[...] * uc_ref[...].astype(F32)
    for cols in chunks:
        _s5_scan(cols, xr_ref, xi_ref, hb_ref, *scan_refs)
        y = y + jnp.dot(hb_ref[:, cols], cmat_ref[0, cols, :], preferred_element_type=F32)
        y = y + jnp.dot(hb_ref[:, ns + cols.start:ns + cols.stop],
                        cmat_ref[0, ns + cols.start:ns + cols.stop, :],
                        preferred_element_type=F32)
        _s5_project(un_ref, bmat_ref, xr_ref, xi_ref, cols)
    o_ref[...] = jax.nn.gelu(y).astype(o_ref.dtype)


def _s5(z, bmat, cmat, lam_re, lam_im, dvec, bsz, seq):
    n_t = seq // PB
    n_tiles = D_SSM // S5_TILE
    ns = S5_TILE_STATES
    vec = lambda b, c, t: (0, c)
    blk = lambda f: pl.BlockSpec((PB, S5_TILE), lambda b, c, t: (b * n_t + f(t), c))
    return pl.pallas_call(
        _s5_kernel,
        grid=(bsz, n_tiles, n_t),
        in_specs=[blk(lambda t: 0),
                  blk(lambda t: t),
                  blk(lambda t: jnp.minimum(t + 1, n_t - 1)),
                  pl.BlockSpec((1, S5_TILE, 2 * ns), lambda b, c, t: (c, 0, 0)),
                  pl.BlockSpec((1, 2 * ns, S5_TILE), lambda b, c, t: (c, 0, 0)),
                  pl.BlockSpec((1, ns), vec),
                  pl.BlockSpec((1, ns), vec),
                  pl.BlockSpec((1, S5_TILE), vec)],
        out_specs=blk(lambda t: t),
        out_shape=jax.ShapeDtypeStruct((bsz * seq, D_SSM), BF16),
        scratch_shapes=[pltpu.VMEM((PB, ns), F32),
                        pltpu.VMEM((PB, ns), F32),
                        pltpu.VMEM((PB, 2 * ns), BF16),
                        pltpu.VMEM((PB, ns), F32),
                        pltpu.VMEM((PB, ns), F32),
                        pltpu.VMEM((PB, ns), BF16),
                        pltpu.VMEM((PB, ns), BF16),
                        pltpu.VMEM((SUBLANES, ns), F32),
                        pltpu.VMEM((SUBLANES, ns), F32),
                        pltpu.VMEM((3, SUBLANES, ns), F32),
                        pltpu.VMEM((3, SUBLANES, ns), F32),
                        pltpu.VMEM((1, ns), F32),
                        pltpu.VMEM((1, ns), F32)],
        compiler_params=_params("parallel", "parallel", "arbitrary"),
        name="s5",
    )(z, z, z, bmat, cmat, lam_re, lam_im, dvec)


def _mix_kernel(hg_ref, y_ref, ga_ref, gb_ref, wa_ref, gw_ref, gv_ref, o_ref,
                wab_ref, gwb_ref, gvb_ref):
    @pl.when(pl.program_id(1) == 0)
    def _():
        wab_ref[...] = wa_ref[...].astype(BF16)
        gwb_ref[...] = gw_ref[...].astype(BF16)
        gvb_ref[...] = gv_ref[...].astype(BF16)

    for c in range(o_ref.shape[0] // ROW_CHUNK):
        rows = slice(c * ROW_CHUNK, (c + 1) * ROW_CHUNK)
        y_a = jnp.dot(hg_ref[rows, :], wab_ref[...], preferred_element_type=F32)
        y = y_ref[rows, :]
        y_b = (jnp.dot(y, gwb_ref[...], preferred_element_type=F32)
               * jax.nn.sigmoid(jnp.dot(y, gvb_ref[...], preferred_element_type=F32)))
        mix = (jax.nn.sigmoid(ga_ref[rows, :].astype(F32)) * y_a
               + jax.nn.sigmoid(gb_ref[rows, :].astype(F32)) * y_b)
        o_ref[rows, :] = mix.astype(o_ref.dtype)


def _mix(hg, y, z, w_a, glu_w, glu_v, tm, tn):
    m = hg.shape[0]
    return pl.pallas_call(
        _mix_kernel,
        grid=(D_MODEL // tn, m // tm),
        in_specs=[pl.BlockSpec((tm, D_RNN), lambda j, i: (i, 0)),
                  pl.BlockSpec((tm, D_SSM), lambda j, i: (i, 0)),
                  pl.BlockSpec((tm, tn), lambda j, i: (i, Z_COL_GA // tn + j)),
                  pl.BlockSpec((tm, tn), lambda j, i: (i, Z_COL_GB // tn + j)),
                  pl.BlockSpec((D_RNN, tn), lambda j, i: (0, j)),
                  pl.BlockSpec((D_SSM, tn), lambda j, i: (0, j)),
                  pl.BlockSpec((D_SSM, tn), lambda j, i: (0, j))],
        out_specs=pl.BlockSpec((tm, tn), lambda j, i: (i, j)),
        out_shape=jax.ShapeDtypeStruct((m, D_MODEL), BF16),
        scratch_shapes=[pltpu.VMEM((D_RNN, tn), BF16),
                        pltpu.VMEM((D_SSM, tn), BF16),
                        pltpu.VMEM((D_SSM, tn), BF16)],
        compiler_params=_params("parallel", "arbitrary"),
        name="mix",
    )(hg, y, z, z, w_a, glu_w, glu_v)


def _layernorm(v, g, b):
    mu = jnp.mean(v, axis=-1, keepdims=True)
    c = v - mu
    var = jnp.mean(c * c, axis=-1, keepdims=True)
    return c * jax.lax.rsqrt(var + LN_EPS) * g + b


def _outproj_ln_kernel(mix_ref, pt_ref, x_ref, w_ref, g_ref, b_ref, o_ref, ob_ref, *, alpha):
    for c in range(PB // ROW_CHUNK):
        rows = slice(c * ROW_CHUNK, (c + 1) * ROW_CHUNK)
        mix = jnp.dot(pt_ref[rows, :], mix_ref[...], preferred_element_type=F32).astype(BF16)
        v = alpha * x_ref[rows, :] + jnp.dot(mix, w_ref[...], preferred_element_type=F32)
        out = _layernorm(v, g_ref[...], b_ref[...])
        o_ref[rows, :] = out
        ob_ref[rows, :] = out.astype(BF16)


def _outproj_ln(mix, perm_t, x, w_out, g, b, alpha):
    m = mix.shape[0]
    row = lambda i: (i, 0)
    fixed = lambda i: (0, 0)
    return pl.pallas_call(
        functools.partial(_outproj_ln_kernel, alpha=alpha),
        grid=(m // PB,),
        in_specs=[pl.BlockSpec((PB, D_MODEL), row),
                  pl.BlockSpec((PB, PB), fixed),
                  pl.BlockSpec((PB, D_MODEL), row),
                  pl.BlockSpec((D_MODEL, D_MODEL), fixed),
                  pl.BlockSpec((1, D_MODEL), fixed),
                  pl.BlockSpec((1, D_MODEL), fixed)],
        out_specs=(pl.BlockSpec((PB, D_MODEL), row), pl.BlockSpec((PB, D_MODEL), row)),
        out_shape=(jax.ShapeDtypeStruct((m, D_MODEL), F32),
                   jax.ShapeDtypeStruct((m, D_MODEL), BF16)),
        compiler_params=_params("parallel"),
        name="outproj_ln",
    )(mix, perm_t, x, w_out, g, b)


def _mlp_up_kernel(x_ref, w_ref, b_ref, o_ref, wb_ref):
    @pl.when(pl.program_id(1) == 0)
    def _():
        wb_ref[...] = w_ref[...].astype(BF16)

    v = jnp.dot(x_ref[...], wb_ref[...], preferred_element_type=F32) + b_ref[...]
    v = jnp.maximum(v, 0.0)
    o_ref[...] = (v * v).astype(o_ref.dtype)


def _mlp_up(xb, w_up, b_up, tm, tn):
    m = xb.shape[0]
    return pl.pallas_call(
        _mlp_up_kernel,
        grid=(D_FF // tn, m // tm),
        in_specs=[pl.BlockSpec((tm, D_MODEL), lambda j, i: (i, 0)),
                  pl.BlockSpec((D_MODEL, tn), lambda j, i: (0, j)),
                  pl.BlockSpec((1, tn), lambda j, i: (0, j))],
        out_specs=pl.BlockSpec((tm, tn), lambda j, i: (i, j)),
        out_shape=jax.ShapeDtypeStruct((m, D_FF), BF16),
        scratch_shapes=[pltpu.VMEM((D_MODEL, tn), BF16)],
        compiler_params=_params("parallel", "arbitrary"),
        name="mlp_up",
    )(xb, w_up, b_up)


def _mlp_down_ln_kernel(a_ref, w_ref, x_ref, bd_ref, g_ref, b_ref, o_ref, *, alpha):
    k = pl.program_id(1)
    last = pl.num_programs(1) - 1
    slab = x_ref.shape[0]
    slab_rows = pl.ds(pl.multiple_of(k * slab, slab), slab)

    def part():
        return jnp.dot(a_ref[...], w_ref[...], preferred_element_type=F32)

    @pl.when(k == 0)
    def _():
        o_ref[...] = part()
        o_ref[slab_rows, :] += alpha * x_ref[...]

    @pl.when(jnp.logical_and(k > 0, k < last))
    def _():
        o_ref[...] += part()
        o_ref[slab_rows, :] += alpha * x_ref[...]

    @pl.when(k == last)
    def _():
        o_ref[slab_rows, :] += alpha * x_ref[...]
        for c in range(o_ref.shape[0] // ROW_CHUNK):
            rows = slice(c * ROW_CHUNK, (c + 1) * ROW_CHUNK)
            v = (o_ref[rows, :] + jnp.dot(a_ref[rows, :], w_ref[...],
                                          preferred_element_type=F32)) + bd_ref[...]
            o_ref[rows, :] = _layernorm(v, g_ref[...], b_ref[...])


def _mlp_down_ln(a, w_down, x1, b_down, g, b, alpha, tm, tk):
    m = a.shape[0]
    n_k = D_FF // tk
    fixed = lambda i, k: (0, 0)
    return pl.pallas_call(
        functools.partial(_mlp_down_ln_kernel, alpha=alpha),
        grid=(m // tm, n_k),
        in_specs=[pl.BlockSpec((tm, tk), lambda i, k: (i, k)),
                  pl.BlockSpec((tk, D_MODEL), lambda i, k: (k, 0)),
                  pl.BlockSpec((tm // n_k, D_MODEL), lambda i, k: (i * n_k + k, 0)),
                  pl.BlockSpec((1, D_MODEL), fixed),
                  pl.BlockSpec((1, D_MODEL), fixed),
                  pl.BlockSpec((1, D_MODEL), fixed)],
        out_specs=pl.BlockSpec((tm, D_MODEL), lambda i, k: (i, 0)),
        out_shape=jax.ShapeDtypeStruct((m, D_MODEL), F32),
        compiler_params=_params("parallel", "arbitrary"),
        name="mlp_down_ln",
    )(a, w_down, x1, b_down, g, b)


def _s5_block_matrices(fr, fi, b_re, b_im, c_re, c_im):
    bbr = fr[..., None] * b_re - fi[..., None] * b_im
    bbi = fr[..., None] * b_im + fi[..., None] * b_re
    n_tiles = D_SSM // S5_TILE
    eye = jnp.eye(S5_TILE_GROUPS, dtype=F32)

    def in_blocks(w):
        w = w.reshape(n_tiles, S5_TILE_GROUPS, SSM_STATE, SSM_GROUP)
        m = jnp.einsum("tgph,gk->tghkp", w, eye)
        return m.reshape(n_tiles, S5_TILE, S5_TILE_STATES)

    def out_blocks(w):
        w = w.reshape(n_tiles, S5_TILE_GROUPS, SSM_GROUP, SSM_STATE)
        m = jnp.einsum("tghp,gk->tgpkh", w, eye)
        return m.reshape(n_tiles, S5_TILE_STATES, S5_TILE)

    bmat = jnp.concatenate([in_blocks(bbr), in_blocks(bbi)], axis=2).astype(BF16)
    cmat = jnp.concatenate([out_blocks(c_re), out_blocks(-c_im)], axis=1).astype(BF16)
    return bmat, cmat


def kernel(x, w_in, conv_w, conv_b, rg_wa, rg_ba, rg_wx, rg_bx, rg_lambda, w_a_out, ssm_a_re, ssm_a_im, ssm_log_dt, ssm_b_re, ssm_b_im, ssm_c_re, ssm_c_im, ssm_d, glu_w, glu_v, w_out, ln1_g, ln1_b, mlp_w_up, mlp_b_up, mlp_w_down, mlp_b_down, ln2_g, ln2_b):
    bsz, seq, _ = x.shape
    assert seq % PB == 0
    m = bsz * seq
    depth = w_in.shape[0]
    alpha = (2.0 * depth) ** 0.25
    perm = _interleave_matrix()
    perm_fwd = jnp.asarray(perm, BF16)
    perm_bwd = jnp.asarray(perm.T, BF16)
    for l in range(depth):
        xf = x.reshape(m, D_MODEL)
        xp = _interleave(xf, perm_fwd)
        z = _in_proj(xp, w_in[l], tm=2048, tn=1024)

        w_gate = jnp.concatenate([rg_wa[l], rg_wx[l]], axis=-1).astype(BF16)
        hg = _rglru(xp, w_in[l], conv_w[l], conv_b[l].reshape(1, D_RNN), w_gate,
                    rg_ba[l].reshape(1, D_RNN), rg_bx[l].reshape(1, D_RNN),
                    rg_lambda[l].reshape(1, D_RNN), bsz, seq)

        lbr, lbi, fr, fi = _s5_discretize(ssm_a_re[l], ssm_a_im[l], ssm_log_dt[l])
        bmat, cmat = _s5_block_matrices(fr, fi, ssm_b_re[l], ssm_b_im[l],
                                        ssm_c_re[l], ssm_c_im[l])
        y = _s5(z, bmat, cmat, lbr.reshape(1, -1), lbi.reshape(1, -1),
                ssm_d[l].reshape(1, D_SSM), bsz, seq)

        mix = _mix(hg, y, z, w_a_out[l], glu_w[l], glu_v[l], tm=1024, tn=512)
        x1, x1b = _outproj_ln(mix, perm_bwd, xf, w_out[l].astype(BF16),
                              ln1_g[l].reshape(1, D_MODEL), ln1_b[l].reshape(1, D_MODEL),
                              alpha)
        a = _mlp_up(x1b, mlp_w_up[l], mlp_b_up[l].reshape(1, D_FF), tm=2048, tn=1024)
        x2 = _mlp_down_ln(a, mlp_w_down[l].astype(BF16), x1,
                          mlp_b_down[l].reshape(1, D_MODEL),
                          ln2_g[l].reshape(1, D_MODEL), ln2_b[l].reshape(1, D_MODEL),
                          alpha, tm=1024, tk=2048)
        x = x2.reshape(bsz, seq, D_MODEL)
    return x
```

```python
import functools

import numpy as np
import jax
import jax.numpy as jnp
from jax.experimental import pallas as pl
from jax.experimental.pallas import tpu as pltpu

F32 = jnp.float32
BF16 = jnp.bfloat16

D_MODEL = 2048
D_RNN = D_MODEL
RG_HEADS = 16
RG_HEAD_DIM = D_RNN // RG_HEADS
CONV_WIDTH = 4
RG_C = 8.0
D_SSM = D_MODEL // 2
SSM_GROUP = 16
SSM_GROUPS = D_SSM // SSM_GROUP
SSM_STATE = 64
D_FF = 4 * D_MODEL
D_IN = 2 * D_RNN + D_SSM + 2 * D_MODEL
LN_EPS = 1e-5

SUBLANES = 8
BF16_ROWS = 2 * SUBLANES
VMEM_LIMIT = 56 * 1024 * 1024

NSEG = SUBLANES
PB = 512
SEG = PB // NSEG
HALO = (CONV_WIDTH - 1) * SUBLANES
ROW_CHUNK = 256

RG_TILE = 512
S5_TILE = 256
S5_TILE_GROUPS = S5_TILE // SSM_GROUP
S5_TILE_STATES = S5_TILE_GROUPS * SSM_STATE
S5_SCAN_LANES = 256
W_COL_GATE = D_RNN
W_COL_REST = 2 * D_RNN
Z_COLS = D_SSM + 2 * D_MODEL
Z_COL_GA = D_SSM
Z_COL_GB = D_SSM + D_MODEL


def _params(*sem):
    return pltpu.CompilerParams(dimension_semantics=sem, vmem_limit_bytes=VMEM_LIMIT)


def _interleave_matrix():
    p = np.arange(PB)
    src = (p % NSEG) * SEG + p // NSEG
    mat = np.zeros((PB, PB), np.float32)
    mat[p, src] = 1.0
    return mat


def _row_ids(shape):
    return jax.lax.broadcasted_iota(jnp.int32, shape, 0)


def _shift_rows(v, d, fill):
    return jnp.where(_row_ids(v.shape) >= d, pltpu.roll(v, d, 0), fill)


def _interleave_kernel(x_ref, p_ref, o_ref):
    for s in range(x_ref.shape[0] // PB):
        rows = slice(s * PB, (s + 1) * PB)
        o_ref[rows, :] = jnp.dot(p_ref[...], x_ref[rows, :].astype(BF16),
                                 preferred_element_type=F32).astype(BF16)


def _interleave(x, perm, blocks_per_step=2):
    m, k = x.shape
    rows = blocks_per_step * PB
    return pl.pallas_call(
        _interleave_kernel,
        grid=(m // rows,),
        in_specs=[pl.BlockSpec((rows, k), lambda i: (i, 0)),
                  pl.BlockSpec((PB, PB), lambda i: (0, 0))],
        out_specs=pl.BlockSpec((rows, k), lambda i: (i, 0)),
        out_shape=jax.ShapeDtypeStruct((m, k), BF16),
        compiler_params=_params("parallel"),
        name="interleave",
    )(x, perm)


def _in_proj_kernel(x_ref, w_ref, o_ref, wb_ref):
    @pl.when(pl.program_id(1) == 0)
    def _():
        wb_ref[...] = w_ref[...].astype(BF16)

    o_ref[...] = jnp.dot(x_ref[...], wb_ref[...],
                         preferred_element_type=F32).astype(o_ref.dtype)


def _in_proj(xp, w, tm, tn):
    m, k = xp.shape
    col0 = W_COL_REST // tn
    return pl.pallas_call(
        _in_proj_kernel,
        grid=(Z_COLS // tn, m // tm),
        in_specs=[pl.BlockSpec((tm, k), lambda j, i: (i, 0)),
                  pl.BlockSpec((k, tn), lambda j, i: (0, col0 + j))],
        out_specs=pl.BlockSpec((tm, tn), lambda j, i: (i, j)),
        out_shape=jax.ShapeDtypeStruct((m, Z_COLS), BF16),
        scratch_shapes=[pltpu.VMEM((k, tn), BF16)],
        compiler_params=_params("parallel", "arbitrary"),
        name="in_proj",
    )(xp, w)


def _rglru_gates(hh, z_ref, cw_ref, cb_ref, w_ref, ext_ref, xc_ref, pre_ref, gg_ref):
    sl = slice(hh * RG_HEAD_DIM, (hh + 1) * RG_HEAD_DIM)
    gg_ref[:, sl] = jax.nn.gelu(
        z_ref[:, RG_TILE + hh * RG_HEAD_DIM:RG_TILE + (hh + 1) * RG_HEAD_DIM])
    grp = (SUBLANES, RG_HEAD_DIM)
    x = z_ref[:, sl]
    first = _row_ids(grp) == 0
    halos = []
    for k in range(CONV_WIDTH - 1):
        prev_g = ext_ref[PB + k * SUBLANES:PB + (k + 1) * SUBLANES, sl]
        cur_g = x[PB - HALO + k * SUBLANES:PB - HALO + (k + 1) * SUBLANES, :]
        halos.append(jnp.where(first, pltpu.roll(prev_g, 1, 0), pltpu.roll(cur_g, 1, 0)))
    for k in range(CONV_WIDTH - 1):
        ext_ref[k * SUBLANES:(k + 1) * SUBLANES, sl] = halos[k]
    ext_ref[HALO:HALO + PB, sl] = x

    xc = cb_ref[:, sl] + cw_ref[CONV_WIDTH - 1:CONV_WIDTH, sl] * x
    for k in range(CONV_WIDTH - 1):
        xc = xc + cw_ref[k:k + 1, sl] * ext_ref[k * SUBLANES:k * SUBLANES + PB, sl]

    xc_ref[:, sl] = xc
    pre_ref[:, 2 * hh * RG_HEAD_DIM:2 * (hh + 1) * RG_HEAD_DIM] = jnp.dot(
        xc.astype(BF16), w_ref[hh], preferred_element_type=F32)


def _rglru_head(hh, o_ref, row0, cvec, ba_ref, bx_ref, xc_ref, pre_ref, gg_ref,
                a_ref, b_ref, carry_ref):
    sl = slice(hh * RG_HEAD_DIM, (hh + 1) * RG_HEAD_DIM)
    grp = (SUBLANES, RG_HEAD_DIM)
    first = _row_ids(grp) == 0
    xc = xc_ref[:, sl]
    r = jax.nn.sigmoid(pre_ref[:, 2 * hh * RG_HEAD_DIM:(2 * hh + 1) * RG_HEAD_DIM] + ba_ref[:, sl])
    i = jax.nn.sigmoid(pre_ref[:, (2 * hh + 1) * RG_HEAD_DIM:(2 * hh + 2) * RG_HEAD_DIM]
                       + bx_ref[:, sl])
    log_a = cvec[:, sl] * r
    a = jnp.exp(log_a)
    a_ref[:, sl] = a
    one_minus_a2 = -jnp.tanh(log_a) * (1.0 + a * a)
    b_ref[:, sl] = jnp.sqrt(one_minus_a2) * (i * xc)

    h_end, a_end = b_ref[0:SUBLANES, sl], a_ref[0:SUBLANES, sl]
    for tau in range(1, SEG):
        rows = slice(tau * SUBLANES, (tau + 1) * SUBLANES)
        a = a_ref[rows, sl]
        h_end = a * h_end + b_ref[rows, sl]
        a_end = a * a_end
        b_ref[rows, sl] = h_end
        a_ref[rows, sl] = a_end

    for d in (1, 2, 4):
        h_end = a_end * _shift_rows(h_end, d, 0.0) + h_end
        a_end = a_end * _shift_rows(a_end, d, 1.0)
    carry = carry_ref[:, sl]
    h_true = h_end + a_end * carry
    h_in = jnp.where(first, carry, pltpu.roll(h_true, 1, 0))
    carry_ref[:, sl] = h_true[SUBLANES - 1:SUBLANES, :]

    h_in2 = jnp.concatenate([h_in, h_in], axis=0)
    for j in range(PB // BF16_ROWS):
        rows = slice(j * BF16_ROWS, (j + 1) * BF16_ROWS)
        h = b_ref[rows, sl] + a_ref[rows, sl] * h_in2
        o_ref[row0 + j * BF16_ROWS:row0 + (j + 1) * BF16_ROWS, sl] = (
            h * gg_ref[rows, sl]).astype(o_ref.dtype)


def _rglru_kernel(x0_ref, x1_ref, x2_ref, wx_ref, wg_ref, cw_ref, cb_ref, w_ref,
                  ba_ref, bx_ref, lam_ref, o_ref,
                  wb_ref, z_ref, ext_ref, xca_ref, prea_ref, gga_ref, a_ref, b_ref, carry_ref):
    t = pl.program_id(2)
    n_heads = RG_TILE // RG_HEAD_DIM
    bufs = (xca_ref, prea_ref, gga_ref)

    def project(x_ref):
        z_ref[...] = jnp.dot(x_ref[...], wb_ref[...], preferred_element_type=F32)

    def gates():
        for hh in range(n_heads):
            _rglru_gates(hh, z_ref, cw_ref, cb_ref, w_ref, ext_ref, *bufs)

    @pl.when(t == 0)
    def _():
        wb_ref[:, :RG_TILE] = wx_ref[...].astype(BF16)
        wb_ref[:, RG_TILE:] = wg_ref[...].astype(BF16)
        ext_ref[PB:PB + HALO, :] = jnp.zeros((HALO, RG_TILE), F32)
        carry_ref[...] = jnp.zeros_like(carry_ref)
        project(x0_ref)
        gates()

    neg = -lam_ref[...]
    softplus = jnp.maximum(neg, 0.0) + jnp.log1p(jnp.exp(-jnp.abs(neg)))
    cvec = -RG_C * softplus

    def back(row0):
        for hh in range(n_heads):
            _rglru_head(hh, o_ref, row0, cvec, ba_ref, bx_ref, *bufs, a_ref, b_ref, carry_ref)

    project(x1_ref)
    back(0)
    gates()
    project(x2_ref)
    back(PB)
    gates()


def _rglru(xp, w_in, conv_w, conv_b, w_gate, ba, bx, lam, bsz, seq):
    n_t = seq // PB
    assert n_t % 2 == 0
    n_c = D_RNN // RG_TILE
    vec = lambda b, c, t: (0, c)
    return pl.pallas_call(
        _rglru_kernel,
        grid=(bsz, n_c, n_t // 2),
        in_specs=[pl.BlockSpec((PB, D_MODEL), lambda b, c, t: (b * n_t, 0)),
                  pl.BlockSpec((PB, D_MODEL), lambda b, c, t: (b * n_t + 2 * t + 1, 0)),
                  pl.BlockSpec((PB, D_MODEL),
                               lambda b, c, t: (b * n_t + jnp.minimum(2 * t + 2, n_t - 1), 0)),
                  pl.BlockSpec((D_MODEL, RG_TILE), vec),
                  pl.BlockSpec((D_MODEL, RG_TILE), lambda b, c, t: (0, W_COL_GATE // RG_TILE + c)),
                  pl.BlockSpec((CONV_WIDTH, RG_TILE), vec),
                  pl.BlockSpec((1, RG_TILE), vec),
                  pl.BlockSpec((RG_TILE // RG_HEAD_DIM, RG_HEAD_DIM, 2 * RG_HEAD_DIM),
                               lambda b, c, t: (c, 0, 0)),
                  pl.BlockSpec((1, RG_TILE), vec),
                  pl.BlockSpec((1, RG_TILE), vec),
                  pl.BlockSpec((1, RG_TILE), vec)],
        out_specs=pl.BlockSpec((2 * PB, RG_TILE), lambda b, c, t: (b * (n_t // 2) + t, c)),
        out_shape=jax.ShapeDtypeStruct((bsz * seq, D_RNN), BF16),
        scratch_shapes=[pltpu.VMEM((D_MODEL, 2 * RG_TILE), BF16),
                        pltpu.VMEM((PB, 2 * RG_TILE), F32),
                        pltpu.VMEM((HALO + PB, RG_TILE), F32),
                        pltpu.VMEM((PB, RG_TILE), F32),
                        pltpu.VMEM((PB, 2 * RG_TILE), F32),
                        pltpu.VMEM((PB, RG_TILE), F32),
                        pltpu.VMEM((PB, RG_TILE), F32),
                        pltpu.VMEM((PB, RG_TILE), F32),
                        pltpu.VMEM((1, RG_TILE), F32)],
        compiler_params=_params("parallel", "parallel", "arbitrary"),
        name="rglru",
    )(xp, xp, xp, w_in, w_in, conv_w, conv_b, w_gate, ba, bx, lam)


def _s5_discretize_kernel(are_ref, aim_ref, ldt_ref, lbr_ref, lbi_ref, fr_ref, fi_ref):
    dt = jnp.exp(ldt_ref[...])
    lr = jnp.minimum(are_ref[...], -1e-4)
    li = aim_ref[...]
    mag = jnp.exp(lr * dt)
    lbr = mag * jnp.cos(li * dt)
    lbi = mag * jnp.sin(li * dt)
    zr, zi = lbr - 1.0, lbi
    den = lr * lr + li * li
    lbr_ref[...] = lbr
    lbi_ref[...] = lbi
    fr_ref[...] = (zr * lr + zi * li) / den
    fi_ref[...] = (zi * lr - zr * li) / den


def _s5_discretize(a_re, a_im, log_dt):
    shp = jax.ShapeDtypeStruct((SSM_GROUPS, SSM_STATE), F32)
    return pl.pallas_call(
        _s5_discretize_kernel,
        out_shape=(shp, shp, shp, shp),
        name="s5_discretize",
    )(a_re, a_im, log_dt.reshape(SSM_GROUPS, 1))


def _cmul(ar, ai, br, bi):
    return ar * br - ai * bi, ar * bi + ai * br


def _s5_project(u_ref, bmat_ref, xr_ref, xi_ref, cols):
    ns = S5_TILE_STATES
    u = u_ref[...]
    xr_ref[:, cols] = jnp.dot(u, bmat_ref[:, cols], preferred_element_type=F32)
    xi_ref[:, cols] = jnp.dot(u, bmat_ref[:, ns + cols.start:ns + cols.stop],
                              preferred_element_type=F32)


def _s5_scan(cols, xr_ref, xi_ref, hb_ref, lre_ref, lim_ref, tab_re_ref, tab_im_ref,
             pow_re_ref, pow_im_ref, step_re_ref, step_im_ref, cr_ref, ci_ref):
    ns = S5_TILE_STATES
    grp = (SUBLANES, S5_SCAN_LANES)
    first = _row_ids(grp) == 0
    lr = jnp.broadcast_to(lre_ref[:, cols], grp)
    li = jnp.broadcast_to(lim_ref[:, cols], grp)

    er, ei = xr_ref[0:SUBLANES, cols], xi_ref[0:SUBLANES, cols]
    for tau in range(1, SEG):
        rows = slice(tau * SUBLANES, (tau + 1) * SUBLANES)
        nr = (lr * er - li * ei) + xr_ref[rows, cols]
        ni = (lr * ei + li * er) + xi_ref[rows, cols]
        xr_ref[rows, cols] = nr
        xi_ref[rows, cols] = ni
        er, ei = nr, ni

    for k, d in enumerate((1, 2, 4)):
        sr, si = pltpu.roll(er, d, 0), pltpu.roll(ei, d, 0)
        mr, mi = _cmul(step_re_ref[k, :, cols], step_im_ref[k, :, cols], sr, si)
        er, ei = er + mr, ei + mi
    cr, ci = cr_ref[:, cols], ci_ref[:, cols]
    mr, mi = _cmul(pow_re_ref[:, cols], pow_im_ref[:, cols], cr, ci)
    er, ei = er + mr, ei + mi
    in_r = jnp.where(first, cr, pltpu.roll(er, 1, 0))
    in_i = jnp.where(first, ci, pltpu.roll(ei, 1, 0))
    cr_ref[:, cols] = er[SUBLANES - 1:SUBLANES, :]
    ci_ref[:, cols] = ei[SUBLANES - 1:SUBLANES, :]

    in_r2 = jnp.concatenate([in_r, in_r], axis=0).astype(BF16)
    in_i2 = jnp.concatenate([in_i, in_i], axis=0).astype(BF16)
    for i in range(PB // BF16_ROWS):
        rows = slice(i * BF16_ROWS, (i + 1) * BF16_ROWS)
        fr, fi = _cmul(tab_re_ref[rows, cols], tab_im_ref[rows, cols], in_r2, in_i2)
        hb_ref[rows, cols] = xr_ref[rows, cols].astype(BF16) + fr
        hb_ref[rows, ns + cols.start:ns + cols.stop] = xi_ref[rows, cols].astype(BF16) + fi


def _s5_expand_weights(bc_ref, cc_ref, sb_ref, sc_ref, bmat_ref, cmat_ref):
    ns = S5_TILE_STATES

    def block_id(shape, dim, log2_block):
        return jax.lax.shift_right_logical(jax.lax.broadcasted_iota(jnp.int32, shape, dim),
                                           log2_block)

    log2_group = SSM_GROUP.bit_length() - 1
    log2_state = SSM_STATE.bit_length() - 1
    keep_b = block_id((S5_TILE, ns), 0, log2_group) == block_id((S5_TILE, ns), 1, log2_state)
    keep_c = block_id((ns, S5_TILE), 0, log2_state) == block_id((ns, S5_TILE), 1, log2_group)
    for part in range(2):
        b_grp = bc_ref[0, :, part * SSM_STATE:(part + 1) * SSM_STATE].astype(BF16)
        rep = jnp.dot(b_grp, sb_ref[...], preferred_element_type=F32)
        bmat_ref[:, part * ns:(part + 1) * ns] = jnp.where(keep_b, rep, 0.0).astype(BF16)
        c_grp = cc_ref[0, part * ns:(part + 1) * ns, :].astype(BF16)
        rep = jnp.dot(c_grp, sc_ref[...], preferred_element_type=F32)
        cmat_ref[part * ns:(part + 1) * ns, :] = jnp.where(keep_c, rep, 0.0).astype(BF16)


def _s5_kernel(u0_ref, uc_ref, un_ref, bc_ref, cc_ref, sb_ref, sc_ref, lre_ref, lim_ref, d_ref,
               o_ref, bmat_ref, cmat_ref,
               xr_ref, xi_ref, hb_ref, tab_re_ref, tab_im_ref, tabb_re_ref, tabb_im_ref,
               pow_re_ref, pow_im_ref, step_re_ref, step_im_ref, cr_ref, ci_ref):
    t = pl.program_id(2)
    ns = S5_TILE_STATES
    grp = (SUBLANES, ns)
    chunks = [slice(c0, c0 + S5_SCAN_LANES) for c0 in range(0, ns, S5_SCAN_LANES)]

    @pl.when(t == 0)
    def _():
        _s5_expand_weights(bc_ref, cc_ref, sb_ref, sc_ref, bmat_ref, cmat_ref)
        for cols in chunks:
            _s5_project(u0_ref, bmat_ref, xr_ref, xi_ref, cols)
        cr_ref[...] = jnp.zeros_like(cr_ref)
        ci_ref[...] = jnp.zeros_like(ci_ref)
        lr = jnp.broadcast_to(lre_ref[...], grp)
        li = jnp.broadcast_to(lim_ref[...], grp)

        def fill(tau, p):
            pr, pi = p
            rows = pl.ds(pl.multiple_of(tau * SUBLANES, SUBLANES), SUBLANES)
            tab_re_ref[rows, :] = pr
            tab_im_ref[rows, :] = pi
            return _cmul(pr, pi, lr, li)

        jax.lax.fori_loop(0, SEG, fill, (lr, li))
        tabb_re_ref[...] = tab_re_ref[...].astype(BF16)
        tabb_im_ref[...] = tab_im_ref[...].astype(BF16)
        mr = tab_re_ref[PB - SUBLANES:PB, :]
        mi = tab_im_ref[PB - SUBLANES:PB, :]
        row = _row_ids(grp)
        pr, pi = mr, mi
        for r in range(SUBLANES):
            pow_re_ref[r:r + 1, :] = pr[0:1, :]
            pow_im_ref[r:r + 1, :] = pi[0:1, :]
            if r + 1 in (1, 2, 4):
                k = (1, 2, 4).index(r + 1)
                step_re_ref[k] = jnp.where(row >= r + 1, pr, 0.0)
                step_im_ref[k] = jnp.where(row >= r + 1, pi, 0.0)
            pr, pi = _cmul(pr, pi, mr, mi)

    scan_refs = (lre_ref, lim_ref, tabb_re_ref, tabb_im_ref, pow_re_ref, pow_im_ref,
                 step_re_ref, step_im_ref, cr_ref, ci_ref)
    y = d_ref[...] * uc_ref[...].astype(F32)
    for cols in chunks:
        _s5_scan(cols, xr_ref, xi_ref, hb_ref, *scan_refs)
        y = y + jnp.dot(hb_ref[:, cols], cmat_ref[cols, :], preferred_element_type=F32)
        y = y + jnp.dot(hb_ref[:, ns + cols.start:ns + cols.stop],
                        cmat_ref[ns + cols.start:ns + cols.stop, :],
                        preferred_element_type=F32)
        _s5_project(un_ref, bmat_ref, xr_ref, xi_ref, cols)
    o_ref[...] = jax.nn.gelu(y).astype(o_ref.dtype)


def _s5(z, b_grp, c_grp, lam_re, lam_im, dvec, bsz, seq):
    n_t = seq // PB
    n_tiles = D_SSM // S5_TILE
    ns = S5_TILE_STATES
    vec = lambda b, c, t: (0, c)
    tile = lambda b, c, t: (c, 0, 0)
    fixed = lambda b, c, t: (0, 0)
    blk = lambda f: pl.BlockSpec((PB, S5_TILE), lambda b, c, t: (b * n_t + f(t), c))
    rep_b = jnp.asarray(np.tile(np.eye(SSM_STATE), (1, S5_TILE_GROUPS)), BF16)
    rep_c = jnp.asarray(np.tile(np.eye(SSM_GROUP), (1, S5_TILE_GROUPS)), BF16)
    return pl.pallas_call(
        _s5_kernel,
        grid=(bsz, n_tiles, n_t),
        in_specs=[blk(lambda t: 0),
                  blk(lambda t: t),
                  blk(lambda t: jnp.minimum(t + 1, n_t - 1)),
                  pl.BlockSpec((1, S5_TILE, 2 * SSM_STATE), tile),
                  pl.BlockSpec((1, 2 * ns, SSM_GROUP), tile),
                  pl.BlockSpec((SSM_STATE, ns), fixed),
                  pl.BlockSpec((SSM_GROUP, S5_TILE), fixed),
                  pl.BlockSpec((1, ns), vec),
                  pl.BlockSpec((1, ns), vec),
                  pl.BlockSpec((1, S5_TILE), vec)],
        out_specs=blk(lambda t: t),
        out_shape=jax.ShapeDtypeStruct((bsz * seq, D_SSM), BF16),
        scratch_shapes=[pltpu.VMEM((S5_TILE, 2 * ns), BF16),
                        pltpu.VMEM((2 * ns, S5_TILE), BF16),
                        pltpu.VMEM((PB, ns), F32),
                        pltpu.VMEM((PB, ns), F32),
                        pltpu.VMEM((PB, 2 * ns), BF16),
                        pltpu.VMEM((PB, ns), F32),
                        pltpu.VMEM((PB, ns), F32),
                        pltpu.VMEM((PB, ns), BF16),
                        pltpu.VMEM((PB, ns), BF16),
                        pltpu.VMEM((SUBLANES, ns), F32),
                        pltpu.VMEM((SUBLANES, ns), F32),
                        pltpu.VMEM((3, SUBLANES, ns), F32),
                        pltpu.VMEM((3, SUBLANES, ns), F32),
                        pltpu.VMEM((1, ns), F32),
                        pltpu.VMEM((1, ns), F32)],
        compiler_params=_params("parallel", "parallel", "arbitrary"),
        name="s5",
    )(z, z, z, b_grp, c_grp, rep_b, rep_c, lam_re, lam_im, dvec)


def _mix_kernel(hg_ref, y_ref, ga_ref, gb_ref, wa_ref, gw_ref, gv_ref, o_ref,
                wab_ref, gwb_ref, gvb_ref):
    @pl.when(pl.program_id(1) == 0)
    def _():
        wab_ref[...] = wa_ref[...].astype(BF16)
        gwb_ref[...] = gw_ref[...].astype(BF16)
        gvb_ref[...] = gv_ref[...].astype(BF16)

    for c in range(o_ref.shape[0] // ROW_CHUNK):
        rows = slice(c * ROW_CHUNK, (c + 1) * ROW_CHUNK)
        y_a = jnp.dot(hg_ref[rows, :], wab_ref[...], preferred_element_type=F32)
        y = y_ref[rows, :]
        y_b = (jnp.dot(y, gwb_ref[...], preferred_element_type=F32)
               * jax.nn.sigmoid(jnp.dot(y, gvb_ref[...], preferred_element_type=F32)))
        mix = (jax.nn.sigmoid(ga_ref[rows, :].astype(F32)) * y_a
               + jax.nn.sigmoid(gb_ref[rows, :].astype(F32)) * y_b)
        o_ref[rows, :] = mix.astype(o_ref.dtype)


def _mix(hg, y, z, w_a, glu_w, glu_v, tm, tn):
    m = hg.shape[0]
    return pl.pallas_call(
        _mix_kernel,
        grid=(D_MODEL // tn, m // tm),
        in_specs=[pl.BlockSpec((tm, D_RNN), lambda j, i: (i, 0)),
                  pl.BlockSpec((tm, D_SSM), lambda j, i: (i, 0)),
                  pl.BlockSpec((tm, tn), lambda j, i: (i, Z_COL_GA // tn + j)),
                  pl.BlockSpec((tm, tn), lambda j, i: (i, Z_COL_GB // tn + j)),
                  pl.BlockSpec((D_RNN, tn), lambda j, i: (0, j)),
                  pl.BlockSpec((D_SSM, tn), lambda j, i: (0, j)),
                  pl.BlockSpec((D_SSM, tn), lambda j, i: (0, j))],
        out_specs=pl.BlockSpec((tm, tn), lambda j, i: (i, j)),
        out_shape=jax.ShapeDtypeStruct((m, D_MODEL), BF16),
        scratch_shapes=[pltpu.VMEM((D_RNN, tn), BF16),
                        pltpu.VMEM((D_SSM, tn), BF16),
                        pltpu.VMEM((D_SSM, tn), BF16)],
        compiler_params=_params("parallel", "arbitrary"),
        name="mix",
    )(hg, y, z, z, w_a, glu_w, glu_v)


def _layernorm(v, g, b):
    mu = jnp.mean(v, axis=-1, keepdims=True)
    c = v - mu
    var = jnp.mean(c * c, axis=-1, keepdims=True)
    return c * jax.lax.rsqrt(var + LN_EPS) * g + b


def _outproj_ln_kernel(mix_ref, pt_ref, x_ref, w_ref, g_ref, b_ref, o_ref, ob_ref, *, alpha):
    for c in range(PB // ROW_CHUNK):
        rows = slice(c * ROW_CHUNK, (c + 1) * ROW_CHUNK)
        mix = jnp.dot(pt_ref[rows, :], mix_ref[...], preferred_element_type=F32).astype(BF16)
        v = alpha * x_ref[rows, :] + jnp.dot(mix, w_ref[...], preferred_element_type=F32)
        out = _layernorm(v, g_ref[...], b_ref[...])
        o_ref[rows, :] = out
        ob_ref[rows, :] = out.astype(BF16)


def _outproj_ln(mix, perm_t, x, w_out, g, b, alpha):
    m = mix.shape[0]
    row = lambda i: (i, 0)
    fixed = lambda i: (0, 0)
    return pl.pallas_call(
        functools.partial(_outproj_ln_kernel, alpha=alpha),
        grid=(m // PB,),
        in_specs=[pl.BlockSpec((PB, D_MODEL), row),
                  pl.BlockSpec((PB, PB), fixed),
                  pl.BlockSpec((PB, D_MODEL), row),
                  pl.BlockSpec((D_MODEL, D_MODEL), fixed),
                  pl.BlockSpec((1, D_MODEL), fixed),
                  pl.BlockSpec((1, D_MODEL), fixed)],
        out_specs=(pl.BlockSpec((PB, D_MODEL), row), pl.BlockSpec((PB, D_MODEL), row)),
        out_shape=(jax.ShapeDtypeStruct((m, D_MODEL), F32),
                   jax.ShapeDtypeStruct((m, D_MODEL), BF16)),
        compiler_params=_params("parallel"),
        name="outproj_ln",
    )(mix, perm_t, x, w_out, g, b)


def _mlp_up_kernel(x_ref, w_ref, b_ref, o_ref, wb_ref):
    @pl.when(pl.program_id(1) == 0)
    def _():
        wb_ref[...] = w_ref[...].astype(BF16)

    v = jnp.dot(x_ref[...], wb_ref[...], preferred_element_type=F32) + b_ref[...]
    v = jnp.maximum(v, 0.0)
    o_ref[...] = (v * v).astype(o_ref.dtype)


def _mlp_up(xb, w_up, b_up, tm, tn):
    m = xb.shape[0]
    return pl.pallas_call(
        _mlp_up_kernel,
        grid=(D_FF // tn, m // tm),
        in_specs=[pl.BlockSpec((tm, D_MODEL), lambda j, i: (i, 0)),
                  pl.BlockSpec((D_MODEL, tn), lambda j, i: (0, j)),
                  pl.BlockSpec((1, tn), lambda j, i: (0, j))],
        out_specs=pl.BlockSpec((tm, tn), lambda j, i: (i, j)),
        out_shape=jax.ShapeDtypeStruct((m, D_FF), BF16),
        scratch_shapes=[pltpu.VMEM((D_MODEL, tn), BF16)],
        compiler_params=_params("parallel", "arbitrary"),
        name="mlp_up",
    )(xb, w_up, b_up)


def _mlp_down_ln_kernel(a_ref, w_ref, x_ref, bd_ref, g_ref, b_ref, o_ref, *, alpha):
    k = pl.program_id(1)
    last = pl.num_programs(1) - 1
    slab = x_ref.shape[0]
    slab_rows = pl.ds(pl.multiple_of(k * slab, slab), slab)

    def part():
        return jnp.dot(a_ref[...], w_ref[...], preferred_element_type=F32)

    @pl.when(k == 0)
    def _():
        o_ref[...] = part()
        o_ref[slab_rows, :] += alpha * x_ref[...]

    @pl.when(jnp.logical_and(k > 0, k < last))
    def _():
        o_ref[...] += part()
        o_ref[slab_rows, :] += alpha * x_ref[...]

    @pl.when(k == last)
    def _():
        o_ref[slab_rows, :] += alpha * x_ref[...]
        for c in range(o_ref.shape[0] // ROW_CHUNK):
            rows = slice(c * ROW_CHUNK, (c + 1) * ROW_CHUNK)
            v = (o_ref[rows, :] + jnp.dot(a_ref[rows, :], w_ref[...],
                                          preferred_element_type=F32)) + bd_ref[...]
            o_ref[rows, :] = _layernorm(v, g_ref[...], b_ref[...])


def _mlp_down_ln(a, w_down, x1, b_down, g, b, alpha, tm, tk):
    m = a.shape[0]
    n_k = D_FF // tk
    fixed = lambda i, k: (0, 0)
    return pl.pallas_call(
        functools.partial(_mlp_down_ln_kernel, alpha=alpha),
        grid=(m // tm, n_k),
        in_specs=[pl.BlockSpec((tm, tk), lambda i, k: (i, k)),
                  pl.BlockSpec((tk, D_MODEL), lambda i, k: (k, 0)),
                  pl.BlockSpec((tm // n_k, D_MODEL), lambda i, k: (i * n_k + k, 0)),
                  pl.BlockSpec((1, D_MODEL), fixed),
                  pl.BlockSpec((1, D_MODEL), fixed),
                  pl.BlockSpec((1, D_MODEL), fixed)],
        out_specs=pl.BlockSpec((tm, D_MODEL), lambda i, k: (i, 0)),
        out_shape=jax.ShapeDtypeStruct((m, D_MODEL), F32),
        compiler_params=_params("parallel", "arbitrary"),
        name="mlp_down_ln",
    )(a, w_down, x1, b_down, g, b)


def _s5_group_matrices(fr, fi, b_re, b_im, c_re, c_im):
    bbr = fr[..., None] * b_re - fi[..., None] * b_im
    bbi = fr[..., None] * b_im + fi[..., None] * b_re
    n_tiles = D_SSM // S5_TILE
    b_grp = jnp.concatenate([bbr.transpose(0, 2, 1), bbi.transpose(0, 2, 1)], axis=-1)
    b_grp = b_grp.reshape(n_tiles, S5_TILE, 2 * SSM_STATE)

    def per_tile(c):
        return c.transpose(0, 2, 1).reshape(n_tiles, S5_TILE_STATES, SSM_GROUP)

    c_grp = jnp.concatenate([per_tile(c_re), per_tile(-c_im)], axis=1)
    return b_grp, c_grp


def kernel(x, w_in, conv_w, conv_b, rg_wa, rg_ba, rg_wx, rg_bx, rg_lambda, w_a_out, ssm_a_re, ssm_a_im, ssm_log_dt, ssm_b_re, ssm_b_im, ssm_c_re, ssm_c_im, ssm_d, glu_w, glu_v, w_out, ln1_g, ln1_b, mlp_w_up, mlp_b_up, mlp_w_down, mlp_b_down, ln2_g, ln2_b):
    bsz, seq, _ = x.shape
    assert seq % PB == 0
    m = bsz * seq
    depth = w_in.shape[0]
    alpha = (2.0 * depth) ** 0.25
    perm = _interleave_matrix()
    perm_fwd = jnp.asarray(perm, BF16)
    perm_bwd = jnp.asarray(perm.T, BF16)
    for l in range(depth):
        xf = x.reshape(m, D_MODEL)
        xp = _interleave(xf, perm_fwd)
        z = _in_proj(xp, w_in[l], tm=2048, tn=1024)

        w_gate = jnp.concatenate([rg_wa[l], rg_wx[l]], axis=-1).astype(BF16)
        hg = _rglru(xp, w_in[l], conv_w[l], conv_b[l].reshape(1, D_RNN), w_gate,
                    rg_ba[l].reshape(1, D_RNN), rg_bx[l].reshape(1, D_RNN),
                    rg_lambda[l].reshape(1, D_RNN), bsz, seq)

        lbr, lbi, fr, fi = _s5_discretize(ssm_a_re[l], ssm_a_im[l], ssm_log_dt[l])
        b_grp, c_grp = _s5_group_matrices(fr, fi, ssm_b_re[l], ssm_b_im[l],
                                          ssm_c_re[l], ssm_c_im[l])
        y = _s5(z, b_grp, c_grp, lbr.reshape(1, -1), lbi.reshape(1, -1),
                ssm_d[l].reshape(1, D_SSM), bsz, seq)

        mix = _mix(hg, y, z, w_a_out[l], glu_w[l], glu_v[l], tm=1024, tn=512)
        x1, x1b = _outproj_ln(mix, perm_bwd, xf, w_out[l].astype(BF16),
                              ln1_g[l].reshape(1, D_MODEL), ln1_b[l].reshape(1, D_MODEL),
                              alpha)
        a = _mlp_up(x1b, mlp_w_up[l], mlp_b_up[l].reshape(1, D_FF), tm=2048, tn=1024)
        x2 = _mlp_down_ln(a, mlp_w_down[l].astype(BF16), x1,
                          mlp_b_down[l].reshape(1, D_MODEL),
                          ln2_g[l].reshape(1, D_MODEL), ln2_b[l].reshape(1, D_MODEL),
                          alpha, tm=1024, tk=2048)
        x = x2.reshape(bsz, seq, D_MODEL)
    return x
```

```python
import functools

import numpy as np
import jax
import jax.numpy as jnp
from jax.experimental import pallas as pl
from jax.experimental.pallas import tpu as pltpu

F32 = jnp.float32
BF16 = jnp.bfloat16

D_MODEL = 2048
D_RNN = D_MODEL
RG_HEADS = 16
RG_HEAD_DIM = D_RNN // RG_HEADS
CONV_WIDTH = 4
RG_C = 8.0
D_SSM = D_MODEL // 2
SSM_GROUP = 16
SSM_GROUPS = D_SSM // SSM_GROUP
SSM_STATE = 64
D_FF = 4 * D_MODEL
D_IN = 2 * D_RNN + D_SSM + 2 * D_MODEL
LN_EPS = 1e-5

SUBLANES = 8
BF16_ROWS = 2 * SUBLANES
VMEM_LIMIT = 56 * 1024 * 1024

NSEG = SUBLANES
PB = 512
SEG = PB // NSEG
HALO = (CONV_WIDTH - 1) * SUBLANES
ROW_CHUNK = 256

RG_TILE = 512
S5_TILE = 256
S5_TILE_GROUPS = S5_TILE // SSM_GROUP
S5_TILE_STATES = S5_TILE_GROUPS * SSM_STATE
S5_SCAN_LANES = 256
W_COL_GATE = D_RNN
W_COL_REST = 2 * D_RNN
Z_COLS = D_SSM + 2 * D_MODEL
Z_COL_GA = D_SSM
Z_COL_GB = D_SSM + D_MODEL


def _params(*sem):
    return pltpu.CompilerParams(dimension_semantics=sem, vmem_limit_bytes=VMEM_LIMIT)


def _interleave_matrix():
    p = np.arange(PB)
    src = (p % NSEG) * SEG + p // NSEG
    mat = np.zeros((PB, PB), np.float32)
    mat[p, src] = 1.0
    return mat


def _row_ids(shape):
    return jax.lax.broadcasted_iota(jnp.int32, shape, 0)


def _shift_rows(v, d, fill):
    return jnp.where(_row_ids(v.shape) >= d, pltpu.roll(v, d, 0), fill)


def _interleave_kernel(x_ref, p_ref, o_ref):
    for s in range(x_ref.shape[0] // PB):
        rows = slice(s * PB, (s + 1) * PB)
        o_ref[rows, :] = jnp.dot(p_ref[...], x_ref[rows, :].astype(BF16),
                                 preferred_element_type=F32).astype(BF16)


def _interleave(x, perm, blocks_per_step=2):
    m, k = x.shape
    rows = blocks_per_step * PB
    return pl.pallas_call(
        _interleave_kernel,
        grid=(m // rows,),
        in_specs=[pl.BlockSpec((rows, k), lambda i: (i, 0)),
                  pl.BlockSpec((PB, PB), lambda i: (0, 0))],
        out_specs=pl.BlockSpec((rows, k), lambda i: (i, 0)),
        out_shape=jax.ShapeDtypeStruct((m, k), BF16),
        compiler_params=_params("parallel"),
        name="interleave",
    )(x, perm)


def _in_proj_kernel(x_ref, w_ref, o_ref, wb_ref):
    @pl.when(pl.program_id(1) == 0)
    def _():
        wb_ref[...] = w_ref[...].astype(BF16)

    o_ref[...] = jnp.dot(x_ref[...], wb_ref[...],
                         preferred_element_type=F32).astype(o_ref.dtype)


def _in_proj(xp, w, tm, tn):
    m, k = xp.shape
    col0 = W_COL_REST // tn
    return pl.pallas_call(
        _in_proj_kernel,
        grid=(Z_COLS // tn, m // tm),
        in_specs=[pl.BlockSpec((tm, k), lambda j, i: (i, 0)),
                  pl.BlockSpec((k, tn), lambda j, i: (0, col0 + j))],
        out_specs=pl.BlockSpec((tm, tn), lambda j, i: (i, j)),
        out_shape=jax.ShapeDtypeStruct((m, Z_COLS), BF16),
        scratch_shapes=[pltpu.VMEM((k, tn), BF16)],
        compiler_params=_params("parallel", "arbitrary"),
        name="in_proj",
    )(xp, w)


def _rglru_gates(hh, z_ref, cw_ref, cb_ref, w_ref, ext_ref, xc_ref, pre_ref, gg_ref):
    sl = slice(hh * RG_HEAD_DIM, (hh + 1) * RG_HEAD_DIM)
    gg_ref[:, sl] = jax.nn.gelu(
        z_ref[:, RG_TILE + hh * RG_HEAD_DIM:RG_TILE + (hh + 1) * RG_HEAD_DIM])
    grp = (SUBLANES, RG_HEAD_DIM)
    x = z_ref[:, sl]
    first = _row_ids(grp) == 0
    halos = []
    for k in range(CONV_WIDTH - 1):
        prev_g = ext_ref[PB + k * SUBLANES:PB + (k + 1) * SUBLANES, sl]
        cur_g = x[PB - HALO + k * SUBLANES:PB - HALO + (k + 1) * SUBLANES, :]
        halos.append(jnp.where(first, pltpu.roll(prev_g, 1, 0), pltpu.roll(cur_g, 1, 0)))
    for k in range(CONV_WIDTH - 1):
        ext_ref[k * SUBLANES:(k + 1) * SUBLANES, sl] = halos[k]
    ext_ref[HALO:HALO + PB, sl] = x

    xc = cb_ref[:, sl] + cw_ref[CONV_WIDTH - 1:CONV_WIDTH, sl] * x
    for k in range(CONV_WIDTH - 1):
        xc = xc + cw_ref[k:k + 1, sl] * ext_ref[k * SUBLANES:k * SUBLANES + PB, sl]

    xc_ref[:, sl] = xc
    pre_ref[:, 2 * hh * RG_HEAD_DIM:2 * (hh + 1) * RG_HEAD_DIM] = jnp.dot(
        xc.astype(BF16), w_ref[hh], preferred_element_type=F32)


def _rglru_head(hh, o_ref, row0, cvec, ba_ref, bx_ref, xc_ref, pre_ref, gg_ref,
                a_ref, b_ref, carry_ref):
    sl = slice(hh * RG_HEAD_DIM, (hh + 1) * RG_HEAD_DIM)
    grp = (SUBLANES, RG_HEAD_DIM)
    first = _row_ids(grp) == 0
    xc = xc_ref[:, sl]
    r = jax.nn.sigmoid(pre_ref[:, 2 * hh * RG_HEAD_DIM:(2 * hh + 1) * RG_HEAD_DIM] + ba_ref[:, sl])
    i = jax.nn.sigmoid(pre_ref[:, (2 * hh + 1) * RG_HEAD_DIM:(2 * hh + 2) * RG_HEAD_DIM]
                       + bx_ref[:, sl])
    log_a = cvec[:, sl] * r
    a = jnp.exp(log_a)
    a_ref[:, sl] = a
    one_minus_a2 = -jnp.tanh(log_a) * (1.0 + a * a)
    b_ref[:, sl] = jnp.sqrt(one_minus_a2) * (i * xc)

    h_end, a_end = b_ref[0:SUBLANES, sl], a_ref[0:SUBLANES, sl]
    for tau in range(1, SEG):
        rows = slice(tau * SUBLANES, (tau + 1) * SUBLANES)
        a = a_ref[rows, sl]
        h_end = a * h_end + b_ref[rows, sl]
        a_end = a * a_end
        b_ref[rows, sl] = h_end
        a_ref[rows, sl] = a_end

    for d in (1, 2, 4):
        h_end = a_end * _shift_rows(h_end, d, 0.0) + h_end
        a_end = a_end * _shift_rows(a_end, d, 1.0)
    carry = carry_ref[:, sl]
    h_true = h_end + a_end * carry
    h_in = jnp.where(first, carry, pltpu.roll(h_true, 1, 0))
    carry_ref[:, sl] = h_true[SUBLANES - 1:SUBLANES, :]

    h_in2 = jnp.concatenate([h_in, h_in], axis=0)
    for j in range(PB // BF16_ROWS):
        rows = slice(j * BF16_ROWS, (j + 1) * BF16_ROWS)
        h = b_ref[rows, sl] + a_ref[rows, sl] * h_in2
        o_ref[row0 + j * BF16_ROWS:row0 + (j + 1) * BF16_ROWS, sl] = (
            h * gg_ref[rows, sl]).astype(o_ref.dtype)


def _rglru_kernel(x0_ref, x1_ref, x2_ref, wx_ref, wg_ref, cw_ref, cb_ref, w_ref,
                  ba_ref, bx_ref, lam_ref, o_ref,
                  wb_ref, z_ref, ext_ref, xca_ref, prea_ref, gga_ref, a_ref, b_ref, carry_ref):
    t = pl.program_id(2)
    n_heads = RG_TILE // RG_HEAD_DIM
    bufs = (xca_ref, prea_ref, gga_ref)

    def project(x_ref):
        z_ref[...] = jnp.dot(x_ref[...], wb_ref[...], preferred_element_type=F32)

    def gates():
        for hh in range(n_heads):
            _rglru_gates(hh, z_ref, cw_ref, cb_ref, w_ref, ext_ref, *bufs)

    @pl.when(t == 0)
    def _():
        wb_ref[:, :RG_TILE] = wx_ref[...].astype(BF16)
        wb_ref[:, RG_TILE:] = wg_ref[...].astype(BF16)
        ext_ref[PB:PB + HALO, :] = jnp.zeros((HALO, RG_TILE), F32)
        carry_ref[...] = jnp.zeros_like(carry_ref)
        project(x0_ref)
        gates()

    neg = -lam_ref[...]
    softplus = jnp.maximum(neg, 0.0) + jnp.log1p(jnp.exp(-jnp.abs(neg)))
    cvec = -RG_C * softplus

    def back(row0):
        for hh in range(n_heads):
            _rglru_head(hh, o_ref, row0, cvec, ba_ref, bx_ref, *bufs, a_ref, b_ref, carry_ref)

    project(x1_ref)
    back(0)
    gates()
    project(x2_ref)
    back(PB)
    gates()


def _rglru(xp, w_in, conv_w, conv_b, w_gate, ba, bx, lam, bsz, seq):
    n_t = seq // PB
    assert n_t % 2 == 0
    n_c = D_RNN // RG_TILE
    vec = lambda b, c, t: (0, c)
    return pl.pallas_call(
        _rglru_kernel,
        grid=(bsz, n_c, n_t // 2),
        in_specs=[pl.BlockSpec((PB, D_MODEL), lambda b, c, t: (b * n_t, 0)),
                  pl.BlockSpec((PB, D_MODEL), lambda b, c, t: (b * n_t + 2 * t + 1, 0)),
                  pl.BlockSpec((PB, D_MODEL),
                               lambda b, c, t: (b * n_t + jnp.minimum(2 * t + 2, n_t - 1), 0)),
                  pl.BlockSpec((D_MODEL, RG_TILE), vec),
                  pl.BlockSpec((D_MODEL, RG_TILE), lambda b, c, t: (0, W_COL_GATE // RG_TILE + c)),
                  pl.BlockSpec((CONV_WIDTH, RG_TILE), vec),
                  pl.BlockSpec((1, RG_TILE), vec),
                  pl.BlockSpec((RG_TILE // RG_HEAD_DIM, RG_HEAD_DIM, 2 * RG_HEAD_DIM),
                               lambda b, c, t: (c, 0, 0)),
                  pl.BlockSpec((1, RG_TILE), vec),
                  pl.BlockSpec((1, RG_TILE), vec),
                  pl.BlockSpec((1, RG_TILE), vec)],
        out_specs=pl.BlockSpec((2 * PB, RG_TILE), lambda b, c, t: (b * (n_t // 2) + t, c)),
        out_shape=jax.ShapeDtypeStruct((bsz * seq, D_RNN), BF16),
        scratch_shapes=[pltpu.VMEM((D_MODEL, 2 * RG_TILE), BF16),
                        pltpu.VMEM((PB, 2 * RG_TILE), F32),
                        pltpu.VMEM((HALO + PB, RG_TILE), F32),
                        pltpu.VMEM((PB, RG_TILE), F32),
                        pltpu.VMEM((PB, 2 * RG_TILE), F32),
                        pltpu.VMEM((PB, RG_TILE), F32),
                        pltpu.VMEM((PB, RG_TILE), F32),
                        pltpu.VMEM((PB, RG_TILE), F32),
                        pltpu.VMEM((1, RG_TILE), F32)],
        compiler_params=_params("parallel", "parallel", "arbitrary"),
        name="rglru",
    )(xp, xp, xp, w_in, w_in, conv_w, conv_b, w_gate, ba, bx, lam)


def _s5_discretize_kernel(are_ref, aim_ref, ldt_ref, lbr_ref, lbi_ref, fr_ref, fi_ref):
    dt = jnp.exp(ldt_ref[...])
    lr = jnp.minimum(are_ref[...], -1e-4)
    li = aim_ref[...]
    mag = jnp.exp(lr * dt)
    lbr = mag * jnp.cos(li * dt)
    lbi = mag * jnp.sin(li * dt)
    zr, zi = lbr - 1.0, lbi
    den = lr * lr + li * li
    lbr_ref[...] = lbr
    lbi_ref[...] = lbi
    fr_ref[...] = (zr * lr + zi * li) / den
    fi_ref[...] = (zi * lr - zr * li) / den


def _s5_discretize(a_re, a_im, log_dt):
    shp = jax.ShapeDtypeStruct((SSM_GROUPS, SSM_STATE), F32)
    return pl.pallas_call(
        _s5_discretize_kernel,
        out_shape=(shp, shp, shp, shp),
        name="s5_discretize",
    )(a_re, a_im, log_dt.reshape(SSM_GROUPS, 1))


def _cmul(ar, ai, br, bi):
    return ar * br - ai * bi, ar * bi + ai * br


def _s5_project(u_ref, bmat_ref, xr_ref, xi_ref, cols):
    ns = S5_TILE_STATES
    u = u_ref[...]
    xr_ref[:, cols] = jnp.dot(u, bmat_ref[:, cols], preferred_element_type=F32)
    xi_ref[:, cols] = jnp.dot(u, bmat_ref[:, ns + cols.start:ns + cols.stop],
                              preferred_element_type=F32)


def _s5_scan(cols, xr_ref, xi_ref, hb_ref, lre_ref, lim_ref, tab_re_ref, tab_im_ref,
             pow_re_ref, pow_im_ref, step_re_ref, step_im_ref, cr_ref, ci_ref):
    ns = S5_TILE_STATES
    grp = (SUBLANES, S5_SCAN_LANES)
    first = _row_ids(grp) == 0
    lr = jnp.broadcast_to(lre_ref[:, cols], grp)
    li = jnp.broadcast_to(lim_ref[:, cols], grp)

    er, ei = xr_ref[0:SUBLANES, cols], xi_ref[0:SUBLANES, cols]
    for tau in range(1, SEG):
        rows = slice(tau * SUBLANES, (tau + 1) * SUBLANES)
        nr = (lr * er - li * ei) + xr_ref[rows, cols]
        ni = (lr * ei + li * er) + xi_ref[rows, cols]
        xr_ref[rows, cols] = nr
        xi_ref[rows, cols] = ni
        er, ei = nr, ni

    for k, d in enumerate((1, 2, 4)):
        sr, si = pltpu.roll(er, d, 0), pltpu.roll(ei, d, 0)
        mr, mi = _cmul(step_re_ref[k, :, cols], step_im_ref[k, :, cols], sr, si)
        er, ei = er + mr, ei + mi
    cr, ci = cr_ref[:, cols], ci_ref[:, cols]
    mr, mi = _cmul(pow_re_ref[:, cols], pow_im_ref[:, cols], cr, ci)
    er, ei = er + mr, ei + mi
    in_r = jnp.where(first, cr, pltpu.roll(er, 1, 0))
    in_i = jnp.where(first, ci, pltpu.roll(ei, 1, 0))
    cr_ref[:, cols] = er[SUBLANES - 1:SUBLANES, :]
    ci_ref[:, cols] = ei[SUBLANES - 1:SUBLANES, :]

    in_r2 = jnp.concatenate([in_r, in_r], axis=0).astype(BF16)
    in_i2 = jnp.concatenate([in_i, in_i], axis=0).astype(BF16)
    for i in range(PB // BF16_ROWS):
        rows = slice(i * BF16_ROWS, (i + 1) * BF16_ROWS)
        fr, fi = _cmul(tab_re_ref[rows, cols], tab_im_ref[rows, cols], in_r2, in_i2)
        hb_ref[rows, cols] = xr_ref[rows, cols].astype(BF16) + fr
        hb_ref[rows, ns + cols.start:ns + cols.stop] = xi_ref[rows, cols].astype(BF16) + fi


def _s5_expand_weights(bc_ref, cc_ref, sb_ref, sc_ref, bmat_ref, cmat_ref):
    ns = S5_TILE_STATES

    def block_id(shape, dim, log2_block):
        return jax.lax.shift_right_logical(jax.lax.broadcasted_iota(jnp.int32, shape, dim),
                                           log2_block)

    log2_group = SSM_GROUP.bit_length() - 1
    log2_state = SSM_STATE.bit_length() - 1
    keep_b = block_id((S5_TILE, ns), 0, log2_group) == block_id((S5_TILE, ns), 1, log2_state)
    keep_c = block_id((ns, S5_TILE), 0, log2_state) == block_id((ns, S5_TILE), 1, log2_group)
    for part in range(2):
        b_grp = bc_ref[0, :, part * SSM_STATE:(part + 1) * SSM_STATE].astype(BF16)
        rep = jnp.dot(b_grp, sb_ref[...], preferred_element_type=F32)
        bmat_ref[:, part * ns:(part + 1) * ns] = jnp.where(keep_b, rep, 0.0).astype(BF16)
        c_grp = cc_ref[0, part * ns:(part + 1) * ns, :].astype(BF16)
        rep = jnp.dot(c_grp, sc_ref[...], preferred_element_type=F32)
        cmat_ref[part * ns:(part + 1) * ns, :] = jnp.where(keep_c, rep, 0.0).astype(BF16)


def _s5_kernel(u0_ref, uc_ref, un_ref, bc_ref, cc_ref, sb_ref, sc_ref, lre_ref, lim_ref, d_ref,
               wdown_ref, wout_ref, o_ref, wdown_bf_ref, wout_bf_ref, bmat_ref, cmat_ref,
               xr_ref, xi_ref, hb_ref, tab_re_ref, tab_im_ref, tabb_re_ref, tabb_im_ref,
               pow_re_ref, pow_im_ref, step_re_ref, step_im_ref, cr_ref, ci_ref):
    t = pl.program_id(2)
    ns = S5_TILE_STATES
    grp = (SUBLANES, ns)
    chunks = [slice(c0, c0 + S5_SCAN_LANES) for c0 in range(0, ns, S5_SCAN_LANES)]

    @pl.when(t == 0)
    def _():
        _s5_expand_weights(bc_ref, cc_ref, sb_ref, sc_ref, bmat_ref, cmat_ref)
        for cols in chunks:
            _s5_project(u0_ref, bmat_ref, xr_ref, xi_ref, cols)
        cr_ref[...] = jnp.zeros_like(cr_ref)
        ci_ref[...] = jnp.zeros_like(ci_ref)
        lr = jnp.broadcast_to(lre_ref[...], grp)
        li = jnp.broadcast_to(lim_ref[...], grp)

        def fill(tau, p):
            pr, pi = p
            rows = pl.ds(pl.multiple_of(tau * SUBLANES, SUBLANES), SUBLANES)
            tab_re_ref[rows, :] = pr
            tab_im_ref[rows, :] = pi
            return _cmul(pr, pi, lr, li)

        jax.lax.fori_loop(0, SEG, fill, (lr, li))
        tabb_re_ref[...] = tab_re_ref[...].astype(BF16)
        tabb_im_ref[...] = tab_im_ref[...].astype(BF16)
        mr = tab_re_ref[PB - SUBLANES:PB, :]
        mi = tab_im_ref[PB - SUBLANES:PB, :]
        row = _row_ids(grp)
        pr, pi = mr, mi
        for r in range(SUBLANES):
            pow_re_ref[r:r + 1, :] = pr[0:1, :]
            pow_im_ref[r:r + 1, :] = pi[0:1, :]
            if r + 1 in (1, 2, 4):
                k = (1, 2, 4).index(r + 1)
                step_re_ref[k] = jnp.where(row >= r + 1, pr, 0.0)
                step_im_ref[k] = jnp.where(row >= r + 1, pi, 0.0)
            pr, pi = _cmul(pr, pi, mr, mi)

    scan_refs = (lre_ref, lim_ref, tabb_re_ref, tabb_im_ref, pow_re_ref, pow_im_ref,
                 step_re_ref, step_im_ref, cr_ref, ci_ref)
    y = d_ref[...] * uc_ref[...].astype(F32)
    for cols in chunks:
        _s5_scan(cols, xr_ref, xi_ref, hb_ref, *scan_refs)
        y = y + jnp.dot(hb_ref[:, cols], cmat_ref[cols, :], preferred_element_type=F32)
        y = y + jnp.dot(hb_ref[:, ns + cols.start:ns + cols.stop],
                        cmat_ref[ns + cols.start:ns + cols.stop, :],
                        preferred_element_type=F32)
        _s5_project(un_ref, bmat_ref, xr_ref, xi_ref, cols)
    o_ref[...] = jax.nn.gelu(y).astype(o_ref.dtype)

    wdown_bf_ref[...] = wdown_ref[...].astype(BF16)
    wout_bf_ref[...] = wout_ref[...].astype(BF16)


def _s5(z, b_grp, c_grp, lam_re, lam_im, dvec, w_down, w_out, bsz, seq):
    n_t = seq // PB
    n_tiles = D_SSM // S5_TILE
    ns = S5_TILE_STATES
    vec = lambda b, c, t: (0, c)
    tile = lambda b, c, t: (c, 0, 0)
    fixed = lambda b, c, t: (0, 0)
    blk = lambda f: pl.BlockSpec((PB, S5_TILE), lambda b, c, t: (b * n_t + f(t), c))
    rep_b = jnp.asarray(np.tile(np.eye(SSM_STATE), (1, S5_TILE_GROUPS)), BF16)
    rep_c = jnp.asarray(np.tile(np.eye(SSM_GROUP), (1, S5_TILE_GROUPS)), BF16)
    slabs = n_tiles * n_t
    slab = lambda rows: pl.BlockSpec((rows // slabs, D_MODEL), lambda b, c, t: (c * n_t + t, 0))
    return pl.pallas_call(
        _s5_kernel,
        grid=(bsz, n_tiles, n_t),
        in_specs=[blk(lambda t: 0),
                  blk(lambda t: t),
                  blk(lambda t: jnp.minimum(t + 1, n_t - 1)),
                  pl.BlockSpec((1, S5_TILE, 2 * SSM_STATE), tile),
                  pl.BlockSpec((1, 2 * ns, SSM_GROUP), tile),
                  pl.BlockSpec((SSM_STATE, ns), fixed),
                  pl.BlockSpec((SSM_GROUP, S5_TILE), fixed),
                  pl.BlockSpec((1, ns), vec),
                  pl.BlockSpec((1, ns), vec),
                  pl.BlockSpec((1, S5_TILE), vec),
                  slab(D_FF),
                  slab(D_MODEL)],
        out_specs=(blk(lambda t: t), slab(D_FF), slab(D_MODEL)),
        out_shape=(jax.ShapeDtypeStruct((bsz * seq, D_SSM), BF16),
                   jax.ShapeDtypeStruct((D_FF, D_MODEL), BF16),
                   jax.ShapeDtypeStruct((D_MODEL, D_MODEL), BF16)),
        scratch_shapes=[pltpu.VMEM((S5_TILE, 2 * ns), BF16),
                        pltpu.VMEM((2 * ns, S5_TILE), BF16),
                        pltpu.VMEM((PB, ns), F32),
                        pltpu.VMEM((PB, ns), F32),
                        pltpu.VMEM((PB, 2 * ns), BF16),
                        pltpu.VMEM((PB, ns), F32),
                        pltpu.VMEM((PB, ns), F32),
                        pltpu.VMEM((PB, ns), BF16),
                        pltpu.VMEM((PB, ns), BF16),
                        pltpu.VMEM((SUBLANES, ns), F32),
                        pltpu.VMEM((SUBLANES, ns), F32),
                        pltpu.VMEM((3, SUBLANES, ns), F32),
                        pltpu.VMEM((3, SUBLANES, ns), F32),
                        pltpu.VMEM((1, ns), F32),
                        pltpu.VMEM((1, ns), F32)],
        compiler_params=_params("arbitrary", "arbitrary", "arbitrary"),
        name="s5",
    )(z, z, z, b_grp, c_grp, rep_b, rep_c, lam_re, lam_im, dvec, w_down, w_out)


def _mix_kernel(hg_ref, y_ref, ga_ref, gb_ref, wa_ref, gw_ref, gv_ref, o_ref,
                wab_ref, gwb_ref, gvb_ref):
    @pl.when(pl.program_id(1) == 0)
    def _():
        wab_ref[...] = wa_ref[...].astype(BF16)
        gwb_ref[...] = gw_ref[...].astype(BF16)
        gvb_ref[...] = gv_ref[...].astype(BF16)

    for c in range(o_ref.shape[0] // ROW_CHUNK):
        rows = slice(c * ROW_CHUNK, (c + 1) * ROW_CHUNK)
        y_a = jnp.dot(hg_ref[rows, :], wab_ref[...], preferred_element_type=F32)
        y = y_ref[rows, :]
        y_b = (jnp.dot(y, gwb_ref[...], preferred_element_type=F32)
               * jax.nn.sigmoid(jnp.dot(y, gvb_ref[...], preferred_element_type=F32)))
        mix = (jax.nn.sigmoid(ga_ref[rows, :].astype(F32)) * y_a
               + jax.nn.sigmoid(gb_ref[rows, :].astype(F32)) * y_b)
        o_ref[rows, :] = mix.astype(o_ref.dtype)


def _mix(hg, y, z, w_a, glu_w, glu_v, tm, tn):
    m = hg.shape[0]
    return pl.pallas_call(
        _mix_kernel,
        grid=(D_MODEL // tn, m // tm),
        in_specs=[pl.BlockSpec((tm, D_RNN), lambda j, i: (i, 0)),
                  pl.BlockSpec((tm, D_SSM), lambda j, i: (i, 0)),
                  pl.BlockSpec((tm, tn), lambda j, i: (i, Z_COL_GA // tn + j)),
                  pl.BlockSpec((tm, tn), lambda j, i: (i, Z_COL_GB // tn + j)),
                  pl.BlockSpec((D_RNN, tn), lambda j, i: (0, j)),
                  pl.BlockSpec((D_SSM, tn), lambda j, i: (0, j)),
                  pl.BlockSpec((D_SSM, tn), lambda j, i: (0, j))],
        out_specs=pl.BlockSpec((tm, tn), lambda j, i: (i, j)),
        out_shape=jax.ShapeDtypeStruct((m, D_MODEL), BF16),
        scratch_shapes=[pltpu.VMEM((D_RNN, tn), BF16),
                        pltpu.VMEM((D_SSM, tn), BF16),
                        pltpu.VMEM((D_SSM, tn), BF16)],
        compiler_params=_params("parallel", "arbitrary"),
        name="mix",
    )(hg, y, z, z, w_a, glu_w, glu_v)


def _layernorm(v, g, b):
    mu = jnp.mean(v, axis=-1, keepdims=True)
    c = v - mu
    var = jnp.mean(c * c, axis=-1, keepdims=True)
    return c * jax.lax.rsqrt(var + LN_EPS) * g + b


def _outproj_ln_kernel(mix_ref, pt_ref, x_ref, w_ref, g_ref, b_ref, o_ref, ob_ref, *, alpha):
    for c in range(PB // ROW_CHUNK):
        rows = slice(c * ROW_CHUNK, (c + 1) * ROW_CHUNK)
        mix = jnp.dot(pt_ref[rows, :], mix_ref[...], preferred_element_type=F32).astype(BF16)
        v = alpha * x_ref[rows, :] + jnp.dot(mix, w_ref[...], preferred_element_type=F32)
        out = _layernorm(v, g_ref[...], b_ref[...])
        o_ref[rows, :] = out
        ob_ref[rows, :] = out.astype(BF16)


def _outproj_ln(mix, perm_t, x, w_out, g, b, alpha):
    m = mix.shape[0]
    row = lambda i: (i, 0)
    fixed = lambda i: (0, 0)
    return pl.pallas_call(
        functools.partial(_outproj_ln_kernel, alpha=alpha),
        grid=(m // PB,),
        in_specs=[pl.BlockSpec((PB, D_MODEL), row),
                  pl.BlockSpec((PB, PB), fixed),
                  pl.BlockSpec((PB, D_MODEL), row),
                  pl.BlockSpec((D_MODEL, D_MODEL), fixed),
                  pl.BlockSpec((1, D_MODEL), fixed),
                  pl.BlockSpec((1, D_MODEL), fixed)],
        out_specs=(pl.BlockSpec((PB, D_MODEL), row), pl.BlockSpec((PB, D_MODEL), row)),
        out_shape=(jax.ShapeDtypeStruct((m, D_MODEL), F32),
                   jax.ShapeDtypeStruct((m, D_MODEL), BF16)),
        compiler_params=_params("parallel"),
        name="outproj_ln",
    )(mix, perm_t, x, w_out, g, b)


def _mlp_up_kernel(x_ref, w_ref, b_ref, o_ref, wb_ref):
    @pl.when(pl.program_id(1) == 0)
    def _():
        wb_ref[...] = w_ref[...].astype(BF16)

    v = jnp.dot(x_ref[...], wb_ref[...], preferred_element_type=F32) + b_ref[...]
    v = jnp.maximum(v, 0.0)
    o_ref[...] = (v * v).astype(o_ref.dtype)


def _mlp_up(xb, w_up, b_up, tm, tn):
    m = xb.shape[0]
    return pl.pallas_call(
        _mlp_up_kernel,
        grid=(D_FF // tn, m // tm),
        in_specs=[pl.BlockSpec((tm, D_MODEL), lambda j, i: (i, 0)),
                  pl.BlockSpec((D_MODEL, tn), lambda j, i: (0, j)),
                  pl.BlockSpec((1, tn), lambda j, i: (0, j))],
        out_specs=pl.BlockSpec((tm, tn), lambda j, i: (i, j)),
        out_shape=jax.ShapeDtypeStruct((m, D_FF), BF16),
        scratch_shapes=[pltpu.VMEM((D_MODEL, tn), BF16)],
        compiler_params=_params("parallel", "arbitrary"),
        name="mlp_up",
    )(xb, w_up, b_up)


def _mlp_down_ln_kernel(a_ref, w_ref, x_ref, bd_ref, g_ref, b_ref, o_ref, *, alpha):
    k = pl.program_id(1)
    last = pl.num_programs(1) - 1
    slab = x_ref.shape[0]
    slab_rows = pl.ds(pl.multiple_of(k * slab, slab), slab)

    def part():
        return jnp.dot(a_ref[...], w_ref[...], preferred_element_type=F32)

    @pl.when(k == 0)
    def _():
        o_ref[...] = part()
        o_ref[slab_rows, :] += alpha * x_ref[...]

    @pl.when(jnp.logical_and(k > 0, k < last))
    def _():
        o_ref[...] += part()
        o_ref[slab_rows, :] += alpha * x_ref[...]

    @pl.when(k == last)
    def _():
        o_ref[slab_rows, :] += alpha * x_ref[...]
        for c in range(o_ref.shape[0] // ROW_CHUNK):
            rows = slice(c * ROW_CHUNK, (c + 1) * ROW_CHUNK)
            v = (o_ref[rows, :] + jnp.dot(a_ref[rows, :], w_ref[...],
                                          preferred_element_type=F32)) + bd_ref[...]
            o_ref[rows, :] = _layernorm(v, g_ref[...], b_ref[...])


def _mlp_down_ln(a, w_down, x1, b_down, g, b, alpha, tm, tk):
    m = a.shape[0]
    n_k = D_FF // tk
    fixed = lambda i, k: (0, 0)
    return pl.pallas_call(
        functools.partial(_mlp_down_ln_kernel, alpha=alpha),
        grid=(m // tm, n_k),
        in_specs=[pl.BlockSpec((tm, tk), lambda i, k: (i, k)),
                  pl.BlockSpec((tk, D_MODEL), lambda i, k: (k, 0)),
                  pl.BlockSpec((tm // n_k, D_MODEL), lambda i, k: (i * n_k + k, 0)),
                  pl.BlockSpec((1, D_MODEL), fixed),
                  pl.BlockSpec((1, D_MODEL), fixed),
                  pl.BlockSpec((1, D_MODEL), fixed)],
        out_specs=pl.BlockSpec((tm, D_MODEL), lambda i, k: (i, 0)),
        out_shape=jax.ShapeDtypeStruct((m, D_MODEL), F32),
        compiler_params=_params("parallel", "arbitrary"),
        name="mlp_down_ln",
    )(a, w_down, x1, b_down, g, b)


def _s5_group_matrices(fr, fi, b_re, b_im, c_re, c_im):
    bbr = fr[..., None] * b_re - fi[..., None] * b_im
    bbi = fr[..., None] * b_im + fi[..., None] * b_re
    n_tiles = D_SSM // S5_TILE
    b_grp = jnp.concatenate([bbr.transpose(0, 2, 1), bbi.transpose(0, 2, 1)], axis=-1)
    b_grp = b_grp.reshape(n_tiles, S5_TILE, 2 * SSM_STATE)

    def per_tile(c):
        return c.transpose(0, 2, 1).reshape(n_tiles, S5_TILE_STATES, SSM_GROUP)

    c_grp = jnp.concatenate([per_tile(c_re), per_tile(-c_im)], axis=1)
    return b_grp, c_grp


def kernel(x, w_in, conv_w, conv_b, rg_wa, rg_ba, rg_wx, rg_bx, rg_lambda, w_a_out, ssm_a_re, ssm_a_im, ssm_log_dt, ssm_b_re, ssm_b_im, ssm_c_re, ssm_c_im, ssm_d, glu_w, glu_v, w_out, ln1_g, ln1_b, mlp_w_up, mlp_b_up, mlp_w_down, mlp_b_down, ln2_g, ln2_b):
    bsz, seq, _ = x.shape
    assert seq % PB == 0
    m = bsz * seq
    depth = w_in.shape[0]
    alpha = (2.0 * depth) ** 0.25
    perm = _interleave_matrix()
    perm_fwd = jnp.asarray(perm, BF16)
    perm_bwd = jnp.asarray(perm.T, BF16)
    for l in range(depth):
        xf = x.reshape(m, D_MODEL)
        xp = _interleave(xf, perm_fwd)
        z = _in_proj(xp, w_in[l], tm=2048, tn=1024)

        w_gate = jnp.concatenate([rg_wa[l], rg_wx[l]], axis=-1).astype(BF16)
        hg = _rglru(xp, w_in[l], conv_w[l], conv_b[l].reshape(1, D_RNN), w_gate,
                    rg_ba[l].reshape(1, D_RNN), rg_bx[l].reshape(1, D_RNN),
                    rg_lambda[l].reshape(1, D_RNN), bsz, seq)

        lbr, lbi, fr, fi = _s5_discretize(ssm_a_re[l], ssm_a_im[l], ssm_log_dt[l])
        b_grp, c_grp = _s5_group_matrices(fr, fi, ssm_b_re[l], ssm_b_im[l],
                                          ssm_c_re[l], ssm_c_im[l])
        y, w_down_bf, w_out_bf = _s5(z, b_grp, c_grp, lbr.reshape(1, -1), lbi.reshape(1, -1),
                                     ssm_d[l].reshape(1, D_SSM), mlp_w_down[l], w_out[l],
                                     bsz, seq)

        mix = _mix(hg, y, z, w_a_out[l], glu_w[l], glu_v[l], tm=1024, tn=512)
        x1, x1b = _outproj_ln(mix, perm_bwd, xf, w_out_bf,
                              ln1_g[l].reshape(1, D_MODEL), ln1_b[l].reshape(1, D_MODEL),
                              alpha)
        a = _mlp_up(x1b, mlp_w_up[l], mlp_b_up[l].reshape(1, D_FF), tm=2048, tn=1024)
        x2 = _mlp_down_ln(a, w_down_bf, x1,
                          mlp_b_down[l].reshape(1, D_MODEL),
                          ln2_g[l].reshape(1, D_MODEL), ln2_b[l].reshape(1, D_MODEL),
                          alpha, tm=1024, tk=2048)
        x = x2.reshape(bsz, seq, D_MODEL)
    return x
```

```python
import functools

import numpy as np
import jax
import jax.numpy as jnp
from jax.experimental import pallas as pl
from jax.experimental.pallas import tpu as pltpu

F32 = jnp.float32
BF16 = jnp.bfloat16

D_MODEL = 2048
D_RNN = D_MODEL
RG_HEADS = 16
RG_HEAD_DIM = D_RNN // RG_HEADS
CONV_WIDTH = 4
RG_C = 8.0
D_SSM = D_MODEL // 2
SSM_GROUP = 16
SSM_GROUPS = D_SSM // SSM_GROUP
SSM_STATE = 64
D_FF = 4 * D_MODEL
D_IN = 2 * D_RNN + D_SSM + 2 * D_MODEL
LN_EPS = 1e-5

SUBLANES = 8
BF16_ROWS = 2 * SUBLANES
VMEM_LIMIT = 56 * 1024 * 1024

NSEG = SUBLANES
PB = 512
SEG = PB // NSEG
HALO = (CONV_WIDTH - 1) * SUBLANES
ROW_CHUNK = 256

RG_TILE = 512
S5_TILE = 256
S5_TILE_GROUPS = S5_TILE // SSM_GROUP
S5_TILE_STATES = S5_TILE_GROUPS * SSM_STATE
S5_SCAN_LANES = 256
W_COL_GATE = D_RNN
W_COL_REST = 2 * D_RNN
Z_COLS = D_SSM + 2 * D_MODEL
Z_COL_GA = D_SSM
Z_COL_GB = D_SSM + D_MODEL


def _params(*sem):
    return pltpu.CompilerParams(dimension_semantics=sem, vmem_limit_bytes=VMEM_LIMIT)


def _interleave_matrix():
    p = np.arange(PB)
    src = (p % NSEG) * SEG + p // NSEG
    mat = np.zeros((PB, PB), np.float32)
    mat[p, src] = 1.0
    return mat


def _row_ids(shape):
    return jax.lax.broadcasted_iota(jnp.int32, shape, 0)


def _shift_rows(v, d, fill):
    return jnp.where(_row_ids(v.shape) >= d, pltpu.roll(v, d, 0), fill)


def _interleave_kernel(x_ref, p_ref, o_ref):
    for s in range(x_ref.shape[0] // PB):
        rows = slice(s * PB, (s + 1) * PB)
        o_ref[rows, :] = jnp.dot(p_ref[...], x_ref[rows, :].astype(BF16),
                                 preferred_element_type=F32).astype(BF16)


def _interleave(x, perm, blocks_per_step=2):
    m, k = x.shape
    rows = blocks_per_step * PB
    return pl.pallas_call(
        _interleave_kernel,
        grid=(m // rows,),
        in_specs=[pl.BlockSpec((rows, k), lambda i: (i, 0)),
                  pl.BlockSpec((PB, PB), lambda i: (0, 0))],
        out_specs=pl.BlockSpec((rows, k), lambda i: (i, 0)),
        out_shape=jax.ShapeDtypeStruct((m, k), BF16),
        compiler_params=_params("parallel"),
        name="interleave",
    )(x, perm)


def _in_proj_kernel(x_ref, w_ref, o_ref, wb_ref):
    @pl.when(pl.program_id(1) == 0)
    def _():
        wb_ref[...] = w_ref[...].astype(BF16)

    o_ref[...] = jnp.dot(x_ref[...], wb_ref[...],
                         preferred_element_type=F32).astype(o_ref.dtype)


def _in_proj(xp, w, tm, tn):
    m, k = xp.shape
    col0 = W_COL_REST // tn
    return pl.pallas_call(
        _in_proj_kernel,
        grid=(Z_COLS // tn, m // tm),
        in_specs=[pl.BlockSpec((tm, k), lambda j, i: (i, 0)),
                  pl.BlockSpec((k, tn), lambda j, i: (0, col0 + j))],
        out_specs=pl.BlockSpec((tm, tn), lambda j, i: (i, j)),
        out_shape=jax.ShapeDtypeStruct((m, Z_COLS), BF16),
        scratch_shapes=[pltpu.VMEM((k, tn), BF16)],
        compiler_params=_params("parallel", "arbitrary"),
        name="in_proj",
    )(xp, w)


def _rglru_gates(hh, z_ref, cw_ref, cb_ref, w_ref, ext_ref, xc_ref, pre_ref, gg_ref):
    sl = slice(hh * RG_HEAD_DIM, (hh + 1) * RG_HEAD_DIM)
    gg_ref[:, sl] = jax.nn.gelu(
        z_ref[:, RG_TILE + hh * RG_HEAD_DIM:RG_TILE + (hh + 1) * RG_HEAD_DIM])
    grp = (SUBLANES, RG_HEAD_DIM)
    x = z_ref[:, sl]
    first = _row_ids(grp) == 0
    halos = []
    for k in range(CONV_WIDTH - 1):
        prev_g = ext_ref[PB + k * SUBLANES:PB + (k + 1) * SUBLANES, sl]
        cur_g = x[PB - HALO + k * SUBLANES:PB - HALO + (k + 1) * SUBLANES, :]
        halos.append(jnp.where(first, pltpu.roll(prev_g, 1, 0), pltpu.roll(cur_g, 1, 0)))
    for k in range(CONV_WIDTH - 1):
        ext_ref[k * SUBLANES:(k + 1) * SUBLANES, sl] = halos[k]
    ext_ref[HALO:HALO + PB, sl] = x

    xc = cb_ref[:, sl] + cw_ref[CONV_WIDTH - 1:CONV_WIDTH, sl] * x
    for k in range(CONV_WIDTH - 1):
        xc = xc + cw_ref[k:k + 1, sl] * ext_ref[k * SUBLANES:k * SUBLANES + PB, sl]

    xc_ref[:, sl] = xc
    pre_ref[:, 2 * hh * RG_HEAD_DIM:2 * (hh + 1) * RG_HEAD_DIM] = jnp.dot(
        xc.astype(BF16), w_ref[hh], preferred_element_type=F32)


def _rglru_head(hh, o_ref, row0, cvec, ba_ref, bx_ref, xc_ref, pre_ref, gg_ref,
                a_ref, b_ref, carry_ref):
    sl = slice(hh * RG_HEAD_DIM, (hh + 1) * RG_HEAD_DIM)
    grp = (SUBLANES, RG_HEAD_DIM)
    first = _row_ids(grp) == 0
    xc = xc_ref[:, sl]
    r = jax.nn.sigmoid(pre_ref[:, 2 * hh * RG_HEAD_DIM:(2 * hh + 1) * RG_HEAD_DIM] + ba_ref[:, sl])
    i = jax.nn.sigmoid(pre_ref[:, (2 * hh + 1) * RG_HEAD_DIM:(2 * hh + 2) * RG_HEAD_DIM]
                       + bx_ref[:, sl])
    log_a = cvec[:, sl] * r
    a = jnp.exp(log_a)
    a_ref[:, sl] = a
    one_minus_a2 = -jnp.tanh(log_a) * (1.0 + a * a)
    b_ref[:, sl] = jnp.sqrt(one_minus_a2) * (i * xc)

    h_end, a_end = b_ref[0:SUBLANES, sl], a_ref[0:SUBLANES, sl]
    for tau in range(1, SEG):
        rows = slice(tau * SUBLANES, (tau + 1) * SUBLANES)
        a = a_ref[rows, sl]
        h_end = a * h_end + b_ref[rows, sl]
        a_end = a * a_end
        b_ref[rows, sl] = h_end
        a_ref[rows, sl] = a_end

    for d in (1, 2, 4):
        h_end = a_end * _shift_rows(h_end, d, 0.0) + h_end
        a_end = a_end * _shift_rows(a_end, d, 1.0)
    carry = carry_ref[:, sl]
    h_true = h_end + a_end * carry
    h_in = jnp.where(first, carry, pltpu.roll(h_true, 1, 0))
    carry_ref[:, sl] = h_true[SUBLANES - 1:SUBLANES, :]

    h_in2 = jnp.concatenate([h_in, h_in], axis=0)
    for j in range(PB // BF16_ROWS):
        rows = slice(j * BF16_ROWS, (j + 1) * BF16_ROWS)
        h = b_ref[rows, sl] + a_ref[rows, sl] * h_in2
        o_ref[row0 + j * BF16_ROWS:row0 + (j + 1) * BF16_ROWS, sl] = (
            h * gg_ref[rows, sl]).astype(o_ref.dtype)


def _rglru_kernel(x0_ref, x1_ref, x2_ref, wx_ref, wg_ref, cw_ref, cb_ref, w_ref,
                  ba_ref, bx_ref, lam_ref, o_ref,
                  wb_ref, z_ref, ext_ref, xca_ref, prea_ref, gga_ref, a_ref, b_ref, carry_ref):
    t = pl.program_id(2)
    n_heads = RG_TILE // RG_HEAD_DIM
    bufs = (xca_ref, prea_ref, gga_ref)

    def project(x_ref):
        z_ref[...] = jnp.dot(x_ref[...], wb_ref[...], preferred_element_type=F32)

    def gates():
        for hh in range(n_heads):
            _rglru_gates(hh, z_ref, cw_ref, cb_ref, w_ref, ext_ref, *bufs)

    @pl.when(t == 0)
    def _():
        wb_ref[:, :RG_TILE] = wx_ref[...].astype(BF16)
        wb_ref[:, RG_TILE:] = wg_ref[...].astype(BF16)
        ext_ref[PB:PB + HALO, :] = jnp.zeros((HALO, RG_TILE), F32)
        carry_ref[...] = jnp.zeros_like(carry_ref)
        project(x0_ref)
        gates()

    neg = -lam_ref[...]
    softplus = jnp.maximum(neg, 0.0) + jnp.log1p(jnp.exp(-jnp.abs(neg)))
    cvec = -RG_C * softplus

    def back(row0):
        for hh in range(n_heads):
            _rglru_head(hh, o_ref, row0, cvec, ba_ref, bx_ref, *bufs, a_ref, b_ref, carry_ref)

    project(x1_ref)
    back(0)
    gates()
    project(x2_ref)
    back(PB)
    gates()


def _rglru(xp, w_in, conv_w, conv_b, w_gate, ba, bx, lam, bsz, seq):
    n_t = seq // PB
    assert n_t % 2 == 0
    n_c = D_RNN // RG_TILE
    vec = lambda b, c, t: (0, c)
    return pl.pallas_call(
        _rglru_kernel,
        grid=(bsz, n_c, n_t // 2),
        in_specs=[pl.BlockSpec((PB, D_MODEL), lambda b, c, t: (b * n_t, 0)),
                  pl.BlockSpec((PB, D_MODEL), lambda b, c, t: (b * n_t + 2 * t + 1, 0)),
                  pl.BlockSpec((PB, D_MODEL),
                               lambda b, c, t: (b * n_t + jnp.minimum(2 * t + 2, n_t - 1), 0)),
                  pl.BlockSpec((D_MODEL, RG_TILE), vec),
                  pl.BlockSpec((D_MODEL, RG_TILE), lambda b, c, t: (0, W_COL_GATE // RG_TILE + c)),
                  pl.BlockSpec((CONV_WIDTH, RG_TILE), vec),
                  pl.BlockSpec((1, RG_TILE), vec),
                  pl.BlockSpec((RG_TILE // RG_HEAD_DIM, RG_HEAD_DIM, 2 * RG_HEAD_DIM),
                               lambda b, c, t: (c, 0, 0)),
                  pl.BlockSpec((1, RG_TILE), vec),
                  pl.BlockSpec((1, RG_TILE), vec),
                  pl.BlockSpec((1, RG_TILE), vec)],
        out_specs=pl.BlockSpec((2 * PB, RG_TILE), lambda b, c, t: (b * (n_t // 2) + t, c)),
        out_shape=jax.ShapeDtypeStruct((bsz * seq, D_RNN), BF16),
        scratch_shapes=[pltpu.VMEM((D_MODEL, 2 * RG_TILE), BF16),
                        pltpu.VMEM((PB, 2 * RG_TILE), F32),
                        pltpu.VMEM((HALO + PB, RG_TILE), F32),
                        pltpu.VMEM((PB, RG_TILE), F32),
                        pltpu.VMEM((PB, 2 * RG_TILE), F32),
                        pltpu.VMEM((PB, RG_TILE), F32),
                        pltpu.VMEM((PB, RG_TILE), F32),
                        pltpu.VMEM((PB, RG_TILE), F32),
                        pltpu.VMEM((1, RG_TILE), F32)],
        compiler_params=_params("parallel", "parallel", "arbitrary"),
        name="rglru",
    )(xp, xp, xp, w_in, w_in, conv_w, conv_b, w_gate, ba, bx, lam)


def _s5_discretize_kernel(are_ref, aim_ref, ldt_ref, lbr_ref, lbi_ref, fr_ref, fi_ref):
    dt = jnp.exp(ldt_ref[...])
    lr = jnp.minimum(are_ref[...], -1e-4)
    li = aim_ref[...]
    mag = jnp.exp(lr * dt)
    lbr = mag * jnp.cos(li * dt)
    lbi = mag * jnp.sin(li * dt)
    zr, zi = lbr - 1.0, lbi
    den = lr * lr + li * li
    lbr_ref[...] = lbr
    lbi_ref[...] = lbi
    fr_ref[...] = (zr * lr + zi * li) / den
    fi_ref[...] = (zi * lr - zr * li) / den


def _s5_discretize(a_re, a_im, log_dt):
    shp = jax.ShapeDtypeStruct((SSM_GROUPS, SSM_STATE), F32)
    return pl.pallas_call(
        _s5_discretize_kernel,
        out_shape=(shp, shp, shp, shp),
        name="s5_discretize",
    )(a_re, a_im, log_dt.reshape(SSM_GROUPS, 1))


def _cmul(ar, ai, br, bi):
    return ar * br - ai * bi, ar * bi + ai * br


def _s5_project(u_ref, bmat_ref, xr_ref, xi_ref, cols):
    ns = S5_TILE_STATES
    u = u_ref[...]
    xr_ref[:, cols] = jnp.dot(u, bmat_ref[:, cols], preferred_element_type=F32)
    xi_ref[:, cols] = jnp.dot(u, bmat_ref[:, ns + cols.start:ns + cols.stop],
                              preferred_element_type=F32)


def _s5_scan(cols, xr_ref, xi_ref, hb_ref, lre_ref, lim_ref, tab_re_ref, tab_im_ref,
             pow_re_ref, pow_im_ref, step_re_ref, step_im_ref, cr_ref, ci_ref):
    ns = S5_TILE_STATES
    grp = (SUBLANES, S5_SCAN_LANES)
    first = _row_ids(grp) == 0
    lr = jnp.broadcast_to(lre_ref[:, cols], grp)
    li = jnp.broadcast_to(lim_ref[:, cols], grp)

    er, ei = xr_ref[0:SUBLANES, cols], xi_ref[0:SUBLANES, cols]
    for tau in range(1, SEG):
        rows = slice(tau * SUBLANES, (tau + 1) * SUBLANES)
        nr = (lr * er - li * ei) + xr_ref[rows, cols]
        ni = (lr * ei + li * er) + xi_ref[rows, cols]
        xr_ref[rows, cols] = nr
        xi_ref[rows, cols] = ni
        er, ei = nr, ni

    for k, d in enumerate((1, 2, 4)):
        sr, si = pltpu.roll(er, d, 0), pltpu.roll(ei, d, 0)
        mr, mi = _cmul(step_re_ref[k, :, cols], step_im_ref[k, :, cols], sr, si)
        er, ei = er + mr, ei + mi
    cr, ci = cr_ref[:, cols], ci_ref[:, cols]
    mr, mi = _cmul(pow_re_ref[:, cols], pow_im_ref[:, cols], cr, ci)
    er, ei = er + mr, ei + mi
    in_r = jnp.where(first, cr, pltpu.roll(er, 1, 0))
    in_i = jnp.where(first, ci, pltpu.roll(ei, 1, 0))
    cr_ref[:, cols] = er[SUBLANES - 1:SUBLANES, :]
    ci_ref[:, cols] = ei[SUBLANES - 1:SUBLANES, :]

    in_r2 = jnp.concatenate([in_r, in_r], axis=0).astype(BF16)
    in_i2 = jnp.concatenate([in_i, in_i], axis=0).astype(BF16)
    for i in range(PB // BF16_ROWS):
        rows = slice(i * BF16_ROWS, (i + 1) * BF16_ROWS)
        fr, fi = _cmul(tab_re_ref[rows, cols], tab_im_ref[rows, cols], in_r2, in_i2)
        hb_ref[rows, cols] = xr_ref[rows, cols].astype(BF16) + fr
        hb_ref[rows, ns + cols.start:ns + cols.stop] = xi_ref[rows, cols].astype(BF16) + fi


def _s5_expand_weights(bc_ref, cc_ref, sb_ref, sc_ref, bmat_ref, cmat_ref):
    ns = S5_TILE_STATES

    def block_id(shape, dim, log2_block):
        return jax.lax.shift_right_logical(jax.lax.broadcasted_iota(jnp.int32, shape, dim),
                                           log2_block)

    log2_group = SSM_GROUP.bit_length() - 1
    log2_state = SSM_STATE.bit_length() - 1
    keep_b = block_id((S5_TILE, ns), 0, log2_group) == block_id((S5_TILE, ns), 1, log2_state)
    keep_c = block_id((ns, S5_TILE), 0, log2_state) == block_id((ns, S5_TILE), 1, log2_group)
    for part in range(2):
        b_grp = bc_ref[0, :, part * SSM_STATE:(part + 1) * SSM_STATE].astype(BF16)
        rep = jnp.dot(b_grp, sb_ref[...], preferred_element_type=F32)
        bmat_ref[:, part * ns:(part + 1) * ns] = jnp.where(keep_b, rep, 0.0).astype(BF16)
        c_grp = cc_ref[0, part * ns:(part + 1) * ns, :].astype(BF16)
        rep = jnp.dot(c_grp, sc_ref[...], preferred_element_type=F32)
        cmat_ref[part * ns:(part + 1) * ns, :] = jnp.where(keep_c, rep, 0.0).astype(BF16)


def _s5_kernel(u0_ref, uc_ref, un_ref, bc_ref, cc_ref, sb_ref, sc_ref, lre_ref, lim_ref, d_ref,
               wdown_ref, wout_ref, o_ref, wdown_bf_ref, wout_bf_ref, bmat_ref, cmat_ref,
               xr_ref, xi_ref, hb_ref, tab_re_ref, tab_im_ref, tabb_re_ref, tabb_im_ref,
               pow_re_ref, pow_im_ref, step_re_ref, step_im_ref, cr_ref, ci_ref):
    t = pl.program_id(2)
    ns = S5_TILE_STATES
    grp = (SUBLANES, ns)
    chunks = [slice(c0, c0 + S5_SCAN_LANES) for c0 in range(0, ns, S5_SCAN_LANES)]

    @pl.when(t == 0)
    def _():
        _s5_expand_weights(bc_ref, cc_ref, sb_ref, sc_ref, bmat_ref, cmat_ref)
        for cols in chunks:
            _s5_project(u0_ref, bmat_ref, xr_ref, xi_ref, cols)
        cr_ref[...] = jnp.zeros_like(cr_ref)
        ci_ref[...] = jnp.zeros_like(ci_ref)
        lr = jnp.broadcast_to(lre_ref[...], grp)
        li = jnp.broadcast_to(lim_ref[...], grp)

        def fill(tau, p):
            pr, pi = p
            rows = pl.ds(pl.multiple_of(tau * SUBLANES, SUBLANES), SUBLANES)
            tab_re_ref[rows, :] = pr
            tab_im_ref[rows, :] = pi
            return _cmul(pr, pi, lr, li)

        jax.lax.fori_loop(0, SEG, fill, (lr, li))
        tabb_re_ref[...] = tab_re_ref[...].astype(BF16)
        tabb_im_ref[...] = tab_im_ref[...].astype(BF16)
        mr = tab_re_ref[PB - SUBLANES:PB, :]
        mi = tab_im_ref[PB - SUBLANES:PB, :]
        row = _row_ids(grp)
        pr, pi = mr, mi
        for r in range(SUBLANES):
            pow_re_ref[r:r + 1, :] = pr[0:1, :]
            pow_im_ref[r:r + 1, :] = pi[0:1, :]
            if r + 1 in (1, 2, 4):
                k = (1, 2, 4).index(r + 1)
                step_re_ref[k] = jnp.where(row >= r + 1, pr, 0.0)
                step_im_ref[k] = jnp.where(row >= r + 1, pi, 0.0)
            pr, pi = _cmul(pr, pi, mr, mi)

    scan_refs = (lre_ref, lim_ref, tabb_re_ref, tabb_im_ref, pow_re_ref, pow_im_ref,
                 step_re_ref, step_im_ref, cr_ref, ci_ref)
    y = d_ref[...] * uc_ref[...].astype(F32)
    for cols in chunks:
        _s5_scan(cols, xr_ref, xi_ref, hb_ref, *scan_refs)
        y = y + jnp.dot(hb_ref[:, cols], cmat_ref[cols, :], preferred_element_type=F32)
        y = y + jnp.dot(hb_ref[:, ns + cols.start:ns + cols.stop],
                        cmat_ref[ns + cols.start:ns + cols.stop, :],
                        preferred_element_type=F32)
        _s5_project(un_ref, bmat_ref, xr_ref, xi_ref, cols)
    o_ref[...] = jax.nn.gelu(y).astype(o_ref.dtype)

    @pl.when(pl.program_id(0) == 0)
    def _():
        wdown_bf_ref[...] = wdown_ref[...].astype(BF16)
        wout_bf_ref[...] = wout_ref[...].astype(BF16)


def _s5(z, b_grp, c_grp, lam_re, lam_im, dvec, w_down, w_out, bsz, seq):
    n_t = seq // PB
    n_tiles = D_SSM // S5_TILE
    ns = S5_TILE_STATES
    vec = lambda b, c, t: (0, c)
    tile = lambda b, c, t: (c, 0, 0)
    fixed = lambda b, c, t: (0, 0)
    blk = lambda f: pl.BlockSpec((PB, S5_TILE), lambda b, c, t: (b * n_t + f(t), c))
    rep_b = jnp.asarray(np.tile(np.eye(SSM_STATE), (1, S5_TILE_GROUPS)), BF16)
    rep_c = jnp.asarray(np.tile(np.eye(SSM_GROUP), (1, S5_TILE_GROUPS)), BF16)
    slabs = n_tiles * n_t
    slab = lambda rows: pl.BlockSpec(
        (rows // slabs, D_MODEL),
        lambda b, c, t: (jnp.where(b == 0, c * n_t + t, slabs - 1), 0))
    return pl.pallas_call(
        _s5_kernel,
        grid=(bsz, n_tiles, n_t),
        in_specs=[blk(lambda t: 0),
                  blk(lambda t: t),
                  blk(lambda t: jnp.minimum(t + 1, n_t - 1)),
                  pl.BlockSpec((1, S5_TILE, 2 * SSM_STATE), tile),
                  pl.BlockSpec((1, 2 * ns, SSM_GROUP), tile),
                  pl.BlockSpec((SSM_STATE, ns), fixed),
                  pl.BlockSpec((SSM_GROUP, S5_TILE), fixed),
                  pl.BlockSpec((1, ns), vec),
                  pl.BlockSpec((1, ns), vec),
                  pl.BlockSpec((1, S5_TILE), vec),
                  slab(D_FF),
                  slab(D_MODEL)],
        out_specs=(blk(lambda t: t), slab(D_FF), slab(D_MODEL)),
        out_shape=(jax.ShapeDtypeStruct((bsz * seq, D_SSM), BF16),
                   jax.ShapeDtypeStruct((D_FF, D_MODEL), BF16),
                   jax.ShapeDtypeStruct((D_MODEL, D_MODEL), BF16)),
        scratch_shapes=[pltpu.VMEM((S5_TILE, 2 * ns), BF16),
                        pltpu.VMEM((2 * ns, S5_TILE), BF16),
                        pltpu.VMEM((PB, ns), F32),
                        pltpu.VMEM((PB, ns), F32),
                        pltpu.VMEM((PB, 2 * ns), BF16),
                        pltpu.VMEM((PB, ns), F32),
                        pltpu.VMEM((PB, ns), F32),
                        pltpu.VMEM((PB, ns), BF16),
                        pltpu.VMEM((PB, ns), BF16),
                        pltpu.VMEM((SUBLANES, ns), F32),
                        pltpu.VMEM((SUBLANES, ns), F32),
                        pltpu.VMEM((3, SUBLANES, ns), F32),
                        pltpu.VMEM((3, SUBLANES, ns), F32),
                        pltpu.VMEM((1, ns), F32),
                        pltpu.VMEM((1, ns), F32)],
        compiler_params=_params("arbitrary", "arbitrary", "arbitrary"),
        name="s5",
    )(z, z, z, b_grp, c_grp, rep_b, rep_c, lam_re, lam_im, dvec, w_down, w_out)


def _mix_kernel(hg_ref, y_ref, ga_ref, gb_ref, wa_ref, gw_ref, gv_ref, o_ref,
                wab_ref, gwb_ref, gvb_ref):
    @pl.when(pl.program_id(1) == 0)
    def _():
        wab_ref[...] = wa_ref[...].astype(BF16)
        gwb_ref[...] = gw_ref[...].astype(BF16)
        gvb_ref[...] = gv_ref[...].astype(BF16)

    for c in range(o_ref.shape[0] // ROW_CHUNK):
        rows = slice(c * ROW_CHUNK, (c + 1) * ROW_CHUNK)
        y_a = jnp.dot(hg_ref[rows, :], wab_ref[...], preferred_element_type=F32)
        y = y_ref[rows, :]
        y_b = (jnp.dot(y, gwb_ref[...], preferred_element_type=F32)
               * jax.nn.sigmoid(jnp.dot(y, gvb_ref[...], preferred_element_type=F32)))
        mix = (jax.nn.sigmoid(ga_ref[rows, :].astype(F32)) * y_a
               + jax.nn.sigmoid(gb_ref[rows, :].astype(F32)) * y_b)
        o_ref[rows, :] = mix.astype(o_ref.dtype)


def _mix(hg, y, z, w_a, glu_w, glu_v, tm, tn):
    m = hg.shape[0]
    return pl.pallas_call(
        _mix_kernel,
        grid=(D_MODEL // tn, m // tm),
        in_specs=[pl.BlockSpec((tm, D_RNN), lambda j, i: (i, 0)),
                  pl.BlockSpec((tm, D_SSM), lambda j, i: (i, 0)),
                  pl.BlockSpec((tm, tn), lambda j, i: (i, Z_COL_GA // tn + j)),
                  pl.BlockSpec((tm, tn), lambda j, i: (i, Z_COL_GB // tn + j)),
                  pl.BlockSpec((D_RNN, tn), lambda j, i: (0, j)),
                  pl.BlockSpec((D_SSM, tn), lambda j, i: (0, j)),
                  pl.BlockSpec((D_SSM, tn), lambda j, i: (0, j))],
        out_specs=pl.BlockSpec((tm, tn), lambda j, i: (i, j)),
        out_shape=jax.ShapeDtypeStruct((m, D_MODEL), BF16),
        scratch_shapes=[pltpu.VMEM((D_RNN, tn), BF16),
                        pltpu.VMEM((D_SSM, tn), BF16),
                        pltpu.VMEM((D_SSM, tn), BF16)],
        compiler_params=_params("parallel", "arbitrary"),
        name="mix",
    )(hg, y, z, z, w_a, glu_w, glu_v)


def _layernorm(v, g, b):
    mu = jnp.mean(v, axis=-1, keepdims=True)
    c = v - mu
    var = jnp.mean(c * c, axis=-1, keepdims=True)
    return c * jax.lax.rsqrt(var + LN_EPS) * g + b


def _outproj_ln_kernel(mix_ref, pt_ref, x_ref, w_ref, g_ref, b_ref, o_ref, ob_ref, *, alpha):
    for c in range(PB // ROW_CHUNK):
        rows = slice(c * ROW_CHUNK, (c + 1) * ROW_CHUNK)
        mix = jnp.dot(pt_ref[rows, :], mix_ref[...], preferred_element_type=F32).astype(BF16)
        v = alpha * x_ref[rows, :] + jnp.dot(mix, w_ref[...], preferred_element_type=F32)
        out = _layernorm(v, g_ref[...], b_ref[...])
        o_ref[rows, :] = out
        ob_ref[rows, :] = out.astype(BF16)


def _outproj_ln(mix, perm_t, x, w_out, g, b, alpha):
    m = mix.shape[0]
    row = lambda i: (i, 0)
    fixed = lambda i: (0, 0)
    return pl.pallas_call(
        functools.partial(_outproj_ln_kernel, alpha=alpha),
        grid=(m // PB,),
        in_specs=[pl.BlockSpec((PB, D_MODEL), row),
                  pl.BlockSpec((PB, PB), fixed),
                  pl.BlockSpec((PB, D_MODEL), row),
                  pl.BlockSpec((D_MODEL, D_MODEL), fixed),
                  pl.BlockSpec((1, D_MODEL), fixed),
                  pl.BlockSpec((1, D_MODEL), fixed)],
        out_specs=(pl.BlockSpec((PB, D_MODEL), row), pl.BlockSpec((PB, D_MODEL), row)),
        out_shape=(jax.ShapeDtypeStruct((m, D_MODEL), F32),
                   jax.ShapeDtypeStruct((m, D_MODEL), BF16)),
        compiler_params=_params("parallel"),
        name="outproj_ln",
    )(mix, perm_t, x, w_out, g, b)


def _mlp_up_kernel(x_ref, w_ref, b_ref, o_ref, wb_ref):
    @pl.when(pl.program_id(1) == 0)
    def _():
        wb_ref[...] = w_ref[...].astype(BF16)

    v = jnp.dot(x_ref[...], wb_ref[...], preferred_element_type=F32) + b_ref[...]
    v = jnp.maximum(v, 0.0)
    o_ref[...] = (v * v).astype(o_ref.dtype)


def _mlp_up(xb, w_up, b_up, tm, tn):
    m = xb.shape[0]
    return pl.pallas_call(
        _mlp_up_kernel,
        grid=(D_FF // tn, m // tm),
        in_specs=[pl.BlockSpec((tm, D_MODEL), lambda j, i: (i, 0)),
                  pl.BlockSpec((D_MODEL, tn), lambda j, i: (0, j)),
                  pl.BlockSpec((1, tn), lambda j, i: (0, j))],
        out_specs=pl.BlockSpec((tm, tn), lambda j, i: (i, j)),
        out_shape=jax.ShapeDtypeStruct((m, D_FF), BF16),
        scratch_shapes=[pltpu.VMEM((D_MODEL, tn), BF16)],
        compiler_params=_params("parallel", "arbitrary"),
        name="mlp_up",
    )(xb, w_up, b_up)


def _mlp_down_ln_kernel(a_ref, w_ref, x_ref, bd_ref, g_ref, b_ref, o_ref, *, alpha):
    k = pl.program_id(1)
    last = pl.num_programs(1) - 1
    slab = x_ref.shape[0]
    slab_rows = pl.ds(pl.multiple_of(k * slab, slab), slab)

    def part():
        return jnp.dot(a_ref[...], w_ref[...], preferred_element_type=F32)

    @pl.when(k == 0)
    def _():
        o_ref[...] = part()
        o_ref[slab_rows, :] += alpha * x_ref[...]

    @pl.when(jnp.logical_and(k > 0, k < last))
    def _():
        o_ref[...] += part()
        o_ref[slab_rows, :] += alpha * x_ref[...]

    @pl.when(k == last)
    def _():
        o_ref[slab_rows, :] += alpha * x_ref[...]
        for c in range(o_ref.shape[0] // ROW_CHUNK):
            rows = slice(c * ROW_CHUNK, (c + 1) * ROW_CHUNK)
            v = (o_ref[rows, :] + jnp.dot(a_ref[rows, :], w_ref[...],
                                          preferred_element_type=F32)) + bd_ref[...]
            o_ref[rows, :] = _layernorm(v, g_ref[...], b_ref[...])


def _mlp_down_ln(a, w_down, x1, b_down, g, b, alpha, tm, tk):
    m = a.shape[0]
    n_k = D_FF // tk
    fixed = lambda i, k: (0, 0)
    return pl.pallas_call(
        functools.partial(_mlp_down_ln_kernel, alpha=alpha),
        grid=(m // tm, n_k),
        in_specs=[pl.BlockSpec((tm, tk), lambda i, k: (i, k)),
                  pl.BlockSpec((tk, D_MODEL), lambda i, k: (k, 0)),
                  pl.BlockSpec((tm // n_k, D_MODEL), lambda i, k: (i * n_k + k, 0)),
                  pl.BlockSpec((1, D_MODEL), fixed),
                  pl.BlockSpec((1, D_MODEL), fixed),
                  pl.BlockSpec((1, D_MODEL), fixed)],
        out_specs=pl.BlockSpec((tm, D_MODEL), lambda i, k: (i, 0)),
        out_shape=jax.ShapeDtypeStruct((m, D_MODEL), F32),
        compiler_params=_params("parallel", "arbitrary"),
        name="mlp_down_ln",
    )(a, w_down, x1, b_down, g, b)


def _s5_group_matrices(fr, fi, b_re, b_im, c_re, c_im):
    bbr = fr[..., None] * b_re - fi[..., None] * b_im
    bbi = fr[..., None] * b_im + fi[..., None] * b_re
    n_tiles = D_SSM // S5_TILE
    b_grp = jnp.concatenate([bbr.transpose(0, 2, 1), bbi.transpose(0, 2, 1)], axis=-1)
    b_grp = b_grp.reshape(n_tiles, S5_TILE, 2 * SSM_STATE)

    def per_tile(c):
        return c.transpose(0, 2, 1).reshape(n_tiles, S5_TILE_STATES, SSM_GROUP)

    c_grp = jnp.concatenate([per_tile(c_re), per_tile(-c_im)], axis=1)
    return b_grp, c_grp


def kernel(x, w_in, conv_w, conv_b, rg_wa, rg_ba, rg_wx, rg_bx, rg_lambda, w_a_out, ssm_a_re, ssm_a_im, ssm_log_dt, ssm_b_re, ssm_b_im, ssm_c_re, ssm_c_im, ssm_d, glu_w, glu_v, w_out, ln1_g, ln1_b, mlp_w_up, mlp_b_up, mlp_w_down, mlp_b_down, ln2_g, ln2_b):
    bsz, seq, _ = x.shape
    assert seq % PB == 0
    m = bsz * seq
    depth = w_in.shape[0]
    alpha = (2.0 * depth) ** 0.25
    perm = _interleave_matrix()
    perm_fwd = jnp.asarray(perm, BF16)
    perm_bwd = jnp.asarray(perm.T, BF16)
    for l in range(depth):
        xf = x.reshape(m, D_MODEL)
        xp = _interleave(xf, perm_fwd)
        z = _in_proj(xp, w_in[l], tm=2048, tn=1024)

        w_gate = jnp.concatenate([rg_wa[l], rg_wx[l]], axis=-1).astype(BF16)
        hg = _rglru(xp, w_in[l], conv_w[l], conv_b[l].reshape(1, D_RNN), w_gate,
                    rg_ba[l].reshape(1, D_RNN), rg_bx[l].reshape(1, D_RNN),
                    rg_lambda[l].reshape(1, D_RNN), bsz, seq)

        lbr, lbi, fr, fi = _s5_discretize(ssm_a_re[l], ssm_a_im[l], ssm_log_dt[l])
        b_grp, c_grp = _s5_group_matrices(fr, fi, ssm_b_re[l], ssm_b_im[l],
                                          ssm_c_re[l], ssm_c_im[l])
        y, w_down_bf, w_out_bf = _s5(z, b_grp, c_grp, lbr.reshape(1, -1), lbi.reshape(1, -1),
                                     ssm_d[l].reshape(1, D_SSM), mlp_w_down[l], w_out[l],
                                     bsz, seq)

        mix = _mix(hg, y, z, w_a_out[l], glu_w[l], glu_v[l], tm=1024, tn=512)
        x1, x1b = _outproj_ln(mix, perm_bwd, xf, w_out_bf,
                              ln1_g[l].reshape(1, D_MODEL), ln1_b[l].reshape(1, D_MODEL),
                              alpha)
        a = _mlp_up(x1b, mlp_w_up[l], mlp_b_up[l].reshape(1, D_FF), tm=2048, tn=1024)
        x2 = _mlp_down_ln(a, w_down_bf, x1,
                          mlp_b_down[l].reshape(1, D_MODEL),
                          ln2_g[l].reshape(1, D_MODEL), ln2_b[l].reshape(1, D_MODEL),
                          alpha, tm=1024, tk=2048)
        x = x2.reshape(bsz, seq, D_MODEL)
    return x
```

```python
import functools

import numpy as np
import jax
import jax.numpy as jnp
from jax.experimental import pallas as pl
from jax.experimental.pallas import tpu as pltpu

F32 = jnp.float32
BF16 = jnp.bfloat16

D_MODEL = 2048
D_RNN = D_MODEL
RG_HEADS = 16
RG_HEAD_DIM = D_RNN // RG_HEADS
CONV_WIDTH = 4
RG_C = 8.0
D_SSM = D_MODEL // 2
SSM_GROUP = 16
SSM_GROUPS = D_SSM // SSM_GROUP
SSM_STATE = 64
D_FF = 4 * D_MODEL
D_IN = 2 * D_RNN + D_SSM + 2 * D_MODEL
LN_EPS = 1e-5

SUBLANES = 8
BF16_ROWS = 2 * SUBLANES
VMEM_LIMIT = 56 * 1024 * 1024

NSEG = SUBLANES
PB = 512
SEG = PB // NSEG
HALO = (CONV_WIDTH - 1) * SUBLANES
ROW_CHUNK = 256

RG_TILE = 512
S5_TILE = 256
S5_TILE_GROUPS = S5_TILE // SSM_GROUP
S5_TILE_STATES = S5_TILE_GROUPS * SSM_STATE
S5_SCAN_LANES = 256
W_COL_GATE = D_RNN
W_COL_REST = 2 * D_RNN
Z_COLS = D_SSM + 2 * D_MODEL
Z_COL_GA = D_SSM
Z_COL_GB = D_SSM + D_MODEL


def _params(*sem):
    return pltpu.CompilerParams(dimension_semantics=sem, vmem_limit_bytes=VMEM_LIMIT)


def _interleave_matrix():
    p = np.arange(PB)
    src = (p % NSEG) * SEG + p // NSEG
    mat = np.zeros((PB, PB), np.float32)
    mat[p, src] = 1.0
    return mat


def _row_ids(shape):
    return jax.lax.broadcasted_iota(jnp.int32, shape, 0)


def _shift_rows(v, d, fill):
    return jnp.where(_row_ids(v.shape) >= d, pltpu.roll(v, d, 0), fill)


def _interleave_kernel(x_ref, p_ref, o_ref):
    for s in range(x_ref.shape[0] // PB):
        rows = slice(s * PB, (s + 1) * PB)
        o_ref[rows, :] = jnp.dot(p_ref[...], x_ref[rows, :].astype(BF16),
                                 preferred_element_type=F32).astype(BF16)


def _interleave(x, perm, blocks_per_step=2):
    m, k = x.shape
    rows = blocks_per_step * PB
    return pl.pallas_call(
        _interleave_kernel,
        grid=(m // rows,),
        in_specs=[pl.BlockSpec((rows, k), lambda i: (i, 0)),
                  pl.BlockSpec((PB, PB), lambda i: (0, 0))],
        out_specs=pl.BlockSpec((rows, k), lambda i: (i, 0)),
        out_shape=jax.ShapeDtypeStruct((m, k), BF16),
        compiler_params=_params("parallel"),
        name="interleave",
    )(x, perm)


def _in_proj_kernel(x_ref, w_ref, o_ref, wb_ref):
    @pl.when(pl.program_id(1) == 0)
    def _():
        wb_ref[...] = w_ref[...].astype(BF16)

    o_ref[...] = jnp.dot(x_ref[...], wb_ref[...],
                         preferred_element_type=F32).astype(o_ref.dtype)


def _in_proj(xp, w, tm, tn):
    m, k = xp.shape
    col0 = W_COL_REST // tn
    return pl.pallas_call(
        _in_proj_kernel,
        grid=(Z_COLS // tn, m // tm),
        in_specs=[pl.BlockSpec((tm, k), lambda j, i: (i, 0)),
                  pl.BlockSpec((k, tn), lambda j, i: (0, col0 + j))],
        out_specs=pl.BlockSpec((tm, tn), lambda j, i: (i, j)),
        out_shape=jax.ShapeDtypeStruct((m, Z_COLS), BF16),
        scratch_shapes=[pltpu.VMEM((k, tn), BF16)],
        compiler_params=_params("parallel", "arbitrary"),
        name="in_proj",
    )(xp, w)


def _rglru_gates(hh, z_ref, cw_ref, cb_ref, w_ref, ext_ref, xc_ref, pre_ref, gg_ref):
    sl = slice(hh * RG_HEAD_DIM, (hh + 1) * RG_HEAD_DIM)
    gg_ref[:, sl] = jax.nn.gelu(
        z_ref[:, RG_TILE + hh * RG_HEAD_DIM:RG_TILE + (hh + 1) * RG_HEAD_DIM])
    grp = (SUBLANES, RG_HEAD_DIM)
    x = z_ref[:, sl]
    first = _row_ids(grp) == 0
    halos = []
    for k in range(CONV_WIDTH - 1):
        prev_g = ext_ref[PB + k * SUBLANES:PB + (k + 1) * SUBLANES, sl]
        cur_g = x[PB - HALO + k * SUBLANES:PB - HALO + (k + 1) * SUBLANES, :]
        halos.append(jnp.where(first, pltpu.roll(prev_g, 1, 0), pltpu.roll(cur_g, 1, 0)))
    for k in range(CONV_WIDTH - 1):
        ext_ref[k * SUBLANES:(k + 1) * SUBLANES, sl] = halos[k]
    ext_ref[HALO:HALO + PB, sl] = x

    xc = cb_ref[:, sl] + cw_ref[CONV_WIDTH - 1:CONV_WIDTH, sl] * x
    for k in range(CONV_WIDTH - 1):
        xc = xc + cw_ref[k:k + 1, sl] * ext_ref[k * SUBLANES:k * SUBLANES + PB, sl]

    xc_ref[:, sl] = xc
    pre_ref[:, 2 * hh * RG_HEAD_DIM:2 * (hh + 1) * RG_HEAD_DIM] = jnp.dot(
        xc.astype(BF16), w_ref[hh], preferred_element_type=F32)


def _rglru_head(hh, o_ref, row0, cvec, ba_ref, bx_ref, xc_ref, pre_ref, gg_ref,
                a_ref, b_ref, carry_ref):
    sl = slice(hh * RG_HEAD_DIM, (hh + 1) * RG_HEAD_DIM)
    grp = (SUBLANES, RG_HEAD_DIM)
    first = _row_ids(grp) == 0
    xc = xc_ref[:, sl]
    r = jax.nn.sigmoid(pre_ref[:, 2 * hh * RG_HEAD_DIM:(2 * hh + 1) * RG_HEAD_DIM] + ba_ref[:, sl])
    i = jax.nn.sigmoid(pre_ref[:, (2 * hh + 1) * RG_HEAD_DIM:(2 * hh + 2) * RG_HEAD_DIM]
                       + bx_ref[:, sl])
    log_a = cvec[:, sl] * r
    a = jnp.exp(log_a)
    a_ref[:, sl] = a
    one_minus_a2 = -jnp.tanh(log_a) * (1.0 + a * a)
    b_ref[:, sl] = jnp.sqrt(one_minus_a2) * (i * xc)

    h_end, a_end = b_ref[0:SUBLANES, sl], a_ref[0:SUBLANES, sl]
    for tau in range(1, SEG):
        rows = slice(tau * SUBLANES, (tau + 1) * SUBLANES)
        a = a_ref[rows, sl]
        h_end = a * h_end + b_ref[rows, sl]
        a_end = a * a_end
        b_ref[rows, sl] = h_end
        a_ref[rows, sl] = a_end

    for d in (1, 2, 4):
        h_end = a_end * _shift_rows(h_end, d, 0.0) + h_end
        a_end = a_end * _shift_rows(a_end, d, 1.0)
    carry = carry_ref[:, sl]
    h_true = h_end + a_end * carry
    h_in = jnp.where(first, carry, pltpu.roll(h_true, 1, 0))
    carry_ref[:, sl] = h_true[SUBLANES - 1:SUBLANES, :]

    h_in2 = jnp.concatenate([h_in, h_in], axis=0)
    for j in range(PB // BF16_ROWS):
        rows = slice(j * BF16_ROWS, (j + 1) * BF16_ROWS)
        h = b_ref[rows, sl] + a_ref[rows, sl] * h_in2
        o_ref[row0 + j * BF16_ROWS:row0 + (j + 1) * BF16_ROWS, sl] = (
            h * gg_ref[rows, sl]).astype(o_ref.dtype)


def _rglru_kernel(x0_ref, x1_ref, x2_ref, wx_ref, wg_ref, cw_ref, cb_ref, w_ref,
                  ba_ref, bx_ref, lam_ref, o_ref,
                  wb_ref, z_ref, ext_ref, xca_ref, prea_ref, gga_ref, a_ref, b_ref, carry_ref):
    t = pl.program_id(2)
    n_heads = RG_TILE // RG_HEAD_DIM
    bufs = (xca_ref, prea_ref, gga_ref)

    def project(x_ref):
        z_ref[...] = jnp.dot(x_ref[...], wb_ref[...], preferred_element_type=F32)

    def gates():
        for hh in range(n_heads):
            _rglru_gates(hh, z_ref, cw_ref, cb_ref, w_ref, ext_ref, *bufs)

    @pl.when(jnp.logical_and(t == 0, pl.program_id(1) == 0))
    def _():
        wb_ref[:, :RG_TILE] = wx_ref[...].astype(BF16)
        wb_ref[:, RG_TILE:] = wg_ref[...].astype(BF16)

    @pl.when(t == 0)
    def _():
        ext_ref[PB:PB + HALO, :] = jnp.zeros((HALO, RG_TILE), F32)
        carry_ref[...] = jnp.zeros_like(carry_ref)
        project(x0_ref)
        gates()

    neg = -lam_ref[...]
    softplus = jnp.maximum(neg, 0.0) + jnp.log1p(jnp.exp(-jnp.abs(neg)))
    cvec = -RG_C * softplus

    def back(row0):
        for hh in range(n_heads):
            _rglru_head(hh, o_ref, row0, cvec, ba_ref, bx_ref, *bufs, a_ref, b_ref, carry_ref)

    project(x1_ref)
    back(0)
    gates()
    project(x2_ref)
    back(PB)
    gates()


def _rglru(xp, w_in, conv_w, conv_b, w_gate, ba, bx, lam, bsz, seq):
    n_t = seq // PB
    assert n_t % 2 == 0
    n_c = D_RNN // RG_TILE
    vec = lambda c, b, t: (0, c)
    return pl.pallas_call(
        _rglru_kernel,
        grid=(n_c, bsz, n_t // 2),
        in_specs=[pl.BlockSpec((PB, D_MODEL), lambda c, b, t: (b * n_t, 0)),
                  pl.BlockSpec((PB, D_MODEL), lambda c, b, t: (b * n_t + 2 * t + 1, 0)),
                  pl.BlockSpec((PB, D_MODEL),
                               lambda c, b, t: (b * n_t + jnp.minimum(2 * t + 2, n_t - 1), 0)),
                  pl.BlockSpec((D_MODEL, RG_TILE), vec),
                  pl.BlockSpec((D_MODEL, RG_TILE), lambda c, b, t: (0, W_COL_GATE // RG_TILE + c)),
                  pl.BlockSpec((CONV_WIDTH, RG_TILE), vec),
                  pl.BlockSpec((1, RG_TILE), vec),
                  pl.BlockSpec((RG_TILE // RG_HEAD_DIM, RG_HEAD_DIM, 2 * RG_HEAD_DIM),
                               lambda c, b, t: (c, 0, 0)),
                  pl.BlockSpec((1, RG_TILE), vec),
                  pl.BlockSpec((1, RG_TILE), vec),
                  pl.BlockSpec((1, RG_TILE), vec)],
        out_specs=pl.BlockSpec((2 * PB, RG_TILE), lambda c, b, t: (b * (n_t // 2) + t, c)),
        out_shape=jax.ShapeDtypeStruct((bsz * seq, D_RNN), BF16),
        scratch_shapes=[pltpu.VMEM((D_MODEL, 2 * RG_TILE), BF16),
                        pltpu.VMEM((PB, 2 * RG_TILE), F32),
                        pltpu.VMEM((HALO + PB, RG_TILE), F32),
                        pltpu.VMEM((PB, RG_TILE), F32),
                        pltpu.VMEM((PB, 2 * RG_TILE), F32),
                        pltpu.VMEM((PB, RG_TILE), F32),
                        pltpu.VMEM((PB, RG_TILE), F32),
                        pltpu.VMEM((PB, RG_TILE), F32),
                        pltpu.VMEM((1, RG_TILE), F32)],
        compiler_params=_params("parallel", "arbitrary", "arbitrary"),
        name="rglru",
    )(xp, xp, xp, w_in, w_in, conv_w, conv_b, w_gate, ba, bx, lam)


def _s5_discretize_kernel(are_ref, aim_ref, ldt_ref, lbr_ref, lbi_ref, fr_ref, fi_ref):
    dt = jnp.exp(ldt_ref[...])
    lr = jnp.minimum(are_ref[...], -1e-4)
    li = aim_ref[...]
    mag = jnp.exp(lr * dt)
    lbr = mag * jnp.cos(li * dt)
    lbi = mag * jnp.sin(li * dt)
    zr, zi = lbr - 1.0, lbi
    den = lr * lr + li * li
    lbr_ref[...] = lbr
    lbi_ref[...] = lbi
    fr_ref[...] = (zr * lr + zi * li) / den
    fi_ref[...] = (zi * lr - zr * li) / den


def _s5_discretize(a_re, a_im, log_dt):
    shp = jax.ShapeDtypeStruct((SSM_GROUPS, SSM_STATE), F32)
    return pl.pallas_call(
        _s5_discretize_kernel,
        out_shape=(shp, shp, shp, shp),
        name="s5_discretize",
    )(a_re, a_im, log_dt.reshape(SSM_GROUPS, 1))


def _cmul(ar, ai, br, bi):
    return ar * br - ai * bi, ar * bi + ai * br


def _s5_project(u_ref, bmat_ref, xr_ref, xi_ref, cols):
    ns = S5_TILE_STATES
    u = u_ref[...]
    xr_ref[:, cols] = jnp.dot(u, bmat_ref[:, cols], preferred_element_type=F32)
    xi_ref[:, cols] = jnp.dot(u, bmat_ref[:, ns + cols.start:ns + cols.stop],
                              preferred_element_type=F32)


def _s5_scan(cols, xr_ref, xi_ref, hb_ref, lre_ref, lim_ref, tab_re_ref, tab_im_ref,
             pow_re_ref, pow_im_ref, step_re_ref, step_im_ref, cr_ref, ci_ref):
    ns = S5_TILE_STATES
    grp = (SUBLANES, S5_SCAN_LANES)
    first = _row_ids(grp) == 0
    lr = jnp.broadcast_to(lre_ref[:, cols], grp)
    li = jnp.broadcast_to(lim_ref[:, cols], grp)

    er, ei = xr_ref[0:SUBLANES, cols], xi_ref[0:SUBLANES, cols]
    for tau in range(1, SEG):
        rows = slice(tau * SUBLANES, (tau + 1) * SUBLANES)
        nr = (lr * er - li * ei) + xr_ref[rows, cols]
        ni = (lr * ei + li * er) + xi_ref[rows, cols]
        xr_ref[rows, cols] = nr
        xi_ref[rows, cols] = ni
        er, ei = nr, ni

    for k, d in enumerate((1, 2, 4)):
        sr, si = pltpu.roll(er, d, 0), pltpu.roll(ei, d, 0)
        mr, mi = _cmul(step_re_ref[k, :, cols], step_im_ref[k, :, cols], sr, si)
        er, ei = er + mr, ei + mi
    cr, ci = cr_ref[:, cols], ci_ref[:, cols]
    mr, mi = _cmul(pow_re_ref[:, cols], pow_im_ref[:, cols], cr, ci)
    er, ei = er + mr, ei + mi
    in_r = jnp.where(first, cr, pltpu.roll(er, 1, 0))
    in_i = jnp.where(first, ci, pltpu.roll(ei, 1, 0))
    cr_ref[:, cols] = er[SUBLANES - 1:SUBLANES, :]
    ci_ref[:, cols] = ei[SUBLANES - 1:SUBLANES, :]

    in_r2 = jnp.concatenate([in_r, in_r], axis=0).astype(BF16)
    in_i2 = jnp.concatenate([in_i, in_i], axis=0).astype(BF16)
    for i in range(PB // BF16_ROWS):
        rows = slice(i * BF16_ROWS, (i + 1) * BF16_ROWS)
        fr, fi = _cmul(tab_re_ref[rows, cols], tab_im_ref[rows, cols], in_r2, in_i2)
        hb_ref[rows, cols] = xr_ref[rows, cols].astype(BF16) + fr
        hb_ref[rows, ns + cols.start:ns + cols.stop] = xi_ref[rows, cols].astype(BF16) + fi


def _s5_expand_weights(bc_ref, cc_ref, sb_ref, sc_ref, bmat_ref, cmat_ref):
    ns = S5_TILE_STATES

    def block_id(shape, dim, log2_block):
        return jax.lax.shift_right_logical(jax.lax.broadcasted_iota(jnp.int32, shape, dim),
                                           log2_block)

    log2_group = SSM_GROUP.bit_length() - 1
    log2_state = SSM_STATE.bit_length() - 1
    keep_b = block_id((S5_TILE, ns), 0, log2_group) == block_id((S5_TILE, ns), 1, log2_state)
    keep_c = block_id((ns, S5_TILE), 0, log2_state) == block_id((ns, S5_TILE), 1, log2_group)
    for part in range(2):
        b_grp = bc_ref[0, :, part * SSM_STATE:(part + 1) * SSM_STATE].astype(BF16)
        rep = jnp.dot(b_grp, sb_ref[...], preferred_element_type=F32)
        bmat_ref[:, part * ns:(part + 1) * ns] = jnp.where(keep_b, rep, 0.0).astype(BF16)
        c_grp = cc_ref[0, part * ns:(part + 1) * ns, :].astype(BF16)
        rep = jnp.dot(c_grp, sc_ref[...], preferred_element_type=F32)
        cmat_ref[part * ns:(part + 1) * ns, :] = jnp.where(keep_c, rep, 0.0).astype(BF16)


def _s5_kernel(u0_ref, uc_ref, un_ref, bc_ref, cc_ref, sb_ref, sc_ref, lre_ref, lim_ref, d_ref,
               o_ref, bmat_ref, cmat_ref,
               xr_ref, xi_ref, hb_ref, tab_re_ref, tab_im_ref, tabb_re_ref, tabb_im_ref,
               pow_re_ref, pow_im_ref, step_re_ref, step_im_ref, cr_ref, ci_ref):
    t = pl.program_id(2)
    ns = S5_TILE_STATES
    grp = (SUBLANES, ns)
    chunks = [slice(c0, c0 + S5_SCAN_LANES) for c0 in range(0, ns, S5_SCAN_LANES)]

    @pl.when(t == 0)
    def _():
        _s5_expand_weights(bc_ref, cc_ref, sb_ref, sc_ref, bmat_ref, cmat_ref)
        for cols in chunks:
            _s5_project(u0_ref, bmat_ref, xr_ref, xi_ref, cols)
        cr_ref[...] = jnp.zeros_like(cr_ref)
        ci_ref[...] = jnp.zeros_like(ci_ref)
        lr = jnp.broadcast_to(lre_ref[...], grp)
        li = jnp.broadcast_to(lim_ref[...], grp)

        def fill(tau, p):
            pr, pi = p
            rows = pl.ds(pl.multiple_of(tau * SUBLANES, SUBLANES), SUBLANES)
            tab_re_ref[rows, :] = pr
            tab_im_ref[rows, :] = pi
            return _cmul(pr, pi, lr, li)

        jax.lax.fori_loop(0, SEG, fill, (lr, li))
        tabb_re_ref[...] = tab_re_ref[...].astype(BF16)
        tabb_im_ref[...] = tab_im_ref[...].astype(BF16)
        mr = tab_re_ref[PB - SUBLANES:PB, :]
        mi = tab_im_ref[PB - SUBLANES:PB, :]
        row = _row_ids(grp)
        pr, pi = mr, mi
        for r in range(SUBLANES):
            pow_re_ref[r:r + 1, :] = pr[0:1, :]
            pow_im_ref[r:r + 1, :] = pi[0:1, :]
            if r + 1 in (1, 2, 4):
                k = (1, 2, 4).index(r + 1)
                step_re_ref[k] = jnp.where(row >= r + 1, pr, 0.0)
                step_im_ref[k] = jnp.where(row >= r + 1, pi, 0.0)
            pr, pi = _cmul(pr, pi, mr, mi)

    scan_refs = (lre_ref, lim_ref, tabb_re_ref, tabb_im_ref, pow_re_ref, pow_im_ref,
                 step_re_ref, step_im_ref, cr_ref, ci_ref)
    y = d_ref[...] * uc_ref[...].astype(F32)
    for cols in chunks:
        _s5_scan(cols, xr_ref, xi_ref, hb_ref, *scan_refs)
        y = y + jnp.dot(hb_ref[:, cols], cmat_ref[cols, :], preferred_element_type=F32)
        y = y + jnp.dot(hb_ref[:, ns + cols.start:ns + cols.stop],
                        cmat_ref[ns + cols.start:ns + cols.stop, :],
                        preferred_element_type=F32)
        _s5_project(un_ref, bmat_ref, xr_ref, xi_ref, cols)
    o_ref[...] = jax.nn.gelu(y).astype(o_ref.dtype)


def _s5(z, b_grp, c_grp, lam_re, lam_im, dvec, bsz, seq):
    n_t = seq // PB
    n_tiles = D_SSM // S5_TILE
    ns = S5_TILE_STATES
    vec = lambda b, c, t: (0, c)
    tile = lambda b, c, t: (c, 0, 0)
    fixed = lambda b, c, t: (0, 0)
    blk = lambda f: pl.BlockSpec((PB, S5_TILE), lambda b, c, t: (b * n_t + f(t), c))
    rep_b = jnp.asarray(np.tile(np.eye(SSM_STATE), (1, S5_TILE_GROUPS)), BF16)
    rep_c = jnp.asarray(np.tile(np.eye(SSM_GROUP), (1, S5_TILE_GROUPS)), BF16)
    return pl.pallas_call(
        _s5_kernel,
        grid=(bsz, n_tiles, n_t),
        in_specs=[blk(lambda t: 0),
                  blk(lambda t: t),
                  blk(lambda t: jnp.minimum(t + 1, n_t - 1)),
                  pl.BlockSpec((1, S5_TILE, 2 * SSM_STATE), tile),
                  pl.BlockSpec((1, 2 * ns, SSM_GROUP), tile),
                  pl.BlockSpec((SSM_STATE, ns), fixed),
                  pl.BlockSpec((SSM_GROUP, S5_TILE), fixed),
                  pl.BlockSpec((1, ns), vec),
                  pl.BlockSpec((1, ns), vec),
                  pl.BlockSpec((1, S5_TILE), vec)],
        out_specs=blk(lambda t: t),
        out_shape=jax.ShapeDtypeStruct((bsz * seq, D_SSM), BF16),
        scratch_shapes=[pltpu.VMEM((S5_TILE, 2 * ns), BF16),
                        pltpu.VMEM((2 * ns, S5_TILE), BF16),
                        pltpu.VMEM((PB, ns), F32),
                        pltpu.VMEM((PB, ns), F32),
                        pltpu.VMEM((PB, 2 * ns), BF16),
                        pltpu.VMEM((PB, ns), F32),
                        pltpu.VMEM((PB, ns), F32),
                        pltpu.VMEM((PB, ns), BF16),
                        pltpu.VMEM((PB, ns), BF16),
                        pltpu.VMEM((SUBLANES, ns), F32),
                        pltpu.VMEM((SUBLANES, ns), F32),
                        pltpu.VMEM((3, SUBLANES, ns), F32),
                        pltpu.VMEM((3, SUBLANES, ns), F32),
                        pltpu.VMEM((1, ns), F32),
                        pltpu.VMEM((1, ns), F32)],
        compiler_params=_params("parallel", "parallel", "arbitrary"),
        name="s5",
    )(z, z, z, b_grp, c_grp, rep_b, rep_c, lam_re, lam_im, dvec)


def _mix_kernel(hg_ref, y_ref, ga_ref, gb_ref, wa_ref, gw_ref, gv_ref, wdown_ref, wout_ref,
                o_ref, wdown_bf_ref, wout_bf_ref, wab_ref, gwb_ref, gvb_ref):
    @pl.when(pl.program_id(1) == 0)
    def _():
        wab_ref[...] = wa_ref[...].astype(BF16)
        gwb_ref[...] = gw_ref[...].astype(BF16)
        gvb_ref[...] = gv_ref[...].astype(BF16)

    wdown_bf_ref[...] = wdown_ref[...].astype(BF16)
    wout_bf_ref[...] = wout_ref[...].astype(BF16)

    for c in range(o_ref.shape[0] // ROW_CHUNK):
        rows = slice(c * ROW_CHUNK, (c + 1) * ROW_CHUNK)
        y_a = jnp.dot(hg_ref[rows, :], wab_ref[...], preferred_element_type=F32)
        y = y_ref[rows, :]
        y_b = (jnp.dot(y, gwb_ref[...], preferred_element_type=F32)
               * jax.nn.sigmoid(jnp.dot(y, gvb_ref[...], preferred_element_type=F32)))
        mix = (jax.nn.sigmoid(ga_ref[rows, :].astype(F32)) * y_a
               + jax.nn.sigmoid(gb_ref[rows, :].astype(F32)) * y_b)
        o_ref[rows, :] = mix.astype(o_ref.dtype)


def _mix(hg, y, z, w_a, glu_w, glu_v, w_down, w_out, tm, tn):
    m = hg.shape[0]
    n_i = m // tm
    steps = (D_MODEL // tn) * n_i
    slab = lambda rows: pl.BlockSpec((rows // steps, D_MODEL), lambda j, i: (j * n_i + i, 0))
    return pl.pallas_call(
        _mix_kernel,
        grid=(D_MODEL // tn, n_i),
        in_specs=[pl.BlockSpec((tm, D_RNN), lambda j, i: (i, 0)),
                  pl.BlockSpec((tm, D_SSM), lambda j, i: (i, 0)),
                  pl.BlockSpec((tm, tn), lambda j, i: (i, Z_COL_GA // tn + j)),
                  pl.BlockSpec((tm, tn), lambda j, i: (i, Z_COL_GB // tn + j)),
                  pl.BlockSpec((D_RNN, tn), lambda j, i: (0, j)),
                  pl.BlockSpec((D_SSM, tn), lambda j, i: (0, j)),
                  pl.BlockSpec((D_SSM, tn), lambda j, i: (0, j)),
                  slab(D_FF),
                  slab(D_MODEL)],
        out_specs=(pl.BlockSpec((tm, tn), lambda j, i: (i, j)), slab(D_FF), slab(D_MODEL)),
        out_shape=(jax.ShapeDtypeStruct((m, D_MODEL), BF16),
                   jax.ShapeDtypeStruct((D_FF, D_MODEL), BF16),
                   jax.ShapeDtypeStruct((D_MODEL, D_MODEL), BF16)),
        scratch_shapes=[pltpu.VMEM((D_RNN, tn), BF16),
                        pltpu.VMEM((D_SSM, tn), BF16),
                        pltpu.VMEM((D_SSM, tn), BF16)],
        compiler_params=_params("arbitrary", "arbitrary"),
        name="mix",
    )(hg, y, z, z, w_a, glu_w, glu_v, w_down, w_out)


def _layernorm(v, g, b):
    mu = jnp.mean(v, axis=-1, keepdims=True)
    c = v - mu
    var = jnp.mean(c * c, axis=-1, keepdims=True)
    return c * jax.lax.rsqrt(var + LN_EPS) * g + b


def _outproj_ln_kernel(mix_ref, pt_ref, x_ref, w_ref, g_ref, b_ref, o_ref, ob_ref, *, alpha):
    for c in range(PB // ROW_CHUNK):
        rows = slice(c * ROW_CHUNK, (c + 1) * ROW_CHUNK)
        mix = jnp.dot(pt_ref[rows, :], mix_ref[...], preferred_element_type=F32).astype(BF16)
        v = alpha * x_ref[rows, :] + jnp.dot(mix, w_ref[...], preferred_element_type=F32)
        out = _layernorm(v, g_ref[...], b_ref[...])
        o_ref[rows, :] = out
        ob_ref[rows, :] = out.astype(BF16)


def _outproj_ln(mix, perm_t, x, w_out, g, b, alpha):
    m = mix.shape[0]
    row = lambda i: (i, 0)
    fixed = lambda i: (0, 0)
    return pl.pallas_call(
        functools.partial(_outproj_ln_kernel, alpha=alpha),
        grid=(m // PB,),
        in_specs=[pl.BlockSpec((PB, D_MODEL), row),
                  pl.BlockSpec((PB, PB), fixed),
                  pl.BlockSpec((PB, D_MODEL), row),
                  pl.BlockSpec((D_MODEL, D_MODEL), fixed),
                  pl.BlockSpec((1, D_MODEL), fixed),
                  pl.BlockSpec((1, D_MODEL), fixed)],
        out_specs=(pl.BlockSpec((PB, D_MODEL), row), pl.BlockSpec((PB, D_MODEL), row)),
        out_shape=(jax.ShapeDtypeStruct((m, D_MODEL), F32),
                   jax.ShapeDtypeStruct((m, D_MODEL), BF16)),
        compiler_params=_params("parallel"),
        name="outproj_ln",
    )(mix, perm_t, x, w_out, g, b)


def _mlp_up_kernel(x_ref, w_ref, b_ref, o_ref, wb_ref):
    @pl.when(pl.program_id(1) == 0)
    def _():
        wb_ref[...] = w_ref[...].astype(BF16)

    v = jnp.dot(x_ref[...], wb_ref[...], preferred_element_type=F32) + b_ref[...]
    v = jnp.maximum(v, 0.0)
    o_ref[...] = (v * v).astype(o_ref.dtype)


def _mlp_up(xb, w_up, b_up, tm, tn):
    m = xb.shape[0]
    return pl.pallas_call(
        _mlp_up_kernel,
        grid=(D_FF // tn, m // tm),
        in_specs=[pl.BlockSpec((tm, D_MODEL), lambda j, i: (i, 0)),
                  pl.BlockSpec((D_MODEL, tn), lambda j, i: (0, j)),
                  pl.BlockSpec((1, tn), lambda j, i: (0, j))],
        out_specs=pl.BlockSpec((tm, tn), lambda j, i: (i, j)),
        out_shape=jax.ShapeDtypeStruct((m, D_FF), BF16),
        scratch_shapes=[pltpu.VMEM((D_MODEL, tn), BF16)],
        compiler_params=_params("parallel", "arbitrary"),
        name="mlp_up",
    )(xb, w_up, b_up)


def _mlp_down_ln_kernel(a_ref, w_ref, x_ref, bd_ref, g_ref, b_ref, o_ref, *, alpha):
    k = pl.program_id(1)
    last = pl.num_programs(1) - 1
    slab = x_ref.shape[0]
    slab_rows = pl.ds(pl.multiple_of(k * slab, slab), slab)

    def part():
        return jnp.dot(a_ref[...], w_ref[...], preferred_element_type=F32)

    @pl.when(k == 0)
    def _():
        o_ref[...] = part()
        o_ref[slab_rows, :] += alpha * x_ref[...]

    @pl.when(jnp.logical_and(k > 0, k < last))
    def _():
        o_ref[...] += part()
        o_ref[slab_rows, :] += alpha * x_ref[...]

    @pl.when(k == last)
    def _():
        o_ref[slab_rows, :] += alpha * x_ref[...]
        for c in range(o_ref.shape[0] // ROW_CHUNK):
            rows = slice(c * ROW_CHUNK, (c + 1) * ROW_CHUNK)
            v = (o_ref[rows, :] + jnp.dot(a_ref[rows, :], w_ref[...],
                                          preferred_element_type=F32)) + bd_ref[...]
            o_ref[rows, :] = _layernorm(v, g_ref[...], b_ref[...])


def _mlp_down_ln(a, w_down, x1, b_down, g, b, alpha, tm, tk):
    m = a.shape[0]
    n_k = D_FF // tk
    fixed = lambda i, k: (0, 0)
    return pl.pallas_call(
        functools.partial(_mlp_down_ln_kernel, alpha=alpha),
        grid=(m // tm, n_k),
        in_specs=[pl.BlockSpec((tm, tk), lambda i, k: (i, k)),
                  pl.BlockSpec((tk, D_MODEL), lambda i, k: (k, 0)),
                  pl.BlockSpec((tm // n_k, D_MODEL), lambda i, k: (i * n_k + k, 0)),
                  pl.BlockSpec((1, D_MODEL), fixed),
                  pl.BlockSpec((1, D_MODEL), fixed),
                  pl.BlockSpec((1, D_MODEL), fixed)],
        out_specs=pl.BlockSpec((tm, D_MODEL), lambda i, k: (i, 0)),
        out_shape=jax.ShapeDtypeStruct((m, D_MODEL), F32),
        compiler_params=_params("parallel", "arbitrary"),
        name="mlp_down_ln",
    )(a, w_down, x1, b_down, g, b)


def _s5_group_matrices(fr, fi, b_re, b_im, c_re, c_im):
    bbr = fr[..., None] * b_re - fi[..., None] * b_im
    bbi = fr[..., None] * b_im + fi[..., None] * b_re
    n_tiles = D_SSM // S5_TILE
    b_grp = jnp.concatenate([bbr.transpose(0, 2, 1), bbi.transpose(0, 2, 1)], axis=-1)
    b_grp = b_grp.reshape(n_tiles, S5_TILE, 2 * SSM_STATE)

    def per_tile(c):
        return c.transpose(0, 2, 1).reshape(n_tiles, S5_TILE_STATES, SSM_GROUP)

    c_grp = jnp.concatenate([per_tile(c_re), per_tile(-c_im)], axis=1)
    return b_grp, c_grp


def kernel(x, w_in, conv_w, conv_b, rg_wa, rg_ba, rg_wx, rg_bx, rg_lambda, w_a_out, ssm_a_re, ssm_a_im, ssm_log_dt, ssm_b_re, ssm_b_im, ssm_c_re, ssm_c_im, ssm_d, glu_w, glu_v, w_out, ln1_g, ln1_b, mlp_w_up, mlp_b_up, mlp_w_down, mlp_b_down, ln2_g, ln2_b):
    bsz, seq, _ = x.shape
    assert seq % PB == 0
    m = bsz * seq
    depth = w_in.shape[0]
    alpha = (2.0 * depth) ** 0.25
    perm = _interleave_matrix()
    perm_fwd = jnp.asarray(perm, BF16)
    perm_bwd = jnp.asarray(perm.T, BF16)
    for l in range(depth):
        xf = x.reshape(m, D_MODEL)
        xp = _interleave(xf, perm_fwd)
        z = _in_proj(xp, w_in[l], tm=2048, tn=1024)

        w_gate = jnp.concatenate([rg_wa[l], rg_wx[l]], axis=-1).astype(BF16)
        hg = _rglru(xp, w_in[l], conv_w[l], conv_b[l].reshape(1, D_RNN), w_gate,
                    rg_ba[l].reshape(1, D_RNN), rg_bx[l].reshape(1, D_RNN),
                    rg_lambda[l].reshape(1, D_RNN), bsz, seq)

        lbr, lbi, fr, fi = _s5_discretize(ssm_a_re[l], ssm_a_im[l], ssm_log_dt[l])
        b_grp, c_grp = _s5_group_matrices(fr, fi, ssm_b_re[l], ssm_b_im[l],
                                          ssm_c_re[l], ssm_c_im[l])
        y = _s5(z, b_grp, c_grp, lbr.reshape(1, -1), lbi.reshape(1, -1),
                ssm_d[l].reshape(1, D_SSM), bsz, seq)

        mix, w_down_bf, w_out_bf = _mix(hg, y, z, w_a_out[l], glu_w[l], glu_v[l],
                                        mlp_w_down[l], w_out[l], tm=1024, tn=512)
        x1, x1b = _outproj_ln(mix, perm_bwd, xf, w_out_bf,
                              ln1_g[l].reshape(1, D_MODEL), ln1_b[l].reshape(1, D_MODEL),
                              alpha)
        a = _mlp_up(x1b, mlp_w_up[l], mlp_b_up[l].reshape(1, D_FF), tm=2048, tn=1024)
        x2 = _mlp_down_ln(a, w_down_bf, x1,
                          mlp_b_down[l].reshape(1, D_MODEL),
                          ln2_g[l].reshape(1, D_MODEL), ln2_b[l].reshape(1, D_MODEL),
                          alpha, tm=1024, tk=2048)
        x = x2.reshape(bsz, seq, D_MODEL)
    return x
```

```python
import functools

import numpy as np
import jax
import jax.numpy as jnp
from jax.experimental import pallas as pl
from jax.experimental.pallas import tpu as pltpu

F32 = jnp.float32
BF16 = jnp.bfloat16

D_MODEL = 2048
D_RNN = D_MODEL
RG_HEADS = 16
RG_HEAD_DIM = D_RNN // RG_HEADS
CONV_WIDTH = 4
RG_C = 8.0
D_SSM = D_MODEL // 2
SSM_GROUP = 16
SSM_GROUPS = D_SSM // SSM_GROUP
SSM_STATE = 64
D_FF = 4 * D_MODEL
D_IN = 2 * D_RNN + D_SSM + 2 * D_MODEL
LN_EPS = 1e-5

SUBLANES = 8
BF16_ROWS = 2 * SUBLANES
VMEM_LIMIT = 56 * 1024 * 1024

NSEG = SUBLANES
PB = 512
SEG = PB // NSEG
HALO = (CONV_WIDTH - 1) * SUBLANES
ROW_CHUNK = 256

RG_TILE = 512
RG_GATE_ROWS = 32
S5_TILE = 256
S5_TILE_GROUPS = S5_TILE // SSM_GROUP
S5_TILE_STATES = S5_TILE_GROUPS * SSM_STATE
S5_SCAN_LANES = 256
W_COL_GATE = D_RNN
W_COL_REST = 2 * D_RNN
Z_COLS = D_SSM + 2 * D_MODEL
Z_COL_GA = D_SSM
Z_COL_GB = D_SSM + D_MODEL


def _params(*sem):
    return pltpu.CompilerParams(dimension_semantics=sem, vmem_limit_bytes=VMEM_LIMIT)


def _interleave_matrix():
    p = np.arange(PB)
    src = (p % NSEG) * SEG + p // NSEG
    mat = np.zeros((PB, PB), np.float32)
    mat[p, src] = 1.0
    return mat


def _row_ids(shape):
    return jax.lax.broadcasted_iota(jnp.int32, shape, 0)


def _shift_rows(v, d, fill):
    return jnp.where(_row_ids(v.shape) >= d, pltpu.roll(v, d, 0), fill)


def _interleave_kernel(x_ref, p_ref, o_ref):
    for s in range(x_ref.shape[0] // PB):
        rows = slice(s * PB, (s + 1) * PB)
        o_ref[rows, :] = jnp.dot(p_ref[...], x_ref[rows, :].astype(BF16),
                                 preferred_element_type=F32).astype(BF16)


def _interleave(x, perm, blocks_per_step=2):
    m, k = x.shape
    rows = blocks_per_step * PB
    return pl.pallas_call(
        _interleave_kernel,
        grid=(m // rows,),
        in_specs=[pl.BlockSpec((rows, k), lambda i: (i, 0)),
                  pl.BlockSpec((PB, PB), lambda i: (0, 0))],
        out_specs=pl.BlockSpec((rows, k), lambda i: (i, 0)),
        out_shape=jax.ShapeDtypeStruct((m, k), BF16),
        compiler_params=_params("parallel"),
        name="interleave",
    )(x, perm)


def _in_proj_kernel(x_ref, w_ref, o_ref, wb_ref):
    @pl.when(pl.program_id(1) == 0)
    def _():
        wb_ref[...] = w_ref[...].astype(BF16)

    o_ref[...] = jnp.dot(x_ref[...], wb_ref[...],
                         preferred_element_type=F32).astype(o_ref.dtype)


def _in_proj(xp, w, tm, tn):
    m, k = xp.shape
    col0 = W_COL_REST // tn
    return pl.pallas_call(
        _in_proj_kernel,
        grid=(Z_COLS // tn, m // tm),
        in_specs=[pl.BlockSpec((tm, k), lambda j, i: (i, 0)),
                  pl.BlockSpec((k, tn), lambda j, i: (0, col0 + j))],
        out_specs=pl.BlockSpec((tm, tn), lambda j, i: (i, j)),
        out_shape=jax.ShapeDtypeStruct((m, Z_COLS), BF16),
        scratch_shapes=[pltpu.VMEM((k, tn), BF16)],
        compiler_params=_params("parallel", "arbitrary"),
        name="in_proj",
    )(xp, w)


def _rglru_gates(hh, z_ref, cw_ref, cb_ref, w_ref, ext_ref, xc_ref, pre_ref, gg_ref):
    sl = slice(hh * RG_HEAD_DIM, (hh + 1) * RG_HEAD_DIM)
    gg_ref[:, sl] = jax.nn.gelu(
        z_ref[:, RG_TILE + hh * RG_HEAD_DIM:RG_TILE + (hh + 1) * RG_HEAD_DIM])
    grp = (SUBLANES, RG_HEAD_DIM)
    x = z_ref[:, sl]
    first = _row_ids(grp) == 0
    halos = []
    for k in range(CONV_WIDTH - 1):
        prev_g = ext_ref[PB + k * SUBLANES:PB + (k + 1) * SUBLANES, sl]
        cur_g = x[PB - HALO + k * SUBLANES:PB - HALO + (k + 1) * SUBLANES, :]
        halos.append(jnp.where(first, pltpu.roll(prev_g, 1, 0), pltpu.roll(cur_g, 1, 0)))
    for k in range(CONV_WIDTH - 1):
        ext_ref[k * SUBLANES:(k + 1) * SUBLANES, sl] = halos[k]
    ext_ref[HALO:HALO + PB, sl] = x

    xc = cb_ref[:, sl] + cw_ref[CONV_WIDTH - 1:CONV_WIDTH, sl] * x
    for k in range(CONV_WIDTH - 1):
        xc = xc + cw_ref[k:k + 1, sl] * ext_ref[k * SUBLANES:k * SUBLANES + PB, sl]

    xc_ref[:, sl] = xc
    pre_ref[:, 2 * hh * RG_HEAD_DIM:2 * (hh + 1) * RG_HEAD_DIM] = jnp.dot(
        xc.astype(BF16), w_ref[hh], preferred_element_type=F32)


def _rglru_head(hh, o_ref, row0, cvec, ba_ref, bx_ref, xc_ref, pre_ref, gg_ref,
                a_ref, b_ref, carry_ref):
    sl = slice(hh * RG_HEAD_DIM, (hh + 1) * RG_HEAD_DIM)
    grp = (SUBLANES, RG_HEAD_DIM)
    first = _row_ids(grp) == 0
    for c in range(PB // RG_GATE_ROWS):
        rows = slice(c * RG_GATE_ROWS, (c + 1) * RG_GATE_ROWS)
        xc = xc_ref[rows, sl]
        r = jax.nn.sigmoid(pre_ref[rows, 2 * hh * RG_HEAD_DIM:(2 * hh + 1) * RG_HEAD_DIM]
                           + ba_ref[:, sl])
        i = jax.nn.sigmoid(pre_ref[rows, (2 * hh + 1) * RG_HEAD_DIM:(2 * hh + 2) * RG_HEAD_DIM]
                           + bx_ref[:, sl])
        log_a = cvec[:, sl] * r
        a = jnp.exp(log_a)
        a_ref[rows, sl] = a
        one_minus_a2 = -jnp.tanh(log_a) * (1.0 + a * a)
        b_ref[rows, sl] = jnp.sqrt(one_minus_a2) * (i * xc)

    h_end, a_end = b_ref[0:SUBLANES, sl], a_ref[0:SUBLANES, sl]
    for tau in range(1, SEG):
        rows = slice(tau * SUBLANES, (tau + 1) * SUBLANES)
        a = a_ref[rows, sl]
        h_end = a * h_end + b_ref[rows, sl]
        a_end = a * a_end
        b_ref[rows, sl] = h_end
        a_ref[rows, sl] = a_end

    for d in (1, 2, 4):
        h_end = a_end * _shift_rows(h_end, d, 0.0) + h_end
        a_end = a_end * _shift_rows(a_end, d, 1.0)
    carry = carry_ref[:, sl]
    h_true = h_end + a_end * carry
    h_in = jnp.where(first, carry, pltpu.roll(h_true, 1, 0))
    carry_ref[:, sl] = h_true[SUBLANES - 1:SUBLANES, :]

    h_in2 = jnp.concatenate([h_in, h_in], axis=0)
    for j in range(PB // BF16_ROWS):
        rows = slice(j * BF16_ROWS, (j + 1) * BF16_ROWS)
        h = b_ref[rows, sl] + a_ref[rows, sl] * h_in2
        o_ref[row0 + j * BF16_ROWS:row0 + (j + 1) * BF16_ROWS, sl] = (
            h * gg_ref[rows, sl]).astype(o_ref.dtype)


def _rglru_kernel(x0_ref, x1_ref, x2_ref, wx_ref, wg_ref, cw_ref, cb_ref, w_ref,
                  ba_ref, bx_ref, lam_ref, o_ref,
                  wb_ref, z_ref, ext_ref, xca_ref, prea_ref, gga_ref, a_ref, b_ref, carry_ref):
    t = pl.program_id(2)
    n_heads = RG_TILE // RG_HEAD_DIM
    bufs = (xca_ref, prea_ref, gga_ref)

    def project(x_ref):
        z_ref[...] = jnp.dot(x_ref[...], wb_ref[...], preferred_element_type=F32)

    def gates():
        for hh in range(n_heads):
            _rglru_gates(hh, z_ref, cw_ref, cb_ref, w_ref, ext_ref, *bufs)

    @pl.when(jnp.logical_and(t == 0, pl.program_id(1) == 0))
    def _():
        wb_ref[:, :RG_TILE] = wx_ref[...].astype(BF16)
        wb_ref[:, RG_TILE:] = wg_ref[...].astype(BF16)

    @pl.when(t == 0)
    def _():
        ext_ref[PB:PB + HALO, :] = jnp.zeros((HALO, RG_TILE), F32)
        carry_ref[...] = jnp.zeros_like(carry_ref)
        project(x0_ref)
        gates()

    neg = -lam_ref[...]
    softplus = jnp.maximum(neg, 0.0) + jnp.log1p(jnp.exp(-jnp.abs(neg)))
    cvec = -RG_C * softplus

    def back(row0):
        for hh in range(n_heads):
            _rglru_head(hh, o_ref, row0, cvec, ba_ref, bx_ref, *bufs, a_ref, b_ref, carry_ref)

    project(x1_ref)
    back(0)
    gates()
    project(x2_ref)
    back(PB)
    gates()


def _rglru(xp, w_in, conv_w, conv_b, w_gate, ba, bx, lam, bsz, seq):
    n_t = seq // PB
    assert n_t % 2 == 0
    n_c = D_RNN // RG_TILE
    vec = lambda c, b, t: (0, c)
    return pl.pallas_call(
        _rglru_kernel,
        grid=(n_c, bsz, n_t // 2),
        in_specs=[pl.BlockSpec((PB, D_MODEL), lambda c, b, t: (b * n_t, 0)),
                  pl.BlockSpec((PB, D_MODEL), lambda c, b, t: (b * n_t + 2 * t + 1, 0)),
                  pl.BlockSpec((PB, D_MODEL),
                               lambda c, b, t: (b * n_t + jnp.minimum(2 * t + 2, n_t - 1), 0)),
                  pl.BlockSpec((D_MODEL, RG_TILE), vec),
                  pl.BlockSpec((D_MODEL, RG_TILE), lambda c, b, t: (0, W_COL_GATE // RG_TILE + c)),
                  pl.BlockSpec((CONV_WIDTH, RG_TILE), vec),
                  pl.BlockSpec((1, RG_TILE), vec),
                  pl.BlockSpec((RG_TILE // RG_HEAD_DIM, RG_HEAD_DIM, 2 * RG_HEAD_DIM),
                               lambda c, b, t: (c, 0, 0)),
                  pl.BlockSpec((1, RG_TILE), vec),
                  pl.BlockSpec((1, RG_TILE), vec),
                  pl.BlockSpec((1, RG_TILE), vec)],
        out_specs=pl.BlockSpec((2 * PB, RG_TILE), lambda c, b, t: (b * (n_t // 2) + t, c)),
        out_shape=jax.ShapeDtypeStruct((bsz * seq, D_RNN), BF16),
        scratch_shapes=[pltpu.VMEM((D_MODEL, 2 * RG_TILE), BF16),
                        pltpu.VMEM((PB, 2 * RG_TILE), F32),
                        pltpu.VMEM((HALO + PB, RG_TILE), F32),
                        pltpu.VMEM((PB, RG_TILE), F32),
                        pltpu.VMEM((PB, 2 * RG_TILE), F32),
                        pltpu.VMEM((PB, RG_TILE), F32),
                        pltpu.VMEM((PB, RG_TILE), F32),
                        pltpu.VMEM((PB, RG_TILE), F32),
                        pltpu.VMEM((1, RG_TILE), F32)],
        compiler_params=_params("parallel", "arbitrary", "arbitrary"),
        name="rglru",
    )(xp, xp, xp, w_in, w_in, conv_w, conv_b, w_gate, ba, bx, lam)


def _s5_discretize_kernel(are_ref, aim_ref, ldt_ref, lbr_ref, lbi_ref, fr_ref, fi_ref):
    dt = jnp.exp(ldt_ref[...])
    lr = jnp.minimum(are_ref[...], -1e-4)
    li = aim_ref[...]
    mag = jnp.exp(lr * dt)
    lbr = mag * jnp.cos(li * dt)
    lbi = mag * jnp.sin(li * dt)
    zr, zi = lbr - 1.0, lbi
    den = lr * lr + li * li
    lbr_ref[...] = lbr
    lbi_ref[...] = lbi
    fr_ref[...] = (zr * lr + zi * li) / den
    fi_ref[...] = (zi * lr - zr * li) / den


def _s5_discretize(a_re, a_im, log_dt):
    shp = jax.ShapeDtypeStruct((SSM_GROUPS, SSM_STATE), F32)
    return pl.pallas_call(
        _s5_discretize_kernel,
        out_shape=(shp, shp, shp, shp),
        name="s5_discretize",
    )(a_re, a_im, log_dt.reshape(SSM_GROUPS, 1))


def _cmul(ar, ai, br, bi):
    return ar * br - ai * bi, ar * bi + ai * br


def _s5_project(u_ref, bmat_ref, xr_ref, xi_ref, cols):
    ns = S5_TILE_STATES
    u = u_ref[...]
    xr_ref[:, cols] = jnp.dot(u, bmat_ref[:, cols], preferred_element_type=F32)
    xi_ref[:, cols] = jnp.dot(u, bmat_ref[:, ns + cols.start:ns + cols.stop],
                              preferred_element_type=F32)


def _s5_scan(cols, xr_ref, xi_ref, hb_ref, lre_ref, lim_ref, tab_re_ref, tab_im_ref,
             pow_re_ref, pow_im_ref, step_re_ref, step_im_ref, cr_ref, ci_ref):
    ns = S5_TILE_STATES
    grp = (SUBLANES, S5_SCAN_LANES)
    first = _row_ids(grp) == 0
    lr = jnp.broadcast_to(lre_ref[:, cols], grp)
    li = jnp.broadcast_to(lim_ref[:, cols], grp)

    er, ei = xr_ref[0:SUBLANES, cols], xi_ref[0:SUBLANES, cols]
    for tau in range(1, SEG):
        rows = slice(tau * SUBLANES, (tau + 1) * SUBLANES)
        nr = (lr * er - li * ei) + xr_ref[rows, cols]
        ni = (lr * ei + li * er) + xi_ref[rows, cols]
        xr_ref[rows, cols] = nr
        xi_ref[rows, cols] = ni
        er, ei = nr, ni

    for k, d in enumerate((1, 2, 4)):
        sr, si = pltpu.roll(er, d, 0), pltpu.roll(ei, d, 0)
        mr, mi = _cmul(step_re_ref[k, :, cols], step_im_ref[k, :, cols], sr, si)
        er, ei = er + mr, ei + mi
    cr, ci = cr_ref[:, cols], ci_ref[:, cols]
    mr, mi = _cmul(pow_re_ref[:, cols], pow_im_ref[:, cols], cr, ci)
    er, ei = er + mr, ei + mi
    in_r = jnp.where(first, cr, pltpu.roll(er, 1, 0))
    in_i = jnp.where(first, ci, pltpu.roll(ei, 1, 0))
    cr_ref[:, cols] = er[SUBLANES - 1:SUBLANES, :]
    ci_ref[:, cols] = ei[SUBLANES - 1:SUBLANES, :]

    in_r2 = jnp.concatenate([in_r, in_r], axis=0).astype(BF16)
    in_i2 = jnp.concatenate([in_i, in_i], axis=0).astype(BF16)
    for i in range(PB // BF16_ROWS):
        rows = slice(i * BF16_ROWS, (i + 1) * BF16_ROWS)
        fr, fi = _cmul(tab_re_ref[rows, cols], tab_im_ref[rows, cols], in_r2, in_i2)
        hb_ref[rows, cols] = xr_ref[rows, cols].astype(BF16) + fr
        hb_ref[rows, ns + cols.start:ns + cols.stop] = xi_ref[rows, cols].astype(BF16) + fi


def _s5_expand_weights(bc_ref, cc_ref, sb_ref, sc_ref, bmat_ref, cmat_ref):
    ns = S5_TILE_STATES

    def block_id(shape, dim, log2_block):
        return jax.lax.shift_right_logical(jax.lax.broadcasted_iota(jnp.int32, shape, dim),
                                           log2_block)

    log2_group = SSM_GROUP.bit_length() - 1
    log2_state = SSM_STATE.bit_length() - 1
    keep_b = block_id((S5_TILE, ns), 0, log2_group) == block_id((S5_TILE, ns), 1, log2_state)
    keep_c = block_id((ns, S5_TILE), 0, log2_state) == block_id((ns, S5_TILE), 1, log2_group)
    for part in range(2):
        b_grp = bc_ref[0, :, part * SSM_STATE:(part + 1) * SSM_STATE].astype(BF16)
        rep = jnp.dot(b_grp, sb_ref[...], preferred_element_type=F32)
        bmat_ref[:, part * ns:(part + 1) * ns] = jnp.where(keep_b, rep, 0.0).astype(BF16)
        c_grp = cc_ref[0, part * ns:(part + 1) * ns, :].astype(BF16)
        rep = jnp.dot(c_grp, sc_ref[...], preferred_element_type=F32)
        cmat_ref[part * ns:(part + 1) * ns, :] = jnp.where(keep_c, rep, 0.0).astype(BF16)


def _s5_kernel(u0_ref, uc_ref, un_ref, bc_ref, cc_ref, sb_ref, sc_ref, lre_ref, lim_ref, d_ref,
               o_ref, bmat_ref, cmat_ref,
               xr_ref, xi_ref, hb_ref, tab_re_ref, tab_im_ref, tabb_re_ref, tabb_im_ref,
               pow_re_ref, pow_im_ref, step_re_ref, step_im_ref, cr_ref, ci_ref):
    t = pl.program_id(2)
    ns = S5_TILE_STATES
    grp = (SUBLANES, ns)
    chunks = [slice(c0, c0 + S5_SCAN_LANES) for c0 in range(0, ns, S5_SCAN_LANES)]

    @pl.when(t == 0)
    def _():
        _s5_expand_weights(bc_ref, cc_ref, sb_ref, sc_ref, bmat_ref, cmat_ref)
        for cols in chunks:
            _s5_project(u0_ref, bmat_ref, xr_ref, xi_ref, cols)
        cr_ref[...] = jnp.zeros_like(cr_ref)
        ci_ref[...] = jnp.zeros_like(ci_ref)
        lr = jnp.broadcast_to(lre_ref[...], grp)
        li = jnp.broadcast_to(lim_ref[...], grp)

        def fill(tau, p):
            pr, pi = p
            rows = pl.ds(pl.multiple_of(tau * SUBLANES, SUBLANES), SUBLANES)
            tab_re_ref[rows, :] = pr
            tab_im_ref[rows, :] = pi
            return _cmul(pr, pi, lr, li)

        jax.lax.fori_loop(0, SEG, fill, (lr, li))
        tabb_re_ref[...] = tab_re_ref[...].astype(BF16)
        tabb_im_ref[...] = tab_im_ref[...].astype(BF16)
        mr = tab_re_ref[PB - SUBLANES:PB, :]
        mi = tab_im_ref[PB - SUBLANES:PB, :]
        row = _row_ids(grp)
        pr, pi = mr, mi
        for r in range(SUBLANES):
            pow_re_ref[r:r + 1, :] = pr[0:1, :]
            pow_im_ref[r:r + 1, :] = pi[0:1, :]
            if r + 1 in (1, 2, 4):
                k = (1, 2, 4).index(r + 1)
                step_re_ref[k] = jnp.where(row >= r + 1, pr, 0.0)
                step_im_ref[k] = jnp.where(row >= r + 1, pi, 0.0)
            pr, pi = _cmul(pr, pi, mr, mi)

    scan_refs = (lre_ref, lim_ref, tabb_re_ref, tabb_im_ref, pow_re_ref, pow_im_ref,
                 step_re_ref, step_im_ref, cr_ref, ci_ref)
    y = d_ref[...] * uc_ref[...].astype(F32)
    for cols in chunks:
        _s5_scan(cols, xr_ref, xi_ref, hb_ref, *scan_refs)
        y = y + jnp.dot(hb_ref[:, cols], cmat_ref[cols, :], preferred_element_type=F32)
        y = y + jnp.dot(hb_ref[:, ns + cols.start:ns + cols.stop],
                        cmat_ref[ns + cols.start:ns + cols.stop, :],
                        preferred_element_type=F32)
        _s5_project(un_ref, bmat_ref, xr_ref, xi_ref, cols)
    o_ref[...] = jax.nn.gelu(y).astype(o_ref.dtype)


def _s5(z, b_grp, c_grp, lam_re, lam_im, dvec, bsz, seq):
    n_t = seq // PB
    n_tiles = D_SSM // S5_TILE
    ns = S5_TILE_STATES
    vec = lambda b, c, t: (0, c)
    tile = lambda b, c, t: (c, 0, 0)
    fixed = lambda b, c, t: (0, 0)
    blk = lambda f: pl.BlockSpec((PB, S5_TILE), lambda b, c, t: (b * n_t + f(t), c))
    rep_b = jnp.asarray(np.tile(np.eye(SSM_STATE), (1, S5_TILE_GROUPS)), BF16)
    rep_c = jnp.asarray(np.tile(np.eye(SSM_GROUP), (1, S5_TILE_GROUPS)), BF16)
    return pl.pallas_call(
        _s5_kernel,
        grid=(bsz, n_tiles, n_t),
        in_specs=[blk(lambda t: 0),
                  blk(lambda t: t),
                  blk(lambda t: jnp.minimum(t + 1, n_t - 1)),
                  pl.BlockSpec((1, S5_TILE, 2 * SSM_STATE), tile),
                  pl.BlockSpec((1, 2 * ns, SSM_GROUP), tile),
                  pl.BlockSpec((SSM_STATE, ns), fixed),
                  pl.BlockSpec((SSM_GROUP, S5_TILE), fixed),
                  pl.BlockSpec((1, ns), vec),
                  pl.BlockSpec((1, ns), vec),
                  pl.BlockSpec((1, S5_TILE), vec)],
        out_specs=blk(lambda t: t),
        out_shape=jax.ShapeDtypeStruct((bsz * seq, D_SSM), BF16),
        scratch_shapes=[pltpu.VMEM((S5_TILE, 2 * ns), BF16),
                        pltpu.VMEM((2 * ns, S5_TILE), BF16),
                        pltpu.VMEM((PB, ns), F32),
                        pltpu.VMEM((PB, ns), F32),
                        pltpu.VMEM((PB, 2 * ns), BF16),
                        pltpu.VMEM((PB, ns), F32),
                        pltpu.VMEM((PB, ns), F32),
                        pltpu.VMEM((PB, ns), BF16),
                        pltpu.VMEM((PB, ns), BF16),
                        pltpu.VMEM((SUBLANES, ns), F32),
                        pltpu.VMEM((SUBLANES, ns), F32),
                        pltpu.VMEM((3, SUBLANES, ns), F32),
                        pltpu.VMEM((3, SUBLANES, ns), F32),
                        pltpu.VMEM((1, ns), F32),
                        pltpu.VMEM((1, ns), F32)],
        compiler_params=_params("parallel", "parallel", "arbitrary"),
        name="s5",
    )(z, z, z, b_grp, c_grp, rep_b, rep_c, lam_re, lam_im, dvec)


def _mix_kernel(hg_ref, y_ref, ga_ref, gb_ref, wa_ref, gw_ref, gv_ref, wdown_ref, wout_ref,
                o_ref, wdown_bf_ref, wout_bf_ref, wab_ref, gwb_ref, gvb_ref):
    @pl.when(pl.program_id(1) == 0)
    def _():
        wab_ref[...] = wa_ref[...].astype(BF16)
        gwb_ref[...] = gw_ref[...].astype(BF16)
        gvb_ref[...] = gv_ref[...].astype(BF16)

    wdown_bf_ref[...] = wdown_ref[...].astype(BF16)
    wout_bf_ref[...] = wout_ref[...].astype(BF16)

    for c in range(o_ref.shape[0] // ROW_CHUNK):
        rows = slice(c * ROW_CHUNK, (c + 1) * ROW_CHUNK)
        y_a = jnp.dot(hg_ref[rows, :], wab_ref[...], preferred_element_type=F32)
        y = y_ref[rows, :]
        y_b = (jnp.dot(y, gwb_ref[...], preferred_element_type=F32)
               * jax.nn.sigmoid(jnp.dot(y, gvb_ref[...], preferred_element_type=F32)))
        mix = (jax.nn.sigmoid(ga_ref[rows, :].astype(F32)) * y_a
               + jax.nn.sigmoid(gb_ref[rows, :].astype(F32)) * y_b)
        o_ref[rows, :] = mix.astype(o_ref.dtype)


def _mix(hg, y, z, w_a, glu_w, glu_v, w_down, w_out, tm, tn):
    m = hg.shape[0]
    n_i = m // tm
    steps = (D_MODEL // tn) * n_i
    slab = lambda rows: pl.BlockSpec((rows // steps, D_MODEL), lambda j, i: (j * n_i + i, 0))
    return pl.pallas_call(
        _mix_kernel,
        grid=(D_MODEL // tn, n_i),
        in_specs=[pl.BlockSpec((tm, D_RNN), lambda j, i: (i, 0)),
                  pl.BlockSpec((tm, D_SSM), lambda j, i: (i, 0)),
                  pl.BlockSpec((tm, tn), lambda j, i: (i, Z_COL_GA // tn + j)),
                  pl.BlockSpec((tm, tn), lambda j, i: (i, Z_COL_GB // tn + j)),
                  pl.BlockSpec((D_RNN, tn), lambda j, i: (0, j)),
                  pl.BlockSpec((D_SSM, tn), lambda j, i: (0, j)),
                  pl.BlockSpec((D_SSM, tn), lambda j, i: (0, j)),
                  slab(D_FF),
                  slab(D_MODEL)],
        out_specs=(pl.BlockSpec((tm, tn), lambda j, i: (i, j)), slab(D_FF), slab(D_MODEL)),
        out_shape=(jax.ShapeDtypeStruct((m, D_MODEL), BF16),
                   jax.ShapeDtypeStruct((D_FF, D_MODEL), BF16),
                   jax.ShapeDtypeStruct((D_MODEL, D_MODEL), BF16)),
        scratch_shapes=[pltpu.VMEM((D_RNN, tn), BF16),
                        pltpu.VMEM((D_SSM, tn), BF16),
                        pltpu.VMEM((D_SSM, tn), BF16)],
        compiler_params=_params("arbitrary", "arbitrary"),
        name="mix",
    )(hg, y, z, z, w_a, glu_w, glu_v, w_down, w_out)


def _layernorm(v, g, b):
    mu = jnp.mean(v, axis=-1, keepdims=True)
    c = v - mu
    var = jnp.mean(c * c, axis=-1, keepdims=True)
    return c * jax.lax.rsqrt(var + LN_EPS) * g + b


def _outproj_ln_kernel(mix_ref, pt_ref, x_ref, w_ref, g_ref, b_ref, o_ref, ob_ref, *, alpha):
    for c in range(PB // ROW_CHUNK):
        rows = slice(c * ROW_CHUNK, (c + 1) * ROW_CHUNK)
        mix = jnp.dot(pt_ref[rows, :], mix_ref[...], preferred_element_type=F32).astype(BF16)
        v = alpha * x_ref[rows, :] + jnp.dot(mix, w_ref[...], preferred_element_type=F32)
        out = _layernorm(v, g_ref[...], b_ref[...])
        o_ref[rows, :] = out
        ob_ref[rows, :] = out.astype(BF16)


def _outproj_ln(mix, perm_t, x, w_out, g, b, alpha):
    m = mix.shape[0]
    row = lambda i: (i, 0)
    fixed = lambda i: (0, 0)
    return pl.pallas_call(
        functools.partial(_outproj_ln_kernel, alpha=alpha),
        grid=(m // PB,),
        in_specs=[pl.BlockSpec((PB, D_MODEL), row),
                  pl.BlockSpec((PB, PB), fixed),
                  pl.BlockSpec((PB, D_MODEL), row),
                  pl.BlockSpec((D_MODEL, D_MODEL), fixed),
                  pl.BlockSpec((1, D_MODEL), fixed),
                  pl.BlockSpec((1, D_MODEL), fixed)],
        out_specs=(pl.BlockSpec((PB, D_MODEL), row), pl.BlockSpec((PB, D_MODEL), row)),
        out_shape=(jax.ShapeDtypeStruct((m, D_MODEL), F32),
                   jax.ShapeDtypeStruct((m, D_MODEL), BF16)),
        compiler_params=_params("parallel"),
        name="outproj_ln",
    )(mix, perm_t, x, w_out, g, b)


def _mlp_up_kernel(x_ref, w_ref, b_ref, o_ref, wb_ref):
    @pl.when(pl.program_id(1) == 0)
    def _():
        wb_ref[...] = w_ref[...].astype(BF16)

    v = jnp.dot(x_ref[...], wb_ref[...], preferred_element_type=F32) + b_ref[...]
    v = jnp.maximum(v, 0.0)
    o_ref[...] = (v * v).astype(o_ref.dtype)


def _mlp_up(xb, w_up, b_up, tm, tn):
    m = xb.shape[0]
    return pl.pallas_call(
        _mlp_up_kernel,
        grid=(D_FF // tn, m // tm),
        in_specs=[pl.BlockSpec((tm, D_MODEL), lambda j, i: (i, 0)),
                  pl.BlockSpec((D_MODEL, tn), lambda j, i: (0, j)),
                  pl.BlockSpec((1, tn), lambda j, i: (0, j))],
        out_specs=pl.BlockSpec((tm, tn), lambda j, i: (i, j)),
        out_shape=jax.ShapeDtypeStruct((m, D_FF), BF16),
        scratch_shapes=[pltpu.VMEM((D_MODEL, tn), BF16)],
        compiler_params=_params("parallel", "arbitrary"),
        name="mlp_up",
    )(xb, w_up, b_up)


def _mlp_down_ln_kernel(a_ref, w_ref, x_ref, bd_ref, g_ref, b_ref, o_ref, *, alpha):
    k = pl.program_id(1)
    last = pl.num_programs(1) - 1
    slab = x_ref.shape[0]
    slab_rows = pl.ds(pl.multiple_of(k * slab, slab), slab)

    def part():
        return jnp.dot(a_ref[...], w_ref[...], preferred_element_type=F32)

    @pl.when(k == 0)
    def _():
        o_ref[...] = part()
        o_ref[slab_rows, :] += alpha * x_ref[...]

    @pl.when(jnp.logical_and(k > 0, k < last))
    def _():
        o_ref[...] += part()
        o_ref[slab_rows, :] += alpha * x_ref[...]

    @pl.when(k == last)
    def _():
        o_ref[slab_rows, :] += alpha * x_ref[...]
        for c in range(o_ref.shape[0] // ROW_CHUNK):
            rows = slice(c * ROW_CHUNK, (c + 1) * ROW_CHUNK)
            v = (o_ref[rows, :] + jnp.dot(a_ref[rows, :], w_ref[...],
                                          preferred_element_type=F32)) + bd_ref[...]
            o_ref[rows, :] = _layernorm(v, g_ref[...], b_ref[...])


def _mlp_down_ln(a, w_down, x1, b_down, g, b, alpha, tm, tk):
    m = a.shape[0]
    n_k = D_FF // tk
    fixed = lambda i, k: (0, 0)
    return pl.pallas_call(
        functools.partial(_mlp_down_ln_kernel, alpha=alpha),
        grid=(m // tm, n_k),
        in_specs=[pl.BlockSpec((tm, tk), lambda i, k: (i, k)),
                  pl.BlockSpec((tk, D_MODEL), lambda i, k: (k, 0)),
                  pl.BlockSpec((tm // n_k, D_MODEL), lambda i, k: (i * n_k + k, 0)),
                  pl.BlockSpec((1, D_MODEL), fixed),
                  pl.BlockSpec((1, D_MODEL), fixed),
                  pl.BlockSpec((1, D_MODEL), fixed)],
        out_specs=pl.BlockSpec((tm, D_MODEL), lambda i, k: (i, 0)),
        out_shape=jax.ShapeDtypeStruct((m, D_MODEL), F32),
        compiler_params=_params("parallel", "arbitrary"),
        name="mlp_down_ln",
    )(a, w_down, x1, b_down, g, b)


def _s5_group_matrices(fr, fi, b_re, b_im, c_re, c_im):
    bbr = fr[..., None] * b_re - fi[..., None] * b_im
    bbi = fr[..., None] * b_im + fi[..., None] * b_re
    n_tiles = D_SSM // S5_TILE
    b_grp = jnp.concatenate([bbr.transpose(0, 2, 1), bbi.transpose(0, 2, 1)], axis=-1)
    b_grp = b_grp.reshape(n_tiles, S5_TILE, 2 * SSM_STATE)

    def per_tile(c):
        return c.transpose(0, 2, 1).reshape(n_tiles, S5_TILE_STATES, SSM_GROUP)

    c_grp = jnp.concatenate([per_tile(c_re), per_tile(-c_im)], axis=1)
    return b_grp, c_grp


def kernel(x, w_in, conv_w, conv_b, rg_wa, rg_ba, rg_wx, rg_bx, rg_lambda, w_a_out, ssm_a_re, ssm_a_im, ssm_log_dt, ssm_b_re, ssm_b_im, ssm_c_re, ssm_c_im, ssm_d, glu_w, glu_v, w_out, ln1_g, ln1_b, mlp_w_up, mlp_b_up, mlp_w_down, mlp_b_down, ln2_g, ln2_b):
    bsz, seq, _ = x.shape
    assert seq % PB == 0
    m = bsz * seq
    depth = w_in.shape[0]
    alpha = (2.0 * depth) ** 0.25
    perm = _interleave_matrix()
    perm_fwd = jnp.asarray(perm, BF16)
    perm_bwd = jnp.asarray(perm.T, BF16)
    for l in range(depth):
        xf = x.reshape(m, D_MODEL)
        xp = _interleave(xf, perm_fwd)
        z = _in_proj(xp, w_in[l], tm=2048, tn=1024)

        w_gate = jnp.concatenate([rg_wa[l], rg_wx[l]], axis=-1).astype(BF16)
        hg = _rglru(xp, w_in[l], conv_w[l], conv_b[l].reshape(1, D_RNN), w_gate,
                    rg_ba[l].reshape(1, D_RNN), rg_bx[l].reshape(1, D_RNN),
                    rg_lambda[l].reshape(1, D_RNN), bsz, seq)

        lbr, lbi, fr, fi = _s5_discretize(ssm_a_re[l], ssm_a_im[l], ssm_log_dt[l])
        b_grp, c_grp = _s5_group_matrices(fr, fi, ssm_b_re[l], ssm_b_im[l],
                                          ssm_c_re[l], ssm_c_im[l])
        y = _s5(z, b_grp, c_grp, lbr.reshape(1, -1), lbi.reshape(1, -1),
                ssm_d[l].reshape(1, D_SSM), bsz, seq)

        mix, w_down_bf, w_out_bf = _mix(hg, y, z, w_a_out[l], glu_w[l], glu_v[l],
                                        mlp_w_down[l], w_out[l], tm=1024, tn=512)
        x1, x1b = _outproj_ln(mix, perm_bwd, xf, w_out_bf,
                              ln1_g[l].reshape(1, D_MODEL), ln1_b[l].reshape(1, D_MODEL),
                              alpha)
        a = _mlp_up(x1b, mlp_w_up[l], mlp_b_up[l].reshape(1, D_FF), tm=2048, tn=1024)
        x2 = _mlp_down_ln(a, w_down_bf, x1,
                          mlp_b_down[l].reshape(1, D_MODEL),
                          ln2_g[l].reshape(1, D_MODEL), ln2_b[l].reshape(1, D_MODEL),
                          alpha, tm=1024, tk=2048)
        x = x2.reshape(bsz, seq, D_MODEL)
    return x
```

```python
import functools

import numpy as np
import jax
import jax.numpy as jnp
from jax.experimental import pallas as pl
from jax.experimental.pallas import tpu as pltpu

F32 = jnp.float32
BF16 = jnp.bfloat16

D_MODEL = 2048
D_RNN = D_MODEL
RG_HEADS = 16
RG_HEAD_DIM = D_RNN // RG_HEADS
CONV_WIDTH = 4
RG_C = 8.0
D_SSM = D_MODEL // 2
SSM_GROUP = 16
SSM_GROUPS = D_SSM // SSM_GROUP
SSM_STATE = 64
D_FF = 4 * D_MODEL
D_IN = 2 * D_RNN + D_SSM + 2 * D_MODEL
LN_EPS = 1e-5

SUBLANES = 8
BF16_ROWS = 2 * SUBLANES
VMEM_LIMIT = 56 * 1024 * 1024

NSEG = SUBLANES
PB = 512
SEG = PB // NSEG
HALO = (CONV_WIDTH - 1) * SUBLANES
ROW_CHUNK = 256

RG_TILE = 512
RG_GATE_ROWS = 32
S5_TILE = 256
S5_TILE_GROUPS = S5_TILE // SSM_GROUP
S5_TILE_STATES = S5_TILE_GROUPS * SSM_STATE
S5_SCAN_LANES = 256
W_COL_GATE = D_RNN
W_COL_REST = 2 * D_RNN
Z_COLS = D_SSM + 2 * D_MODEL
Z_COL_GA = D_SSM
Z_COL_GB = D_SSM + D_MODEL


def _params(*sem):
    return pltpu.CompilerParams(dimension_semantics=sem, vmem_limit_bytes=VMEM_LIMIT)


def _interleave_matrix():
    p = np.arange(PB)
    src = (p % NSEG) * SEG + p // NSEG
    mat = np.zeros((PB, PB), np.float32)
    mat[p, src] = 1.0
    return mat


def _row_ids(shape):
    return jax.lax.broadcasted_iota(jnp.int32, shape, 0)


def _shift_rows(v, d, fill):
    return jnp.where(_row_ids(v.shape) >= d, pltpu.roll(v, d, 0), fill)


def _interleave_kernel(x_ref, p_ref, o_ref):
    for s in range(x_ref.shape[0] // PB):
        rows = slice(s * PB, (s + 1) * PB)
        o_ref[rows, :] = jnp.dot(p_ref[...], x_ref[rows, :].astype(BF16),
                                 preferred_element_type=F32).astype(BF16)


def _interleave(x, perm, blocks_per_step=2):
    m, k = x.shape
    rows = blocks_per_step * PB
    return pl.pallas_call(
        _interleave_kernel,
        grid=(m // rows,),
        in_specs=[pl.BlockSpec((rows, k), lambda i: (i, 0)),
                  pl.BlockSpec((PB, PB), lambda i: (0, 0))],
        out_specs=pl.BlockSpec((rows, k), lambda i: (i, 0)),
        out_shape=jax.ShapeDtypeStruct((m, k), BF16),
        compiler_params=_params("parallel"),
        name="interleave",
    )(x, perm)


def _in_proj_kernel(x_ref, w_ref, o_ref, wb_ref):
    @pl.when(pl.program_id(1) == 0)
    def _():
        wb_ref[...] = w_ref[...].astype(BF16)

    o_ref[...] = jnp.dot(x_ref[...], wb_ref[...],
                         preferred_element_type=F32).astype(o_ref.dtype)


def _in_proj(xp, w, tm, tn):
    m, k = xp.shape
    col0 = W_COL_REST // tn
    return pl.pallas_call(
        _in_proj_kernel,
        grid=(Z_COLS // tn, m // tm),
        in_specs=[pl.BlockSpec((tm, k), lambda j, i: (i, 0)),
                  pl.BlockSpec((k, tn), lambda j, i: (0, col0 + j))],
        out_specs=pl.BlockSpec((tm, tn), lambda j, i: (i, j)),
        out_shape=jax.ShapeDtypeStruct((m, Z_COLS), BF16),
        scratch_shapes=[pltpu.VMEM((k, tn), BF16)],
        compiler_params=_params("parallel", "arbitrary"),
        name="in_proj",
    )(xp, w)


def _rglru_gates(hh, z_ref, cw_ref, cb_ref, w_ref, ext_ref, xc_ref, pre_ref, gg_ref):
    sl = slice(hh * RG_HEAD_DIM, (hh + 1) * RG_HEAD_DIM)
    gg_ref[:, sl] = jax.nn.gelu(
        z_ref[:, RG_TILE + hh * RG_HEAD_DIM:RG_TILE + (hh + 1) * RG_HEAD_DIM])
    grp = (SUBLANES, RG_HEAD_DIM)
    x = z_ref[:, sl]
    first = _row_ids(grp) == 0
    halos = []
    for k in range(CONV_WIDTH - 1):
        prev_g = ext_ref[PB + k * SUBLANES:PB + (k + 1) * SUBLANES, sl]
        cur_g = x[PB - HALO + k * SUBLANES:PB - HALO + (k + 1) * SUBLANES, :]
        halos.append(jnp.where(first, pltpu.roll(prev_g, 1, 0), pltpu.roll(cur_g, 1, 0)))
    for k in range(CONV_WIDTH - 1):
        ext_ref[k * SUBLANES:(k + 1) * SUBLANES, sl] = halos[k]
    ext_ref[HALO:HALO + PB, sl] = x

    xc = cb_ref[:, sl] + cw_ref[CONV_WIDTH - 1:CONV_WIDTH, sl] * x
    for k in range(CONV_WIDTH - 1):
        xc = xc + cw_ref[k:k + 1, sl] * ext_ref[k * SUBLANES:k * SUBLANES + PB, sl]

    xc_ref[:, sl] = xc
    pre_ref[:, 2 * hh * RG_HEAD_DIM:2 * (hh + 1) * RG_HEAD_DIM] = jnp.dot(
        xc.astype(BF16), w_ref[hh], preferred_element_type=F32)


def _rglru_head(hh, o_ref, row0, cvec, ba_ref, bx_ref, xc_ref, pre_ref, gg_ref,
                a_ref, b_ref, carry_ref):
    sl = slice(hh * RG_HEAD_DIM, (hh + 1) * RG_HEAD_DIM)
    grp = (SUBLANES, RG_HEAD_DIM)
    first = _row_ids(grp) == 0
    for c in range(PB // RG_GATE_ROWS):
        rows = slice(c * RG_GATE_ROWS, (c + 1) * RG_GATE_ROWS)
        xc = xc_ref[rows, sl]
        r = jax.nn.sigmoid(pre_ref[rows, 2 * hh * RG_HEAD_DIM:(2 * hh + 1) * RG_HEAD_DIM]
                           + ba_ref[:, sl])
        i = jax.nn.sigmoid(pre_ref[rows, (2 * hh + 1) * RG_HEAD_DIM:(2 * hh + 2) * RG_HEAD_DIM]
                           + bx_ref[:, sl])
        log_a = cvec[:, sl] * r
        a = jnp.exp(log_a)
        a_ref[rows, sl] = a
        one_minus_a2 = -jnp.tanh(log_a) * (1.0 + a * a)
        b_ref[rows, sl] = jnp.sqrt(one_minus_a2) * (i * xc)

    h_end, a_end = b_ref[0:SUBLANES, sl], a_ref[0:SUBLANES, sl]
    for tau in range(1, SEG):
        rows = slice(tau * SUBLANES, (tau + 1) * SUBLANES)
        a = a_ref[rows, sl]
        h_end = a * h_end + b_ref[rows, sl]
        a_end = a * a_end
        b_ref[rows, sl] = h_end
        a_ref[rows, sl] = a_end

    for d in (1, 2, 4):
        h_end = a_end * _shift_rows(h_end, d, 0.0) + h_end
        a_end = a_end * _shift_rows(a_end, d, 1.0)
    carry = carry_ref[:, sl]
    h_true = h_end + a_end * carry
    h_in = jnp.where(first, carry, pltpu.roll(h_true, 1, 0))
    carry_ref[:, sl] = h_true[SUBLANES - 1:SUBLANES, :]

    h_in2 = jnp.concatenate([h_in, h_in], axis=0)
    for j in range(PB // BF16_ROWS):
        rows = slice(j * BF16_ROWS, (j + 1) * BF16_ROWS)
        h = b_ref[rows, sl] + a_ref[rows, sl] * h_in2
        o_ref[row0 + j * BF16_ROWS:row0 + (j + 1) * BF16_ROWS, sl] = (
            h * gg_ref[rows, sl]).astype(o_ref.dtype)


def _rglru_kernel(x0_ref, x1_ref, x2_ref, wx_ref, wg_ref, cw_ref, cb_ref, w_ref,
                  ba_ref, bx_ref, lam_ref, o_ref,
                  wb_ref, z_ref, ext_ref, xca_ref, prea_ref, gga_ref, a_ref, b_ref, carry_ref):
    t = pl.program_id(2)
    n_heads = RG_TILE // RG_HEAD_DIM
    bufs = (xca_ref, prea_ref, gga_ref)

    def project(x_ref):
        z_ref[...] = jnp.dot(x_ref[...], wb_ref[...], preferred_element_type=F32)

    def gates():
        for hh in range(n_heads):
            _rglru_gates(hh, z_ref, cw_ref, cb_ref, w_ref, ext_ref, *bufs)

    @pl.when(jnp.logical_and(t == 0, pl.program_id(1) == 0))
    def _():
        wb_ref[:, :RG_TILE] = wx_ref[...].astype(BF16)
        wb_ref[:, RG_TILE:] = wg_ref[...].astype(BF16)

    @pl.when(t == 0)
    def _():
        ext_ref[PB:PB + HALO, :] = jnp.zeros((HALO, RG_TILE), F32)
        carry_ref[...] = jnp.zeros_like(carry_ref)
        project(x0_ref)
        gates()

    neg = -lam_ref[...]
    softplus = jnp.maximum(neg, 0.0) + jnp.log1p(jnp.exp(-jnp.abs(neg)))
    cvec = -RG_C * softplus

    def back(row0):
        for hh in range(n_heads):
            _rglru_head(hh, o_ref, row0, cvec, ba_ref, bx_ref, *bufs, a_ref, b_ref, carry_ref)

    project(x1_ref)
    back(0)
    gates()
    project(x2_ref)
    back(PB)
    gates()


def _rglru(xp, w_in, conv_w, conv_b, w_gate, ba, bx, lam, bsz, seq):
    n_t = seq // PB
    assert n_t % 2 == 0
    n_c = D_RNN // RG_TILE
    vec = lambda c, b, t: (0, c)
    return pl.pallas_call(
        _rglru_kernel,
        grid=(n_c, bsz, n_t // 2),
        in_specs=[pl.BlockSpec((PB, D_MODEL), lambda c, b, t: (b * n_t, 0)),
                  pl.BlockSpec((PB, D_MODEL), lambda c, b, t: (b * n_t + 2 * t + 1, 0)),
                  pl.BlockSpec((PB, D_MODEL),
                               lambda c, b, t: (b * n_t + jnp.minimum(2 * t + 2, n_t - 1), 0)),
                  pl.BlockSpec((D_MODEL, RG_TILE), vec),
                  pl.BlockSpec((D_MODEL, RG_TILE), lambda c, b, t: (0, W_COL_GATE // RG_TILE + c)),
                  pl.BlockSpec((CONV_WIDTH, RG_TILE), vec),
                  pl.BlockSpec((1, RG_TILE), vec),
                  pl.BlockSpec((RG_TILE // RG_HEAD_DIM, RG_HEAD_DIM, 2 * RG_HEAD_DIM),
                               lambda c, b, t: (c, 0, 0)),
                  pl.BlockSpec((1, RG_TILE), vec),
                  pl.BlockSpec((1, RG_TILE), vec),
                  pl.BlockSpec((1, RG_TILE), vec)],
        out_specs=pl.BlockSpec((2 * PB, RG_TILE), lambda c, b, t: (b * (n_t // 2) + t, c)),
        out_shape=jax.ShapeDtypeStruct((bsz * seq, D_RNN), BF16),
        scratch_shapes=[pltpu.VMEM((D_MODEL, 2 * RG_TILE), BF16),
                        pltpu.VMEM((PB, 2 * RG_TILE), F32),
                        pltpu.VMEM((HALO + PB, RG_TILE), F32),
                        pltpu.VMEM((PB, RG_TILE), F32),
                        pltpu.VMEM((PB, 2 * RG_TILE), F32),
                        pltpu.VMEM((PB, RG_TILE), F32),
                        pltpu.VMEM((PB, RG_TILE), F32),
                        pltpu.VMEM((PB, RG_TILE), F32),
                        pltpu.VMEM((1, RG_TILE), F32)],
        compiler_params=_params("parallel", "arbitrary", "arbitrary"),
        name="rglru",
    )(xp, xp, xp, w_in, w_in, conv_w, conv_b, w_gate, ba, bx, lam)


def _s5_discretize_kernel(are_ref, aim_ref, ldt_ref, lbr_ref, lbi_ref, fr_ref, fi_ref):
    dt = jnp.exp(ldt_ref[...])
    lr = jnp.minimum(are_ref[...], -1e-4)
    li = aim_ref[...]
    mag = jnp.exp(lr * dt)
    lbr = mag * jnp.cos(li * dt)
    lbi = mag * jnp.sin(li * dt)
    zr, zi = lbr - 1.0, lbi
    den = lr * lr + li * li
    lbr_ref[...] = lbr
    lbi_ref[...] = lbi
    fr_ref[...] = (zr * lr + zi * li) / den
    fi_ref[...] = (zi * lr - zr * li) / den


def _s5_discretize(a_re, a_im, log_dt):
    shp = jax.ShapeDtypeStruct((SSM_GROUPS, SSM_STATE), F32)
    return pl.pallas_call(
        _s5_discretize_kernel,
        out_shape=(shp, shp, shp, shp),
        name="s5_discretize",
    )(a_re, a_im, log_dt.reshape(SSM_GROUPS, 1))


def _cmul(ar, ai, br, bi):
    return ar * br - ai * bi, ar * bi + ai * br


def _s5_project(u_ref, bmat_ref, xr_ref, xi_ref, cols):
    ns = S5_TILE_STATES
    u = u_ref[...]
    xr_ref[:, cols] = jnp.dot(u, bmat_ref[:, cols], preferred_element_type=F32)
    xi_ref[:, cols] = jnp.dot(u, bmat_ref[:, ns + cols.start:ns + cols.stop],
                              preferred_element_type=F32)


def _s5_scan(cols, xr_ref, xi_ref, hb_ref, lre_ref, lim_ref, tab_re_ref, tab_im_ref,
             pow_re_ref, pow_im_ref, step_re_ref, step_im_ref, cr_ref, ci_ref):
    ns = S5_TILE_STATES
    grp = (SUBLANES, S5_SCAN_LANES)
    first = _row_ids(grp) == 0
    lr = jnp.broadcast_to(lre_ref[:, cols], grp)
    li = jnp.broadcast_to(lim_ref[:, cols], grp)

    er, ei = xr_ref[0:SUBLANES, cols], xi_ref[0:SUBLANES, cols]
    for tau in range(1, SEG):
        rows = slice(tau * SUBLANES, (tau + 1) * SUBLANES)
        nr = (lr * er - li * ei) + xr_ref[rows, cols]
        ni = (lr * ei + li * er) + xi_ref[rows, cols]
        xr_ref[rows, cols] = nr
        xi_ref[rows, cols] = ni
        er, ei = nr, ni

    for k, d in enumerate((1, 2, 4)):
        sr, si = pltpu.roll(er, d, 0), pltpu.roll(ei, d, 0)
        mr, mi = _cmul(step_re_ref[k, :, cols], step_im_ref[k, :, cols], sr, si)
        er, ei = er + mr, ei + mi
    cr, ci = cr_ref[:, cols], ci_ref[:, cols]
    mr, mi = _cmul(pow_re_ref[:, cols], pow_im_ref[:, cols], cr, ci)
    er, ei = er + mr, ei + mi
    in_r = jnp.where(first, cr, pltpu.roll(er, 1, 0))
    in_i = jnp.where(first, ci, pltpu.roll(ei, 1, 0))
    cr_ref[:, cols] = er[SUBLANES - 1:SUBLANES, :]
    ci_ref[:, cols] = ei[SUBLANES - 1:SUBLANES, :]

    in_r2 = jnp.concatenate([in_r, in_r], axis=0).astype(BF16)
    in_i2 = jnp.concatenate([in_i, in_i], axis=0).astype(BF16)
    for i in range(PB // BF16_ROWS):
        rows = slice(i * BF16_ROWS, (i + 1) * BF16_ROWS)
        fr, fi = _cmul(tab_re_ref[rows, cols], tab_im_ref[rows, cols], in_r2, in_i2)
        hb_ref[rows, cols] = xr_ref[rows, cols].astype(BF16) + fr
        hb_ref[rows, ns + cols.start:ns + cols.stop] = xi_ref[rows, cols].astype(BF16) + fi


def _s5_expand_weights(bc_ref, cc_ref, sb_ref, sc_ref, bmat_ref, cmat_ref):
    ns = S5_TILE_STATES

    def block_id(shape, dim, log2_block):
        return jax.lax.shift_right_logical(jax.lax.broadcasted_iota(jnp.int32, shape, dim),
                                           log2_block)

    log2_group = SSM_GROUP.bit_length() - 1
    log2_state = SSM_STATE.bit_length() - 1
    keep_b = block_id((S5_TILE, ns), 0, log2_group) == block_id((S5_TILE, ns), 1, log2_state)
    keep_c = block_id((ns, S5_TILE), 0, log2_state) == block_id((ns, S5_TILE), 1, log2_group)
    for part in range(2):
        b_grp = bc_ref[0, :, part * SSM_STATE:(part + 1) * SSM_STATE].astype(BF16)
        rep = jnp.dot(b_grp, sb_ref[...], preferred_element_type=F32)
        bmat_ref[:, part * ns:(part + 1) * ns] = jnp.where(keep_b, rep, 0.0).astype(BF16)
        c_grp = cc_ref[0, part * ns:(part + 1) * ns, :].astype(BF16)
        rep = jnp.dot(c_grp, sc_ref[...], preferred_element_type=F32)
        cmat_ref[part * ns:(part + 1) * ns, :] = jnp.where(keep_c, rep, 0.0).astype(BF16)


def _s5_kernel(u0_ref, uc_ref, un_ref, bc_ref, cc_ref, sb_ref, sc_ref, lre_ref, lim_ref, d_ref,
               o_ref, bmat_ref, cmat_ref,
               xr_ref, xi_ref, hb_ref, tab_re_ref, tab_im_ref, tabb_re_ref, tabb_im_ref,
               pow_re_ref, pow_im_ref, step_re_ref, step_im_ref, cr_ref, ci_ref):
    t = pl.program_id(2)
    ns = S5_TILE_STATES
    grp = (SUBLANES, ns)
    chunks = [slice(c0, c0 + S5_SCAN_LANES) for c0 in range(0, ns, S5_SCAN_LANES)]

    @pl.when(t == 0)
    def _():
        _s5_expand_weights(bc_ref, cc_ref, sb_ref, sc_ref, bmat_ref, cmat_ref)
        for cols in chunks:
            _s5_project(u0_ref, bmat_ref, xr_ref, xi_ref, cols)
        cr_ref[...] = jnp.zeros_like(cr_ref)
        ci_ref[...] = jnp.zeros_like(ci_ref)
        lr = jnp.broadcast_to(lre_ref[...], grp)
        li = jnp.broadcast_to(lim_ref[...], grp)

        def fill(tau, p):
            pr, pi = p
            rows = pl.ds(pl.multiple_of(tau * SUBLANES, SUBLANES), SUBLANES)
            tab_re_ref[rows, :] = pr
            tab_im_ref[rows, :] = pi
            return _cmul(pr, pi, lr, li)

        jax.lax.fori_loop(0, SEG, fill, (lr, li))
        tabb_re_ref[...] = tab_re_ref[...].astype(BF16)
        tabb_im_ref[...] = tab_im_ref[...].astype(BF16)
        mr = tab_re_ref[PB - SUBLANES:PB, :]
        mi = tab_im_ref[PB - SUBLANES:PB, :]
        row = _row_ids(grp)
        pr, pi = mr, mi
        for r in range(SUBLANES):
            pow_re_ref[r:r + 1, :] = pr[0:1, :]
            pow_im_ref[r:r + 1, :] = pi[0:1, :]
            if r + 1 in (1, 2, 4):
                k = (1, 2, 4).index(r + 1)
                step_re_ref[k] = jnp.where(row >= r + 1, pr, 0.0)
                step_im_ref[k] = jnp.where(row >= r + 1, pi, 0.0)
            pr, pi = _cmul(pr, pi, mr, mi)

    scan_refs = (lre_ref, lim_ref, tabb_re_ref, tabb_im_ref, pow_re_ref, pow_im_ref,
                 step_re_ref, step_im_ref, cr_ref, ci_ref)
    y = d_ref[...] * uc_ref[...].astype(F32)
    for cols in chunks:
        _s5_scan(cols, xr_ref, xi_ref, hb_ref, *scan_refs)
        y = y + jnp.dot(hb_ref[:, cols], cmat_ref[cols, :], preferred_element_type=F32)
        y = y + jnp.dot(hb_ref[:, ns + cols.start:ns + cols.stop],
                        cmat_ref[ns + cols.start:ns + cols.stop, :],
                        preferred_element_type=F32)
        _s5_project(un_ref, bmat_ref, xr_ref, xi_ref, cols)
    o_ref[...] = jax.nn.gelu(y).astype(o_ref.dtype)


def _s5(z, b_grp, c_grp, lam_re, lam_im, dvec, bsz, seq):
    n_t = seq // PB
    n_tiles = D_SSM // S5_TILE
    ns = S5_TILE_STATES
    vec = lambda b, c, t: (0, c)
    tile = lambda b, c, t: (c, 0, 0)
    fixed = lambda b, c, t: (0, 0)
    blk = lambda f: pl.BlockSpec((PB, S5_TILE), lambda b, c, t: (b * n_t + f(t), c))
    rep_b = jnp.asarray(np.tile(np.eye(SSM_STATE), (1, S5_TILE_GROUPS)), BF16)
    rep_c = jnp.asarray(np.tile(np.eye(SSM_GROUP), (1, S5_TILE_GROUPS)), BF16)
    return pl.pallas_call(
        _s5_kernel,
        grid=(bsz, n_tiles, n_t),
        in_specs=[blk(lambda t: 0),
                  blk(lambda t: t),
                  blk(lambda t: jnp.minimum(t + 1, n_t - 1)),
                  pl.BlockSpec((1, S5_TILE, 2 * SSM_STATE), tile),
                  pl.BlockSpec((1, 2 * ns, SSM_GROUP), tile),
                  pl.BlockSpec((SSM_STATE, ns), fixed),
                  pl.BlockSpec((SSM_GROUP, S5_TILE), fixed),
                  pl.BlockSpec((1, ns), vec),
                  pl.BlockSpec((1, ns), vec),
                  pl.BlockSpec((1, S5_TILE), vec)],
        out_specs=blk(lambda t: t),
        out_shape=jax.ShapeDtypeStruct((bsz * seq, D_SSM), BF16),
        scratch_shapes=[pltpu.VMEM((S5_TILE, 2 * ns), BF16),
                        pltpu.VMEM((2 * ns, S5_TILE), BF16),
                        pltpu.VMEM((PB, ns), F32),
                        pltpu.VMEM((PB, ns), F32),
                        pltpu.VMEM((PB, 2 * ns), BF16),
                        pltpu.VMEM((PB, ns), F32),
                        pltpu.VMEM((PB, ns), F32),
                        pltpu.VMEM((PB, ns), BF16),
                        pltpu.VMEM((PB, ns), BF16),
                        pltpu.VMEM((SUBLANES, ns), F32),
                        pltpu.VMEM((SUBLANES, ns), F32),
                        pltpu.VMEM((3, SUBLANES, ns), F32),
                        pltpu.VMEM((3, SUBLANES, ns), F32),
                        pltpu.VMEM((1, ns), F32),
                        pltpu.VMEM((1, ns), F32)],
        compiler_params=_params("parallel", "parallel", "arbitrary"),
        name="s5",
    )(z, z, z, b_grp, c_grp, rep_b, rep_c, lam_re, lam_im, dvec)


def _mix_kernel(hg_ref, y_ref, ga_ref, gb_ref, wa_ref, gw_ref, gv_ref,
                wup_ref, wdown_ref, wout_ref,
                o_ref, wup_bf_ref, wdown_bf_ref, wout_bf_ref, wab_ref, gwb_ref, gvb_ref):
    @pl.when(pl.program_id(1) == 0)
    def _():
        wab_ref[...] = wa_ref[...].astype(BF16)
        gwb_ref[...] = gw_ref[...].astype(BF16)
        gvb_ref[...] = gv_ref[...].astype(BF16)

    wup_bf_ref[...] = wup_ref[...].astype(BF16)
    wdown_bf_ref[...] = wdown_ref[...].astype(BF16)
    wout_bf_ref[...] = wout_ref[...].astype(BF16)

    for c in range(o_ref.shape[0] // ROW_CHUNK):
        rows = slice(c * ROW_CHUNK, (c + 1) * ROW_CHUNK)
        y_a = jnp.dot(hg_ref[rows, :], wab_ref[...], preferred_element_type=F32)
        y = y_ref[rows, :]
        y_b = (jnp.dot(y, gwb_ref[...], preferred_element_type=F32)
               * jax.nn.sigmoid(jnp.dot(y, gvb_ref[...], preferred_element_type=F32)))
        mix = (jax.nn.sigmoid(ga_ref[rows, :].astype(F32)) * y_a
               + jax.nn.sigmoid(gb_ref[rows, :].astype(F32)) * y_b)
        o_ref[rows, :] = mix.astype(o_ref.dtype)


def _mix(hg, y, z, w_a, glu_w, glu_v, w_up, w_down, w_out, tm, tn):
    m = hg.shape[0]
    n_i = m // tm
    steps = (D_MODEL // tn) * n_i
    slab = lambda rows, cols=D_MODEL: pl.BlockSpec((rows // steps, cols),
                                                   lambda j, i: (j * n_i + i, 0))
    return pl.pallas_call(
        _mix_kernel,
        grid=(D_MODEL // tn, n_i),
        in_specs=[pl.BlockSpec((tm, D_RNN), lambda j, i: (i, 0)),
                  pl.BlockSpec((tm, D_SSM), lambda j, i: (i, 0)),
                  pl.BlockSpec((tm, tn), lambda j, i: (i, Z_COL_GA // tn + j)),
                  pl.BlockSpec((tm, tn), lambda j, i: (i, Z_COL_GB // tn + j)),
                  pl.BlockSpec((D_RNN, tn), lambda j, i: (0, j)),
                  pl.BlockSpec((D_SSM, tn), lambda j, i: (0, j)),
                  pl.BlockSpec((D_SSM, tn), lambda j, i: (0, j)),
                  slab(D_MODEL, D_FF),
                  slab(D_FF),
                  slab(D_MODEL)],
        out_specs=(pl.BlockSpec((tm, tn), lambda j, i: (i, j)),
                   slab(D_MODEL, D_FF), slab(D_FF), slab(D_MODEL)),
        out_shape=(jax.ShapeDtypeStruct((m, D_MODEL), BF16),
                   jax.ShapeDtypeStruct((D_MODEL, D_FF), BF16),
                   jax.ShapeDtypeStruct((D_FF, D_MODEL), BF16),
                   jax.ShapeDtypeStruct((D_MODEL, D_MODEL), BF16)),
        scratch_shapes=[pltpu.VMEM((D_RNN, tn), BF16),
                        pltpu.VMEM((D_SSM, tn), BF16),
                        pltpu.VMEM((D_SSM, tn), BF16)],
        compiler_params=_params("arbitrary", "arbitrary"),
        name="mix",
    )(hg, y, z, z, w_a, glu_w, glu_v, w_up, w_down, w_out)


def _layernorm(v, g, b):
    mu = jnp.mean(v, axis=-1, keepdims=True)
    c = v - mu
    var = jnp.mean(c * c, axis=-1, keepdims=True)
    return c * jax.lax.rsqrt(var + LN_EPS) * g + b


def _outproj_ln_kernel(mix_ref, pt_ref, x_ref, w_ref, g_ref, b_ref, o_ref, ob_ref, *, alpha):
    for c in range(PB // ROW_CHUNK):
        rows = slice(c * ROW_CHUNK, (c + 1) * ROW_CHUNK)
        mix = jnp.dot(pt_ref[rows, :], mix_ref[...], preferred_element_type=F32).astype(BF16)
        v = alpha * x_ref[rows, :] + jnp.dot(mix, w_ref[...], preferred_element_type=F32)
        out = _layernorm(v, g_ref[...], b_ref[...])
        o_ref[rows, :] = out
        ob_ref[rows, :] = out.astype(BF16)


def _outproj_ln(mix, perm_t, x, w_out, g, b, alpha):
    m = mix.shape[0]
    row = lambda i: (i, 0)
    fixed = lambda i: (0, 0)
    return pl.pallas_call(
        functools.partial(_outproj_ln_kernel, alpha=alpha),
        grid=(m // PB,),
        in_specs=[pl.BlockSpec((PB, D_MODEL), row),
                  pl.BlockSpec((PB, PB), fixed),
                  pl.BlockSpec((PB, D_MODEL), row),
                  pl.BlockSpec((D_MODEL, D_MODEL), fixed),
                  pl.BlockSpec((1, D_MODEL), fixed),
                  pl.BlockSpec((1, D_MODEL), fixed)],
        out_specs=(pl.BlockSpec((PB, D_MODEL), row), pl.BlockSpec((PB, D_MODEL), row)),
        out_shape=(jax.ShapeDtypeStruct((m, D_MODEL), F32),
                   jax.ShapeDtypeStruct((m, D_MODEL), BF16)),
        compiler_params=_params("parallel"),
        name="outproj_ln",
    )(mix, perm_t, x, w_out, g, b)


def _mlp_up_kernel(x_ref, w_ref, b_ref, o_ref):
    v = jnp.dot(x_ref[...], w_ref[...], preferred_element_type=F32) + b_ref[...]
    v = jnp.maximum(v, 0.0)
    o_ref[...] = (v * v).astype(o_ref.dtype)


def _mlp_up(xb, w_up, b_up, tm, tn):
    m = xb.shape[0]
    return pl.pallas_call(
        _mlp_up_kernel,
        grid=(D_FF // tn, m // tm),
        in_specs=[pl.BlockSpec((tm, D_MODEL), lambda j, i: (i, 0)),
                  pl.BlockSpec((D_MODEL, tn), lambda j, i: (0, j)),
                  pl.BlockSpec((1, tn), lambda j, i: (0, j))],
        out_specs=pl.BlockSpec((tm, tn), lambda j, i: (i, j)),
        out_shape=jax.ShapeDtypeStruct((m, D_FF), BF16),
        compiler_params=_params("parallel", "arbitrary"),
        name="mlp_up",
    )(xb, w_up, b_up)


def _mlp_down_ln_kernel(a_ref, w_ref, x_ref, bd_ref, g_ref, b_ref, o_ref, *, alpha):
    k = pl.program_id(1)
    last = pl.num_programs(1) - 1
    slab = x_ref.shape[0]
    slab_rows = pl.ds(pl.multiple_of(k * slab, slab), slab)

    def part():
        return jnp.dot(a_ref[...], w_ref[...], preferred_element_type=F32)

    @pl.when(k == 0)
    def _():
        o_ref[...] = part()
        o_ref[slab_rows, :] += alpha * x_ref[...]

    @pl.when(jnp.logical_and(k > 0, k < last))
    def _():
        o_ref[...] += part()
        o_ref[slab_rows, :] += alpha * x_ref[...]

    @pl.when(k == last)
    def _():
        o_ref[slab_rows, :] += alpha * x_ref[...]
        for c in range(o_ref.shape[0] // ROW_CHUNK):
            rows = slice(c * ROW_CHUNK, (c + 1) * ROW_CHUNK)
            v = (o_ref[rows, :] + jnp.dot(a_ref[rows, :], w_ref[...],
                                          preferred_element_type=F32)) + bd_ref[...]
            o_ref[rows, :] = _layernorm(v, g_ref[...], b_ref[...])


def _mlp_down_ln(a, w_down, x1, b_down, g, b, alpha, tm, tk):
    m = a.shape[0]
    n_k = D_FF // tk
    fixed = lambda i, k: (0, 0)
    return pl.pallas_call(
        functools.partial(_mlp_down_ln_kernel, alpha=alpha),
        grid=(m // tm, n_k),
        in_specs=[pl.BlockSpec((tm, tk), lambda i, k: (i, k)),
                  pl.BlockSpec((tk, D_MODEL), lambda i, k: (k, 0)),
                  pl.BlockSpec((tm // n_k, D_MODEL), lambda i, k: (i * n_k + k, 0)),
                  pl.BlockSpec((1, D_MODEL), fixed),
                  pl.BlockSpec((1, D_MODEL), fixed),
                  pl.BlockSpec((1, D_MODEL), fixed)],
        out_specs=pl.BlockSpec((tm, D_MODEL), lambda i, k: (i, 0)),
        out_shape=jax.ShapeDtypeStruct((m, D_MODEL), F32),
        compiler_params=_params("parallel", "arbitrary"),
        name="mlp_down_ln",
    )(a, w_down, x1, b_down, g, b)


def _s5_group_matrices(fr, fi, b_re, b_im, c_re, c_im):
    bbr = fr[..., None] * b_re - fi[..., None] * b_im
    bbi = fr[..., None] * b_im + fi[..., None] * b_re
    n_tiles = D_SSM // S5_TILE
    b_grp = jnp.concatenate([bbr.transpose(0, 2, 1), bbi.transpose(0, 2, 1)], axis=-1)
    b_grp = b_grp.reshape(n_tiles, S5_TILE, 2 * SSM_STATE)

    def per_tile(c):
        return c.transpose(0, 2, 1).reshape(n_tiles, S5_TILE_STATES, SSM_GROUP)

    c_grp = jnp.concatenate([per_tile(c_re), per_tile(-c_im)], axis=1)
    return b_grp, c_grp


def kernel(x, w_in, conv_w, conv_b, rg_wa, rg_ba, rg_wx, rg_bx, rg_lambda, w_a_out, ssm_a_re, ssm_a_im, ssm_log_dt, ssm_b_re, ssm_b_im, ssm_c_re, ssm_c_im, ssm_d, glu_w, glu_v, w_out, ln1_g, ln1_b, mlp_w_up, mlp_b_up, mlp_w_down, mlp_b_down, ln2_g, ln2_b):
    bsz, seq, _ = x.shape
    assert seq % PB == 0
    m = bsz * seq
    depth = w_in.shape[0]
    alpha = (2.0 * depth) ** 0.25
    perm = _interleave_matrix()
    perm_fwd = jnp.asarray(perm, BF16)
    perm_bwd = jnp.asarray(perm.T, BF16)
    for l in range(depth):
        xf = x.reshape(m, D_MODEL)
        xp = _interleave(xf, perm_fwd)
        z = _in_proj(xp, w_in[l], tm=2048, tn=1024)

        w_gate = jnp.concatenate([rg_wa[l], rg_wx[l]], axis=-1).astype(BF16)
        hg = _rglru(xp, w_in[l], conv_w[l], conv_b[l].reshape(1, D_RNN), w_gate,
                    rg_ba[l].reshape(1, D_RNN), rg_bx[l].reshape(1, D_RNN),
                    rg_lambda[l].reshape(1, D_RNN), bsz, seq)

        lbr, lbi, fr, fi = _s5_discretize(ssm_a_re[l], ssm_a_im[l], ssm_log_dt[l])
        b_grp, c_grp = _s5_group_matrices(fr, fi, ssm_b_re[l], ssm_b_im[l],
                                          ssm_c_re[l], ssm_c_im[l])
        y = _s5(z, b_grp, c_grp, lbr.reshape(1, -1), lbi.reshape(1, -1),
                ssm_d[l].reshape(1, D_SSM), bsz, seq)

        mix, w_up_bf, w_down_bf, w_out_bf = _mix(hg, y, z, w_a_out[l], glu_w[l], glu_v[l],
                                                 mlp_w_up[l], mlp_w_down[l], w_out[l],
                                                 tm=1024, tn=512)
        x1, x1b = _outproj_ln(mix, perm_bwd, xf, w_out_bf,
                              ln1_g[l].reshape(1, D_MODEL), ln1_b[l].reshape(1, D_MODEL),
                              alpha)
        a = _mlp_up(x1b, w_up_bf, mlp_b_up[l].reshape(1, D_FF), tm=1024, tn=2048)
        x2 = _mlp_down_ln(a, w_down_bf, x1,
                          mlp_b_down[l].reshape(1, D_MODEL),
                          ln2_g[l].reshape(1, D_MODEL), ln2_b[l].reshape(1, D_MODEL),
                          alpha, tm=1024, tk=2048)
        x = x2.reshape(bsz, seq, D_MODEL)
    return x
```

```python
import functools

import numpy as np
import jax
import jax.numpy as jnp
from jax.experimental import pallas as pl
from jax.experimental.pallas import tpu as pltpu

F32 = jnp.float32
BF16 = jnp.bfloat16

D_MODEL = 2048
D_RNN = D_MODEL
RG_HEADS = 16
RG_HEAD_DIM = D_RNN // RG_HEADS
CONV_WIDTH = 4
RG_C = 8.0
D_SSM = D_MODEL // 2
SSM_GROUP = 16
SSM_GROUPS = D_SSM // SSM_GROUP
SSM_STATE = 64
D_FF = 4 * D_MODEL
D_IN = 2 * D_RNN + D_SSM + 2 * D_MODEL
LN_EPS = 1e-5

SUBLANES = 8
BF16_ROWS = 2 * SUBLANES
VMEM_LIMIT = 56 * 1024 * 1024

NSEG = SUBLANES
PB = 512
SEG = PB // NSEG
HALO = (CONV_WIDTH - 1) * SUBLANES
ROW_CHUNK = 256

RG_TILE = 512
RG_GATE_ROWS = 32
S5_TILE = 256
S5_TILE_GROUPS = S5_TILE // SSM_GROUP
S5_TILE_STATES = S5_TILE_GROUPS * SSM_STATE
S5_SCAN_LANES = 256
W_COL_GATE = D_RNN
W_COL_REST = 2 * D_RNN
Z_COLS = D_SSM + 2 * D_MODEL
Z_COL_GA = D_SSM
Z_COL_GB = D_SSM + D_MODEL


def _params(*sem):
    return pltpu.CompilerParams(dimension_semantics=sem, vmem_limit_bytes=VMEM_LIMIT)


def _interleave_matrix():
    p = np.arange(PB)
    src = (p % NSEG) * SEG + p // NSEG
    mat = np.zeros((PB, PB), np.float32)
    mat[p, src] = 1.0
    return mat


def _row_ids(shape):
    return jax.lax.broadcasted_iota(jnp.int32, shape, 0)


def _shift_rows(v, d, fill):
    return jnp.where(_row_ids(v.shape) >= d, pltpu.roll(v, d, 0), fill)


def _interleave_kernel(x_ref, p_ref, o_ref):
    for s in range(x_ref.shape[0] // PB):
        rows = slice(s * PB, (s + 1) * PB)
        o_ref[rows, :] = jnp.dot(p_ref[...], x_ref[rows, :].astype(BF16),
                                 preferred_element_type=F32).astype(BF16)


def _interleave(x, perm, blocks_per_step=2):
    m, k = x.shape
    rows = blocks_per_step * PB
    return pl.pallas_call(
        _interleave_kernel,
        grid=(m // rows,),
        in_specs=[pl.BlockSpec((rows, k), lambda i: (i, 0)),
                  pl.BlockSpec((PB, PB), lambda i: (0, 0))],
        out_specs=pl.BlockSpec((rows, k), lambda i: (i, 0)),
        out_shape=jax.ShapeDtypeStruct((m, k), BF16),
        compiler_params=_params("parallel"),
        name="interleave",
    )(x, perm)


def _in_proj_kernel(x_ref, w_ref, o_ref, wb_ref):
    @pl.when(pl.program_id(1) == 0)
    def _():
        wb_ref[...] = w_ref[...].astype(BF16)

    o_ref[...] = jnp.dot(x_ref[...], wb_ref[...],
                         preferred_element_type=F32).astype(o_ref.dtype)


def _in_proj(xp, w, tm, tn):
    m, k = xp.shape
    col0 = W_COL_REST // tn
    return pl.pallas_call(
        _in_proj_kernel,
        grid=(Z_COLS // tn, m // tm),
        in_specs=[pl.BlockSpec((tm, k), lambda j, i: (i, 0)),
                  pl.BlockSpec((k, tn), lambda j, i: (0, col0 + j))],
        out_specs=pl.BlockSpec((tm, tn), lambda j, i: (i, j)),
        out_shape=jax.ShapeDtypeStruct((m, Z_COLS), BF16),
        scratch_shapes=[pltpu.VMEM((k, tn), BF16)],
        compiler_params=_params("parallel", "arbitrary"),
        name="in_proj",
    )(xp, w)


def _rglru_gates(hh, z_ref, cw_ref, cb_ref, w_ref, ext_ref, xc_ref, pre_ref, gg_ref):
    sl = slice(hh * RG_HEAD_DIM, (hh + 1) * RG_HEAD_DIM)
    gg_ref[:, sl] = jax.nn.gelu(
        z_ref[:, RG_TILE + hh * RG_HEAD_DIM:RG_TILE + (hh + 1) * RG_HEAD_DIM])
    grp = (SUBLANES, RG_HEAD_DIM)
    x = z_ref[:, sl]
    first = _row_ids(grp) == 0
    halos = []
    for k in range(CONV_WIDTH - 1):
        prev_g = ext_ref[PB + k * SUBLANES:PB + (k + 1) * SUBLANES, sl]
        cur_g = x[PB - HALO + k * SUBLANES:PB - HALO + (k + 1) * SUBLANES, :]
        halos.append(jnp.where(first, pltpu.roll(prev_g, 1, 0), pltpu.roll(cur_g, 1, 0)))
    for k in range(CONV_WIDTH - 1):
        ext_ref[k * SUBLANES:(k + 1) * SUBLANES, sl] = halos[k]
    ext_ref[HALO:HALO + PB, sl] = x

    xc = cb_ref[:, sl] + cw_ref[CONV_WIDTH - 1:CONV_WIDTH, sl] * x
    for k in range(CONV_WIDTH - 1):
        xc = xc + cw_ref[k:k + 1, sl] * ext_ref[k * SUBLANES:k * SUBLANES + PB, sl]

    xc_ref[:, sl] = xc
    pre_ref[:, 2 * hh * RG_HEAD_DIM:2 * (hh + 1) * RG_HEAD_DIM] = jnp.dot(
        xc.astype(BF16), w_ref[hh], preferred_element_type=F32)


def _rglru_head(hh, o_ref, row0, cvec, ba_ref, bx_ref, xc_ref, pre_ref, gg_ref,
                a_ref, b_ref, carry_ref):
    sl = slice(hh * RG_HEAD_DIM, (hh + 1) * RG_HEAD_DIM)
    grp = (SUBLANES, RG_HEAD_DIM)
    first = _row_ids(grp) == 0
    for c in range(PB // RG_GATE_ROWS):
        rows = slice(c * RG_GATE_ROWS, (c + 1) * RG_GATE_ROWS)
        xc = xc_ref[rows, sl]
        r = jax.nn.sigmoid(pre_ref[rows, 2 * hh * RG_HEAD_DIM:(2 * hh + 1) * RG_HEAD_DIM]
                           + ba_ref[:, sl])
        i = jax.nn.sigmoid(pre_ref[rows, (2 * hh + 1) * RG_HEAD_DIM:(2 * hh + 2) * RG_HEAD_DIM]
                           + bx_ref[:, sl])
        log_a = cvec[:, sl] * r
        a = jnp.exp(log_a)
        a_ref[rows, sl] = a
        one_minus_a2 = -jnp.tanh(log_a) * (1.0 + a * a)
        b_ref[rows, sl] = jnp.sqrt(one_minus_a2) * (i * xc)

    h_end, a_end = b_ref[0:SUBLANES, sl], a_ref[0:SUBLANES, sl]
    for tau in range(1, SEG):
        rows = slice(tau * SUBLANES, (tau + 1) * SUBLANES)
        a = a_ref[rows, sl]
        h_end = a * h_end + b_ref[rows, sl]
        a_end = a * a_end
        b_ref[rows, sl] = h_end
        a_ref[rows, sl] = a_end

    for d in (1, 2, 4):
        h_end = a_end * _shift_rows(h_end, d, 0.0) + h_end
        a_end = a_end * _shift_rows(a_end, d, 1.0)
    carry = carry_ref[:, sl]
    h_true = h_end + a_end * carry
    h_in = jnp.where(first, carry, pltpu.roll(h_true, 1, 0))
    carry_ref[:, sl] = h_true[SUBLANES - 1:SUBLANES, :]

    h_in2 = jnp.concatenate([h_in, h_in], axis=0)
    for j in range(PB // BF16_ROWS):
        rows = slice(j * BF16_ROWS, (j + 1) * BF16_ROWS)
        h = b_ref[rows, sl] + a_ref[rows, sl] * h_in2
        o_ref[row0 + j * BF16_ROWS:row0 + (j + 1) * BF16_ROWS, sl] = (
            h * gg_ref[rows, sl]).astype(o_ref.dtype)


def _rglru_kernel(x0_ref, x1_ref, x2_ref, wx_ref, wg_ref, cw_ref, cb_ref, w_ref,
                  ba_ref, bx_ref, lam_ref, o_ref,
                  wb_ref, z_ref, ext_ref, xca_ref, prea_ref, gga_ref, a_ref, b_ref, carry_ref):
    t = pl.program_id(2)
    n_heads = RG_TILE // RG_HEAD_DIM
    bufs = (xca_ref, prea_ref, gga_ref)

    def project(x_ref):
        z_ref[...] = jnp.dot(x_ref[...], wb_ref[...], preferred_element_type=F32)

    def gates():
        for hh in range(n_heads):
            _rglru_gates(hh, z_ref, cw_ref, cb_ref, w_ref, ext_ref, *bufs)

    @pl.when(jnp.logical_and(t == 0, pl.program_id(1) == 0))
    def _():
        wb_ref[:, :RG_TILE] = wx_ref[...].astype(BF16)
        wb_ref[:, RG_TILE:] = wg_ref[...].astype(BF16)

    @pl.when(t == 0)
    def _():
        ext_ref[PB:PB + HALO, :] = jnp.zeros((HALO, RG_TILE), F32)
        carry_ref[...] = jnp.zeros_like(carry_ref)
        project(x0_ref)
        gates()

    neg = -lam_ref[...]
    softplus = jnp.maximum(neg, 0.0) + jnp.log1p(jnp.exp(-jnp.abs(neg)))
    cvec = -RG_C * softplus

    def back(row0):
        for hh in range(n_heads):
            _rglru_head(hh, o_ref, row0, cvec, ba_ref, bx_ref, *bufs, a_ref, b_ref, carry_ref)

    project(x1_ref)
    back(0)
    gates()
    project(x2_ref)
    back(PB)
    gates()


def _rglru(xp, w_in, conv_w, conv_b, w_gate, ba, bx, lam, bsz, seq):
    n_t = seq // PB
    assert n_t % 2 == 0
    n_c = D_RNN // RG_TILE
    vec = lambda c, b, t: (0, c)
    return pl.pallas_call(
        _rglru_kernel,
        grid=(n_c, bsz, n_t // 2),
        in_specs=[pl.BlockSpec((PB, D_MODEL), lambda c, b, t: (b * n_t, 0)),
                  pl.BlockSpec((PB, D_MODEL), lambda c, b, t: (b * n_t + 2 * t + 1, 0)),
                  pl.BlockSpec((PB, D_MODEL),
                               lambda c, b, t: (b * n_t + jnp.minimum(2 * t + 2, n_t - 1), 0)),
                  pl.BlockSpec((D_MODEL, RG_TILE), vec),
                  pl.BlockSpec((D_MODEL, RG_TILE), lambda c, b, t: (0, W_COL_GATE // RG_TILE + c)),
                  pl.BlockSpec((CONV_WIDTH, RG_TILE), vec),
                  pl.BlockSpec((1, RG_TILE), vec),
                  pl.BlockSpec((RG_TILE // RG_HEAD_DIM, RG_HEAD_DIM, 2 * RG_HEAD_DIM),
                               lambda c, b, t: (c, 0, 0)),
                  pl.BlockSpec((1, RG_TILE), vec),
                  pl.BlockSpec((1, RG_TILE), vec),
                  pl.BlockSpec((1, RG_TILE), vec)],
        out_specs=pl.BlockSpec((2 * PB, RG_TILE), lambda c, b, t: (b * (n_t // 2) + t, c)),
        out_shape=jax.ShapeDtypeStruct((bsz * seq, D_RNN), BF16),
        scratch_shapes=[pltpu.VMEM((D_MODEL, 2 * RG_TILE), BF16),
                        pltpu.VMEM((PB, 2 * RG_TILE), F32),
                        pltpu.VMEM((HALO + PB, RG_TILE), F32),
                        pltpu.VMEM((PB, RG_TILE), F32),
                        pltpu.VMEM((PB, 2 * RG_TILE), F32),
                        pltpu.VMEM((PB, RG_TILE), F32),
                        pltpu.VMEM((PB, RG_TILE), F32),
                        pltpu.VMEM((PB, RG_TILE), F32),
                        pltpu.VMEM((1, RG_TILE), F32)],
        compiler_params=_params("parallel", "arbitrary", "arbitrary"),
        name="rglru",
    )(xp, xp, xp, w_in, w_in, conv_w, conv_b, w_gate, ba, bx, lam)


def _s5_discretize_kernel(are_ref, aim_ref, ldt_ref, lbr_ref, lbi_ref, fr_ref, fi_ref):
    dt = jnp.exp(ldt_ref[...])
    lr = jnp.minimum(are_ref[...], -1e-4)
    li = aim_ref[...]
    mag = jnp.exp(lr * dt)
    lbr = mag * jnp.cos(li * dt)
    lbi = mag * jnp.sin(li * dt)
    zr, zi = lbr - 1.0, lbi
    den = lr * lr + li * li
    lbr_ref[...] = lbr
    lbi_ref[...] = lbi
    fr_ref[...] = (zr * lr + zi * li) / den
    fi_ref[...] = (zi * lr - zr * li) / den


def _s5_discretize(a_re, a_im, log_dt):
    shp = jax.ShapeDtypeStruct((SSM_GROUPS, SSM_STATE), F32)
    return pl.pallas_call(
        _s5_discretize_kernel,
        out_shape=(shp, shp, shp, shp),
        name="s5_discretize",
    )(a_re, a_im, log_dt.reshape(SSM_GROUPS, 1))


def _cmul(ar, ai, br, bi):
    return ar * br - ai * bi, ar * bi + ai * br


def _s5_project(u_ref, bmat_ref, xr_ref, xi_ref, cols):
    ns = S5_TILE_STATES
    u = u_ref[...]
    xr_ref[:, cols] = jnp.dot(u, bmat_ref[:, cols], preferred_element_type=F32)
    xi_ref[:, cols] = jnp.dot(u, bmat_ref[:, ns + cols.start:ns + cols.stop],
                              preferred_element_type=F32)


def _s5_scan(cols, xr_ref, xi_ref, hb_ref, lre_ref, lim_ref, tab_re_ref, tab_im_ref,
             pow_re_ref, pow_im_ref, step_re_ref, step_im_ref, cr_ref, ci_ref):
    ns = S5_TILE_STATES
    grp = (SUBLANES, S5_SCAN_LANES)
    first = _row_ids(grp) == 0
    lr = jnp.broadcast_to(lre_ref[:, cols], grp)
    li = jnp.broadcast_to(lim_ref[:, cols], grp)

    er, ei = xr_ref[0:SUBLANES, cols], xi_ref[0:SUBLANES, cols]
    for tau in range(1, SEG):
        rows = slice(tau * SUBLANES, (tau + 1) * SUBLANES)
        nr = (lr * er - li * ei) + xr_ref[rows, cols]
        ni = (lr * ei + li * er) + xi_ref[rows, cols]
        xr_ref[rows, cols] = nr
        xi_ref[rows, cols] = ni
        er, ei = nr, ni

    for k, d in enumerate((1, 2, 4)):
        sr, si = pltpu.roll(er, d, 0), pltpu.roll(ei, d, 0)
        mr, mi = _cmul(step_re_ref[k, :, cols], step_im_ref[k, :, cols], sr, si)
        er, ei = er + mr, ei + mi
    cr, ci = cr_ref[:, cols], ci_ref[:, cols]
    mr, mi = _cmul(pow_re_ref[:, cols], pow_im_ref[:, cols], cr, ci)
    er, ei = er + mr, ei + mi
    in_r = jnp.where(first, cr, pltpu.roll(er, 1, 0))
    in_i = jnp.where(first, ci, pltpu.roll(ei, 1, 0))
    cr_ref[:, cols] = er[SUBLANES - 1:SUBLANES, :]
    ci_ref[:, cols] = ei[SUBLANES - 1:SUBLANES, :]

    in_r2 = jnp.concatenate([in_r, in_r], axis=0).astype(BF16)
    in_i2 = jnp.concatenate([in_i, in_i], axis=0).astype(BF16)
    for i in range(PB // BF16_ROWS):
        rows = slice(i * BF16_ROWS, (i + 1) * BF16_ROWS)
        fr, fi = _cmul(tab_re_ref[rows, cols], tab_im_ref[rows, cols], in_r2, in_i2)
        hb_ref[rows, cols] = xr_ref[rows, cols].astype(BF16) + fr
        hb_ref[rows, ns + cols.start:ns + cols.stop] = xi_ref[rows, cols].astype(BF16) + fi


def _s5_expand_weights(bc_ref, cc_ref, sb_ref, sc_ref, bmat_ref, cmat_ref):
    ns = S5_TILE_STATES

    def block_id(shape, dim, log2_block):
        return jax.lax.shift_right_logical(jax.lax.broadcasted_iota(jnp.int32, shape, dim),
                                           log2_block)

    log2_group = SSM_GROUP.bit_length() - 1
    log2_state = SSM_STATE.bit_length() - 1
    keep_b = block_id((S5_TILE, ns), 0, log2_group) == block_id((S5_TILE, ns), 1, log2_state)
    keep_c = block_id((ns, S5_TILE), 0, log2_state) == block_id((ns, S5_TILE), 1, log2_group)
    for part in range(2):
        b_grp = bc_ref[0, :, part * SSM_STATE:(part + 1) * SSM_STATE].astype(BF16)
        rep = jnp.dot(b_grp, sb_ref[...], preferred_element_type=F32)
        bmat_ref[:, part * ns:(part + 1) * ns] = jnp.where(keep_b, rep, 0.0).astype(BF16)
        c_grp = cc_ref[0, part * ns:(part + 1) * ns, :].astype(BF16)
        rep = jnp.dot(c_grp, sc_ref[...], preferred_element_type=F32)
        cmat_ref[part * ns:(part + 1) * ns, :] = jnp.where(keep_c, rep, 0.0).astype(BF16)


def _s5_kernel(u0_ref, uc_ref, un_ref, bc_ref, cc_ref, sb_ref, sc_ref, lre_ref, lim_ref, d_ref,
               o_ref, bmat_ref, cmat_ref,
               xr_ref, xi_ref, hb_ref, tab_re_ref, tab_im_ref, tabb_re_ref, tabb_im_ref,
               pow_re_ref, pow_im_ref, step_re_ref, step_im_ref, cr_ref, ci_ref):
    t = pl.program_id(2)
    ns = S5_TILE_STATES
    grp = (SUBLANES, ns)
    chunks = [slice(c0, c0 + S5_SCAN_LANES) for c0 in range(0, ns, S5_SCAN_LANES)]

    @pl.when(t == 0)
    def _():
        _s5_expand_weights(bc_ref, cc_ref, sb_ref, sc_ref, bmat_ref, cmat_ref)
        for cols in chunks:
            _s5_project(u0_ref, bmat_ref, xr_ref, xi_ref, cols)
        cr_ref[...] = jnp.zeros_like(cr_ref)
        ci_ref[...] = jnp.zeros_like(ci_ref)
        lr = jnp.broadcast_to(lre_ref[...], grp)
        li = jnp.broadcast_to(lim_ref[...], grp)

        def fill(tau, p):
            pr, pi = p
            rows = pl.ds(pl.multiple_of(tau * SUBLANES, SUBLANES), SUBLANES)
            tab_re_ref[rows, :] = pr
            tab_im_ref[rows, :] = pi
            return _cmul(pr, pi, lr, li)

        jax.lax.fori_loop(0, SEG, fill, (lr, li))
        tabb_re_ref[...] = tab_re_ref[...].astype(BF16)
        tabb_im_ref[...] = tab_im_ref[...].astype(BF16)
        mr = tab_re_ref[PB - SUBLANES:PB, :]
        mi = tab_im_ref[PB - SUBLANES:PB, :]
        row = _row_ids(grp)
        pr, pi = mr, mi
        for r in range(SUBLANES):
            pow_re_ref[r:r + 1, :] = pr[0:1, :]
            pow_im_ref[r:r + 1, :] = pi[0:1, :]
            if r + 1 in (1, 2, 4):
                k = (1, 2, 4).index(r + 1)
                step_re_ref[k] = jnp.where(row >= r + 1, pr, 0.0)
                step_im_ref[k] = jnp.where(row >= r + 1, pi, 0.0)
            pr, pi = _cmul(pr, pi, mr, mi)

    scan_refs = (lre_ref, lim_ref, tabb_re_ref, tabb_im_ref, pow_re_ref, pow_im_ref,
                 step_re_ref, step_im_ref, cr_ref, ci_ref)
    y = d_ref[...] * uc_ref[...].astype(F32)
    for cols in chunks:
        _s5_scan(cols, xr_ref, xi_ref, hb_ref, *scan_refs)
        y = y + jnp.dot(hb_ref[:, cols], cmat_ref[cols, :], preferred_element_type=F32)
        y = y + jnp.dot(hb_ref[:, ns + cols.start:ns + cols.stop],
                        cmat_ref[ns + cols.start:ns + cols.stop, :],
                        preferred_element_type=F32)
        _s5_project(un_ref, bmat_ref, xr_ref, xi_ref, cols)
    o_ref[...] = jax.nn.gelu(y).astype(o_ref.dtype)


def _s5(z, b_grp, c_grp, lam_re, lam_im, dvec, bsz, seq):
    n_t = seq // PB
    n_tiles = D_SSM // S5_TILE
    ns = S5_TILE_STATES
    vec = lambda b, c, t: (0, c)
    tile = lambda b, c, t: (c, 0, 0)
    fixed = lambda b, c, t: (0, 0)
    blk = lambda f: pl.BlockSpec((PB, S5_TILE), lambda b, c, t: (b * n_t + f(t), c))
    rep_b = jnp.asarray(np.tile(np.eye(SSM_STATE), (1, S5_TILE_GROUPS)), BF16)
    rep_c = jnp.asarray(np.tile(np.eye(SSM_GROUP), (1, S5_TILE_GROUPS)), BF16)
    return pl.pallas_call(
        _s5_kernel,
        grid=(bsz, n_tiles, n_t),
        in_specs=[blk(lambda t: 0),
                  blk(lambda t: t),
                  blk(lambda t: jnp.minimum(t + 1, n_t - 1)),
                  pl.BlockSpec((1, S5_TILE, 2 * SSM_STATE), tile),
                  pl.BlockSpec((1, 2 * ns, SSM_GROUP), tile),
                  pl.BlockSpec((SSM_STATE, ns), fixed),
                  pl.BlockSpec((SSM_GROUP, S5_TILE), fixed),
                  pl.BlockSpec((1, ns), vec),
                  pl.BlockSpec((1, ns), vec),
                  pl.BlockSpec((1, S5_TILE), vec)],
        out_specs=blk(lambda t: t),
        out_shape=jax.ShapeDtypeStruct((bsz * seq, D_SSM), BF16),
        scratch_shapes=[pltpu.VMEM((S5_TILE, 2 * ns), BF16),
                        pltpu.VMEM((2 * ns, S5_TILE), BF16),
                        pltpu.VMEM((PB, ns), F32),
                        pltpu.VMEM((PB, ns), F32),
                        pltpu.VMEM((PB, 2 * ns), BF16),
                        pltpu.VMEM((PB, ns), F32),
                        pltpu.VMEM((PB, ns), F32),
                        pltpu.VMEM((PB, ns), BF16),
                        pltpu.VMEM((PB, ns), BF16),
                        pltpu.VMEM((SUBLANES, ns), F32),
                        pltpu.VMEM((SUBLANES, ns), F32),
                        pltpu.VMEM((3, SUBLANES, ns), F32),
                        pltpu.VMEM((3, SUBLANES, ns), F32),
                        pltpu.VMEM((1, ns), F32),
                        pltpu.VMEM((1, ns), F32)],
        compiler_params=_params("parallel", "parallel", "arbitrary"),
        name="s5",
    )(z, z, z, b_grp, c_grp, rep_b, rep_c, lam_re, lam_im, dvec)


def _mix_kernel(hg_ref, y_ref, ga_ref, gb_ref, wa_ref, gw_ref, gv_ref, wdown_ref, wout_ref,
                o_ref, wdown_bf_ref, wout_bf_ref, wab_ref, gwb_ref, gvb_ref):
    @pl.when(pl.program_id(1) == 0)
    def _():
        wab_ref[...] = wa_ref[...].astype(BF16)
        gwb_ref[...] = gw_ref[...].astype(BF16)
        gvb_ref[...] = gv_ref[...].astype(BF16)

    wdown_bf_ref[...] = wdown_ref[...].astype(BF16)
    wout_bf_ref[...] = wout_ref[...].astype(BF16)

    for c in range(o_ref.shape[0] // ROW_CHUNK):
        rows = slice(c * ROW_CHUNK, (c + 1) * ROW_CHUNK)
        y_a = jnp.dot(hg_ref[rows, :], wab_ref[...], preferred_element_type=F32)
        y = y_ref[rows, :]
        y_b = (jnp.dot(y, gwb_ref[...], preferred_element_type=F32)
               * jax.nn.sigmoid(jnp.dot(y, gvb_ref[...], preferred_element_type=F32)))
        mix = (jax.nn.sigmoid(ga_ref[rows, :].astype(F32)) * y_a
               + jax.nn.sigmoid(gb_ref[rows, :].astype(F32)) * y_b)
        o_ref[rows, :] = mix.astype(o_ref.dtype)


def _mix(hg, y, z, w_a, glu_w, glu_v, w_down, w_out, tm, tn):
    m = hg.shape[0]
    n_i = m // tm
    steps = (D_MODEL // tn) * n_i
    slab = lambda rows: pl.BlockSpec((rows // steps, D_MODEL), lambda j, i: (j * n_i + i, 0))
    return pl.pallas_call(
        _mix_kernel,
        grid=(D_MODEL // tn, n_i),
        in_specs=[pl.BlockSpec((tm, D_RNN), lambda j, i: (i, 0)),
                  pl.BlockSpec((tm, D_SSM), lambda j, i: (i, 0)),
                  pl.BlockSpec((tm, tn), lambda j, i: (i, Z_COL_GA // tn + j)),
                  pl.BlockSpec((tm, tn), lambda j, i: (i, Z_COL_GB // tn + j)),
                  pl.BlockSpec((D_RNN, tn), lambda j, i: (0, j), pipeline_mode=pl.Buffered(1)),
                  pl.BlockSpec((D_SSM, tn), lambda j, i: (0, j), pipeline_mode=pl.Buffered(1)),
                  pl.BlockSpec((D_SSM, tn), lambda j, i: (0, j), pipeline_mode=pl.Buffered(1)),
                  slab(D_FF),
                  slab(D_MODEL)],
        out_specs=(pl.BlockSpec((tm, tn), lambda j, i: (i, j)), slab(D_FF), slab(D_MODEL)),
        out_shape=(jax.ShapeDtypeStruct((m, D_MODEL), BF16),
                   jax.ShapeDtypeStruct((D_FF, D_MODEL), BF16),
                   jax.ShapeDtypeStruct((D_MODEL, D_MODEL), BF16)),
        scratch_shapes=[pltpu.VMEM((D_RNN, tn), BF16),
                        pltpu.VMEM((D_SSM, tn), BF16),
                        pltpu.VMEM((D_SSM, tn), BF16)],
        compiler_params=_params("arbitrary", "arbitrary"),
        name="mix",
    )(hg, y, z, z, w_a, glu_w, glu_v, w_down, w_out)


def _layernorm(v, g, b):
    mu = jnp.mean(v, axis=-1, keepdims=True)
    c = v - mu
    var = jnp.mean(c * c, axis=-1, keepdims=True)
    return c * jax.lax.rsqrt(var + LN_EPS) * g + b


def _outproj_ln_kernel(mix_ref, pt_ref, x_ref, w_ref, g_ref, b_ref, o_ref, ob_ref, *, alpha):
    for c in range(PB // ROW_CHUNK):
        rows = slice(c * ROW_CHUNK, (c + 1) * ROW_CHUNK)
        mix = jnp.dot(pt_ref[rows, :], mix_ref[...], preferred_element_type=F32).astype(BF16)
        v = alpha * x_ref[rows, :] + jnp.dot(mix, w_ref[...], preferred_element_type=F32)
        out = _layernorm(v, g_ref[...], b_ref[...])
        o_ref[rows, :] = out
        ob_ref[rows, :] = out.astype(BF16)


def _outproj_ln(mix, perm_t, x, w_out, g, b, alpha):
    m = mix.shape[0]
    row = lambda i: (i, 0)
    fixed = lambda i: (0, 0)
    return pl.pallas_call(
        functools.partial(_outproj_ln_kernel, alpha=alpha),
        grid=(m // PB,),
        in_specs=[pl.BlockSpec((PB, D_MODEL), row),
                  pl.BlockSpec((PB, PB), fixed),
                  pl.BlockSpec((PB, D_MODEL), row),
                  pl.BlockSpec((D_MODEL, D_MODEL), fixed),
                  pl.BlockSpec((1, D_MODEL), fixed),
                  pl.BlockSpec((1, D_MODEL), fixed)],
        out_specs=(pl.BlockSpec((PB, D_MODEL), row), pl.BlockSpec((PB, D_MODEL), row)),
        out_shape=(jax.ShapeDtypeStruct((m, D_MODEL), F32),
                   jax.ShapeDtypeStruct((m, D_MODEL), BF16)),
        compiler_params=_params("parallel"),
        name="outproj_ln",
    )(mix, perm_t, x, w_out, g, b)


def _mlp_up_kernel(x_ref, w_ref, b_ref, o_ref, wb_ref):
    @pl.when(pl.program_id(1) == 0)
    def _():
        wb_ref[...] = w_ref[...].astype(BF16)

    v = jnp.dot(x_ref[...], wb_ref[...], preferred_element_type=F32) + b_ref[...]
    v = jnp.maximum(v, 0.0)
    o_ref[...] = (v * v).astype(o_ref.dtype)


def _mlp_up(xb, w_up, b_up, tm, tn):
    m = xb.shape[0]
    return pl.pallas_call(
        _mlp_up_kernel,
        grid=(D_FF // tn, m // tm),
        in_specs=[pl.BlockSpec((tm, D_MODEL), lambda j, i: (i, 0)),
                  pl.BlockSpec((D_MODEL, tn), lambda j, i: (0, j)),
                  pl.BlockSpec((1, tn), lambda j, i: (0, j))],
        out_specs=pl.BlockSpec((tm, tn), lambda j, i: (i, j)),
        out_shape=jax.ShapeDtypeStruct((m, D_FF), BF16),
        scratch_shapes=[pltpu.VMEM((D_MODEL, tn), BF16)],
        compiler_params=_params("parallel", "arbitrary"),
        name="mlp_up",
    )(xb, w_up, b_up)


def _mlp_down_ln_kernel(a_ref, w_ref, x_ref, bd_ref, g_ref, b_ref, o_ref, *, alpha):
    k = pl.program_id(1)
    last = pl.num_programs(1) - 1
    slab = x_ref.shape[0]
    slab_rows = pl.ds(pl.multiple_of(k * slab, slab), slab)

    def part():
        return jnp.dot(a_ref[...], w_ref[...], preferred_element_type=F32)

    @pl.when(k == 0)
    def _():
        o_ref[...] = part()
        o_ref[slab_rows, :] += alpha * x_ref[...]

    @pl.when(jnp.logical_and(k > 0, k < last))
    def _():
        o_ref[...] += part()
        o_ref[slab_rows, :] += alpha * x_ref[...]

    @pl.when(k == last)
    def _():
        o_ref[slab_rows, :] += alpha * x_ref[...]
        for c in range(o_ref.shape[0] // ROW_CHUNK):
            rows = slice(c * ROW_CHUNK, (c + 1) * ROW_CHUNK)
            v = (o_ref[rows, :] + jnp.dot(a_ref[rows, :], w_ref[...],
                                          preferred_element_type=F32)) + bd_ref[...]
            o_ref[rows, :] = _layernorm(v, g_ref[...], b_ref[...])


def _mlp_down_ln(a, w_down, x1, b_down, g, b, alpha, tm, tk):
    m = a.shape[0]
    n_k = D_FF // tk
    fixed = lambda i, k: (0, 0)
    return pl.pallas_call(
        functools.partial(_mlp_down_ln_kernel, alpha=alpha),
        grid=(m // tm, n_k),
        in_specs=[pl.BlockSpec((tm, tk), lambda i, k: (i, k)),
                  pl.BlockSpec((tk, D_MODEL), lambda i, k: (k, 0)),
                  pl.BlockSpec((tm // n_k, D_MODEL), lambda i, k: (i * n_k + k, 0)),
                  pl.BlockSpec((1, D_MODEL), fixed),
                  pl.BlockSpec((1, D_MODEL), fixed),
                  pl.BlockSpec((1, D_MODEL), fixed)],
        out_specs=pl.BlockSpec((tm, D_MODEL), lambda i, k: (i, 0)),
        out_shape=jax.ShapeDtypeStruct((m, D_MODEL), F32),
        compiler_params=_params("parallel", "arbitrary"),
        name="mlp_down_ln",
    )(a, w_down, x1, b_down, g, b)


def _s5_group_matrices(fr, fi, b_re, b_im, c_re, c_im):
    bbr = fr[..., None] * b_re - fi[..., None] * b_im
    bbi = fr[..., None] * b_im + fi[..., None] * b_re
    n_tiles = D_SSM // S5_TILE
    b_grp = jnp.concatenate([bbr.transpose(0, 2, 1), bbi.transpose(0, 2, 1)], axis=-1)
    b_grp = b_grp.reshape(n_tiles, S5_TILE, 2 * SSM_STATE)

    def per_tile(c):
        return c.transpose(0, 2, 1).reshape(n_tiles, S5_TILE_STATES, SSM_GROUP)

    c_grp = jnp.concatenate([per_tile(c_re), per_tile(-c_im)], axis=1)
    return b_grp, c_grp


def kernel(x, w_in, conv_w, conv_b, rg_wa, rg_ba, rg_wx, rg_bx, rg_lambda, w_a_out, ssm_a_re, ssm_a_im, ssm_log_dt, ssm_b_re, ssm_b_im, ssm_c_re, ssm_c_im, ssm_d, glu_w, glu_v, w_out, ln1_g, ln1_b, mlp_w_up, mlp_b_up, mlp_w_down, mlp_b_down, ln2_g, ln2_b):
    bsz, seq, _ = x.shape
    assert seq % PB == 0
    m = bsz * seq
    depth = w_in.shape[0]
    alpha = (2.0 * depth) ** 0.25
    perm = _interleave_matrix()
    perm_fwd = jnp.asarray(perm, BF16)
    perm_bwd = jnp.asarray(perm.T, BF16)
    for l in range(depth):
        xf = x.reshape(m, D_MODEL)
        xp = _interleave(xf, perm_fwd)
        z = _in_proj(xp, w_in[l], tm=2048, tn=1024)

        w_gate = jnp.concatenate([rg_wa[l], rg_wx[l]], axis=-1).astype(BF16)
        hg = _rglru(xp, w_in[l], conv_w[l], conv_b[l].reshape(1, D_RNN), w_gate,
                    rg_ba[l].reshape(1, D_RNN), rg_bx[l].reshape(1, D_RNN),
                    rg_lambda[l].reshape(1, D_RNN), bsz, seq)

        lbr, lbi, fr, fi = _s5_discretize(ssm_a_re[l], ssm_a_im[l], ssm_log_dt[l])
        b_grp, c_grp = _s5_group_matrices(fr, fi, ssm_b_re[l], ssm_b_im[l],
                                          ssm_c_re[l], ssm_c_im[l])
        y = _s5(z, b_grp, c_grp, lbr.reshape(1, -1), lbi.reshape(1, -1),
                ssm_d[l].reshape(1, D_SSM), bsz, seq)

        mix, w_down_bf, w_out_bf = _mix(hg, y, z, w_a_out[l], glu_w[l], glu_v[l],
                                        mlp_w_down[l], w_out[l], tm=512, tn=1024)
        x1, x1b = _outproj_ln(mix, perm_bwd, xf, w_out_bf,
                              ln1_g[l].reshape(1, D_MODEL), ln1_b[l].reshape(1, D_MODEL),
                              alpha)
        a = _mlp_up(x1b, mlp_w_up[l], mlp_b_up[l].reshape(1, D_FF), tm=2048, tn=1024)
        x2 = _mlp_down_ln(a, w_down_bf, x1,
                          mlp_b_down[l].reshape(1, D_MODEL),
                          ln2_g[l].reshape(1, D_MODEL), ln2_b[l].reshape(1, D_MODEL),
                          alpha, tm=1024, tk=2048)
        x = x2.reshape(bsz, seq, D_MODEL)
    return x
```
